```python
import jax, jax.numpy as jnp
from jax import lax
import numpy as np

D_MODEL = 2048
BATCH = 16
SEQ = 2048
DEPTH = 4

HEAD_DIM = 128
MIX_WIDTH = D_MODEL
N_HEADS_NSA = MIX_WIDTH // (2 * HEAD_DIM)
N_KV_NSA = 2
GQA_GROUP = N_HEADS_NSA // N_KV_NSA
N_HEADS_FOX = MIX_WIDTH // (2 * HEAD_DIM)
NSA_WIDTH = N_HEADS_NSA * HEAD_DIM
FOX_WIDTH = N_HEADS_FOX * HEAD_DIM
ROPE_DIM = HEAD_DIM // 4
ROPE_THETA = 500000.0
CMP_BLOCK = 32
CMP_STRIDE = 16
CMP_HIDDEN = 256
SLC_BLOCK = 64
SLC_TOPK = 8
N_LOCAL_SLC = 2
WINDOW = 512
Q_BLOCK = 128
SLC_Q_CHUNK = 32
D_FF = 5632
EPS = 1e-6
NEG_INF = -1e30
FORCED_SCORE = 1e9
N_NSA_BRANCHES = 3
COLS_NSA_Q = NSA_WIDTH
COLS_NSA_KV = N_NSA_BRANCHES * 2 * N_KV_NSA * HEAD_DIM
COLS_NSA_GATE = N_NSA_BRANCHES * N_HEADS_NSA
COLS_FOX_QKV = 3 * FOX_WIDTH
COLS_FOX_F = N_HEADS_FOX
PROJ_COLS = COLS_NSA_Q + COLS_NSA_KV + COLS_NSA_GATE + COLS_FOX_QKV + COLS_FOX_F
SPLITS = (COLS_NSA_Q,
          COLS_NSA_Q + COLS_NSA_KV,
          COLS_NSA_Q + COLS_NSA_KV + COLS_NSA_GATE,
          COLS_NSA_Q + COLS_NSA_KV + COLS_NSA_GATE + COLS_FOX_QKV)

kernel_name = "hymba_nsa_fox_macaron"


def rms_norm(x, g):
    xf = x.astype(jnp.float32)
    y = xf * lax.rsqrt(jnp.mean(xf * xf, axis=-1, keepdims=True) + EPS)
    return (y * g.astype(jnp.float32)).astype(x.dtype)


def swiglu(h, w_gate, w_up, w_down):
    return (jax.nn.silu(h @ w_gate) * (h @ w_up)) @ w_down


def rope_tables(pos):
    inv = ROPE_THETA ** (-jnp.arange(0, ROPE_DIM, 2, dtype=jnp.float32) / ROPE_DIM)
    ang = pos.astype(jnp.float32)[:, None] * inv[None, :]
    return jnp.cos(ang), jnp.sin(ang)


def partial_rope(x, cos, sin):
    half = ROPE_DIM // 2
    shape = (1, cos.shape[0]) + (1,) * (x.ndim - 3) + (half,)
    c = cos.reshape(shape).astype(x.dtype)
    s = sin.reshape(shape).astype(x.dtype)
    x1, x2, xp = x[..., :half], x[..., half:ROPE_DIM], x[..., ROPE_DIM:]
    return jnp.concatenate([x1 * c - x2 * s, x2 * c + x1 * s, xp], axis=-1)


def masked_probs(s, mask):
    s = jnp.where(mask, s, NEG_INF)
    m = jnp.max(s, axis=-1, keepdims=True)
    p = jnp.where(mask, jnp.exp(s - m), 0.0)
    return p / jnp.maximum(jnp.sum(p, axis=-1, keepdims=True), 1e-30)


def to_blocks(a, size):
    return a.reshape((a.shape[0], a.shape[1] // size, size) + a.shape[2:]).swapaxes(0, 1)


def from_blocks(a):
    a = a.swapaxes(0, 1)
    return a.reshape((a.shape[0], a.shape[1] * a.shape[2]) + a.shape[3:])


def compress_blocks(k, cmp_idx, pos_emb, w1, w2):
    B = k.shape[0]
    n_cmp = cmp_idx.shape[0]
    blocks = k[:, cmp_idx] + pos_emb[None, None, :, None, :]
    blocks = blocks.transpose(0, 1, 3, 2, 4).reshape(B, n_cmp, N_KV_NSA, CMP_BLOCK * HEAD_DIM)
    return jax.nn.gelu(blocks @ w1) @ w2


def nsa_attention(q, k_c, v_c, k_s, v_s, k_w, v_w, gates, k_norm, cmp_pos_emb, cmp_w1, cmp_w2):
    B, T = q.shape[:2]
    scale = HEAD_DIM ** -0.5
    pos = jnp.arange(T)
    cos, sin = rope_tables(pos)

    n_cmp = (T - CMP_BLOCK) // CMP_STRIDE + 1
    cmp_start = np.arange(n_cmp) * CMP_STRIDE
    cmp_idx = cmp_start[:, None] + np.arange(CMP_BLOCK)[None, :]
    cmp_end = cmp_start + CMP_BLOCK - 1
    kc = compress_blocks(k_c, cmp_idx, cmp_pos_emb[0], cmp_w1[0], cmp_w2[0])
    vc = compress_blocks(v_c, cmp_idx, cmp_pos_emb[1], cmp_w1[1], cmp_w2[1])
    cos_c, sin_c = rope_tables(jnp.asarray(cmp_end))
    kc = partial_rope(rms_norm(kc, k_norm[0]), cos_c, sin_c)
    s_c = jnp.einsum('btkgd,bckd->bkgtc', q, kc, preferred_element_type=jnp.float32) * scale
    p_c = masked_probs(s_c, jnp.asarray(cmp_end)[None, :] <= pos[:, None])
    o_cmp = jnp.einsum('bkgtc,bckd->btkgd', p_c.astype(vc.dtype), vc)

    n_slc = T // SLC_BLOCK
    top_n = min(SLC_TOPK, n_slc)
    slc_start = np.arange(n_slc) * SLC_BLOCK
    overlap = ((cmp_start[:, None] < slc_start[None, :] + SLC_BLOCK)
               & (cmp_start[:, None] + CMP_BLOCK > slc_start[None, :])).astype(np.float32)
    imp = jnp.einsum('bkgtc,cj->btkj', p_c, jnp.asarray(overlap))
    t_blk = pos // SLC_BLOCK
    j = jnp.arange(n_slc)
    causal_blk = j[None, :] <= t_blk[:, None]
    forced = (j[None, :] == 0) | (causal_blk & (j[None, :] > t_blk[:, None] - N_LOCAL_SLC))
    score = jnp.where(forced[None, :, None, :], FORCED_SCORE,
                      jnp.where(causal_blk[None, :, None, :], imp, NEG_INF))
    _, sel = lax.top_k(score, top_n)

    s_norm = rms_norm(k_s, k_norm[1])
    k_s = partial_rope(s_norm, cos, sin)
    kb = k_s.reshape(B, n_slc, SLC_BLOCK, N_KV_NSA, HEAD_DIM).transpose(0, 3, 1, 2, 4)
    vb = v_s.reshape(B, n_slc, SLC_BLOCK, N_KV_NSA, HEAD_DIM).transpose(0, 3, 1, 2, 4)
    b_idx = jnp.arange(B)[:, None, None, None]
    h_idx = jnp.arange(N_KV_NSA)[None, None, :, None]
    n_keys = top_n * SLC_BLOCK

    def slc_chunk(args):
        q_c, sel_c, t_c = args
        kg = kb[b_idx, h_idx, sel_c].reshape(B, SLC_Q_CHUNK, N_KV_NSA, n_keys, HEAD_DIM)
        vg = vb[b_idx, h_idx, sel_c].reshape(B, SLC_Q_CHUNK, N_KV_NSA, n_keys, HEAD_DIM)
        kpos = (sel_c[..., None] * SLC_BLOCK + jnp.arange(SLC_BLOCK)).reshape(B, SLC_Q_CHUNK, N_KV_NSA, n_keys)
        mask = (kpos <= t_c[None, :, None, None])[:, :, :, None, :]
        s = jnp.einsum('bckgd,bckjd->bckgj', q_c, kg, preferred_element_type=jnp.float32) * scale
        p = masked_probs(s, mask)
        return jnp.einsum('bckgj,bckjd->bckgd', p.astype(vg.dtype), vg)

    o_slc = from_blocks(lax.map(slc_chunk, (to_blocks(q, SLC_Q_CHUNK), to_blocks(sel, SLC_Q_CHUNK),
                                            pos.reshape(-1, SLC_Q_CHUNK))))

    k_w = partial_rope(rms_norm(k_w, k_norm[2]), cos, sin)
    kp = jnp.pad(k_w, ((0, 0), (WINDOW, 0), (0, 0), (0, 0)))
    vp = jnp.pad(v_w, ((0, 0), (WINDOW, 0), (0, 0), (0, 0)))
    band = WINDOW + Q_BLOCK

    def win_block(args):
        q_b, qb = args
        start = qb * Q_BLOCK
        k_b = lax.dynamic_slice_in_dim(kp, start, band, axis=1)
        v_b = lax.dynamic_slice_in_dim(vp, start, band, axis=1)
        kpos = start - WINDOW + jnp.arange(band)
        tq = start + jnp.arange(Q_BLOCK)
        mask = ((kpos[None, :] <= tq[:, None]) & (kpos[None, :] > tq[:, None] - WINDOW)
                & (kpos[None, :] >= 0))
        s = jnp.einsum('bqkgd,bskd->bkgqs', q_b, k_b, preferred_element_type=jnp.float32) * scale
        p = masked_probs(s, mask)
        return jnp.einsum('bkgqs,bskd->bqkgd', p.astype(v_b.dtype), v_b)

    o_win = from_blocks(lax.map(win_block, (to_blocks(q, Q_BLOCK), jnp.arange(T // Q_BLOCK))))

    return gates[..., 0:1] * o_cmp + gates[..., 1:2] * o_slc + gates[..., 2:3] * o_win


def forgetting_attention(q, k, v, log_f):
    T = q.shape[1]
    scale = HEAD_DIM ** -0.5
    c = jnp.cumsum(log_f, axis=1)
    c_keys = c.transpose(0, 2, 1)
    pos = jnp.arange(T)

    def block(args):
        q_b, c_b, t_b = args
        s = jnp.einsum('bqhd,bshd->bhqs', q_b, k, preferred_element_type=jnp.float32) * scale
        s = s + c_b.transpose(0, 2, 1)[..., None] - c_keys[:, :, None, :]
        p = masked_probs(s, pos[None, :] <= t_b[:, None])
        return jnp.einsum('bhqs,bshd->bqhd', p.astype(v.dtype), v)

    return from_blocks(lax.map(block, (to_blocks(q, Q_BLOCK), to_blocks(c, Q_BLOCK),
                                       pos.reshape(-1, Q_BLOCK))))


def hybrid_mixer(h, w_in, nsa_q_norm, nsa_k_norm, cmp_pos_emb, cmp_w1, cmp_w2, nsa_out_norm,
                 fox_q_norm, fox_k_norm, fox_forget_bias, fox_out_norm, w_out):
    B, T, _ = h.shape
    proj = h @ w_in
    q_n, kv_n, gate_n, fox_qkv, fox_f = jnp.split(proj, SPLITS, axis=-1)

    cos, sin = rope_tables(jnp.arange(T))
    q = q_n.reshape(B, T, N_KV_NSA, GQA_GROUP, HEAD_DIM)
    q = partial_rope(rms_norm(q, nsa_q_norm), cos, sin)
    kv = kv_n.reshape(B, T, N_NSA_BRANCHES, 2, N_KV_NSA, HEAD_DIM)
    gates = jax.nn.sigmoid(gate_n.astype(jnp.float32)).astype(h.dtype)
    gates = gates.reshape(B, T, N_KV_NSA, GQA_GROUP, N_NSA_BRANCHES)
    o_nsa = nsa_attention(q, kv[:, :, 0, 0], kv[:, :, 0, 1], kv[:, :, 1, 0], kv[:, :, 1, 1],
                          kv[:, :, 2, 0], kv[:, :, 2, 1], gates, nsa_k_norm,
                          cmp_pos_emb, cmp_w1, cmp_w2)
    o_nsa = rms_norm(o_nsa.reshape(B, T, NSA_WIDTH), nsa_out_norm)

    fqkv = fox_qkv.reshape(B, T, 3, N_HEADS_FOX, HEAD_DIM)
    fq = rms_norm(fqkv[:, :, 0], fox_q_norm)
    fk = rms_norm(fqkv[:, :, 1], fox_k_norm)
    log_f = jax.nn.log_sigmoid(fox_f.astype(jnp.float32) + fox_forget_bias.astype(jnp.float32))
    o_fox = forgetting_attention(fq, fk, fqkv[:, :, 2], log_f)
    o_fox = rms_norm(o_fox.reshape(B, T, FOX_WIDTH), fox_out_norm)

    return jnp.concatenate([o_nsa, o_fox], axis=-1) @ w_out


def setup_inputs(seed: int = 0) -> dict:
    key = jax.random.key(seed)
    ks = iter(jax.random.split(key, 32))

    def nrm(shape, scale):
        return scale * jax.random.normal(next(ks), shape, jnp.float32)

    def gain(shape):
        return 1.0 + 0.02 * jax.random.normal(next(ks), shape, jnp.float32)

    L = DEPTH
    return {
        "x": nrm((BATCH, SEQ, D_MODEL), 1.0),
        "ffn1_norm": gain((L, D_MODEL)),
        "ffn1_w_gate": nrm((L, D_MODEL, D_FF), D_MODEL ** -0.5),
        "ffn1_w_up": nrm((L, D_MODEL, D_FF), D_MODEL ** -0.5),
        "ffn1_w_down": nrm((L, D_FF, D_MODEL), D_FF ** -0.5),
        "mix_norm": gain((L, D_MODEL)),
        "w_in": nrm((L, D_MODEL, PROJ_COLS), D_MODEL ** -0.5),
        "nsa_q_norm": gain((L, HEAD_DIM)),
        "nsa_k_norm": gain((L, N_NSA_BRANCHES, HEAD_DIM)),
        "cmp_pos_emb": nrm((L, 2, CMP_BLOCK, HEAD_DIM), 0.1),
        "cmp_w1": nrm((L, 2, CMP_BLOCK * HEAD_DIM, CMP_HIDDEN), (CMP_BLOCK * HEAD_DIM) ** -0.5),
        "cmp_w2": nrm((L, 2, CMP_HIDDEN, HEAD_DIM), CMP_HIDDEN ** -0.5),
        "nsa_out_norm": gain((L, NSA_WIDTH)),
        "fox_q_norm": gain((L, HEAD_DIM)),
        "fox_k_norm": gain((L, HEAD_DIM)),
        "fox_forget_bias": jnp.linspace(1.0, 5.0, N_HEADS_FOX, dtype=jnp.float32)[None, :]
                           + nrm((L, N_HEADS_FOX), 0.1),
        "fox_out_norm": gain((L, FOX_WIDTH)),
        "w_out": nrm((L, MIX_WIDTH, D_MODEL), MIX_WIDTH ** -0.5),
        "ffn2_norm": gain((L, D_MODEL)),
        "ffn2_w_gate": nrm((L, D_MODEL, D_FF), D_MODEL ** -0.5),
        "ffn2_w_up": nrm((L, D_MODEL, D_FF), D_MODEL ** -0.5),
        "ffn2_w_down": nrm((L, D_FF, D_MODEL), D_FF ** -0.5),
    }


def reference(x, ffn1_norm, ffn1_w_gate, ffn1_w_up, ffn1_w_down, mix_norm, w_in, nsa_q_norm,
              nsa_k_norm, cmp_pos_emb, cmp_w1, cmp_w2, nsa_out_norm, fox_q_norm, fox_k_norm,
              fox_forget_bias, fox_out_norm, w_out, ffn2_norm, ffn2_w_gate, ffn2_w_up, ffn2_w_down):
    for i in range(DEPTH):
        x = x + 0.5 * swiglu(rms_norm(x, ffn1_norm[i]), ffn1_w_gate[i], ffn1_w_up[i], ffn1_w_down[i])
        x = x + hybrid_mixer(rms_norm(x, mix_norm[i]), w_in[i], nsa_q_norm[i], nsa_k_norm[i],
                             cmp_pos_emb[i], cmp_w1[i], cmp_w2[i], nsa_out_norm[i],
                             fox_q_norm[i], fox_k_norm[i], fox_forget_bias[i], fox_out_norm[i], w_out[i])
        x = x + 0.5 * swiglu(rms_norm(x, ffn2_norm[i]), ffn2_w_gate[i], ffn2_w_up[i], ffn2_w_down[i])
    return x
```

```python
import functools

import numpy as np
import jax
import jax.numpy as jnp
from jax import lax
from jax.experimental import pallas as pl
from jax.experimental.pallas import tpu as pltpu

F32 = jnp.float32
BF16 = jnp.bfloat16

HEAD_DIM = 128
N_HEADS_NSA = 8
N_KV_NSA = 2
GQA_GROUP = 4
N_HEADS_FOX = 8
NSA_WIDTH = N_HEADS_NSA * HEAD_DIM
FOX_WIDTH = N_HEADS_FOX * HEAD_DIM
ROPE_DIM = 32
ROPE_HALF = ROPE_DIM // 2
ROPE_THETA = 500000.0
CMP_BLOCK = 32
CMP_STRIDE = 16
CMP_HIDDEN = 256
SLC_BLOCK = 64
SLC_TOPK = 8
N_LOCAL_SLC = 2
WINDOW = 512
EPS = 1e-6
NEG_INF = -1e30
FORCED_SCORE = 1e9
SCALE = HEAD_DIM ** -0.5

COLS_NSA_Q = NSA_WIDTH
COLS_NSA_KV = 3 * 2 * N_KV_NSA * HEAD_DIM
COLS_NSA_GATE = 3 * N_HEADS_NSA
COLS_FOX_QKV = 3 * FOX_WIDTH
COLS_FOX_F = N_HEADS_FOX

PROJ_TN = 2 * HEAD_DIM
N_MAIN_TILES = (COLS_NSA_Q + COLS_NSA_KV + COLS_FOX_QKV) // PROJ_TN
N_PROJ_TILES = N_MAIN_TILES + 1
N_SLABS = 2 * N_PROJ_TILES
SLAB_Q = 0
SLAB_KV = 8
SLAB_FOX = 20
SLAB_SMALL = 44
LANE_LOGF = COLS_NSA_GATE

VMEM_LIMIT_BYTES = 56 * 1024 * 1024


def _cparams(sem):
    return pltpu.CompilerParams(dimension_semantics=sem, vmem_limit_bytes=VMEM_LIMIT_BYTES)


def _dot(a, b):
    return jnp.dot(a, b, preferred_element_type=F32)


def _dot_nt(a, b):
    return lax.dot_general(a, b, (((1,), (1,)), ((), ())), preferred_element_type=F32)


def _row_rms(x, width):
    return x * lax.rsqrt(jnp.sum(x * x, axis=-1, keepdims=True) * (1.0 / width) + EPS)


def _ffn_body(x_ref, g_ref, wg_ref, wu_ref, wd_ref, o_ref, h_ref):
    j = pl.program_id(1)

    @pl.when(j == 0)
    def _():
        x = x_ref[...]
        h_ref[...] = (_row_rms(x, x.shape[-1]) * g_ref[...]).astype(BF16)
        o_ref[...] = x

    h = h_ref[...]
    a = _dot(h, wg_ref[...])
    b = _dot(h, wu_ref[...])
    u = (0.5 * a * jax.nn.sigmoid(a)) * b
    o_ref[...] += _dot(u.astype(BF16), wd_ref[...])


def _ffn(x, g, wg, wu, wd, layer, tm=512, tf=512):
    n, d = x.shape
    f = wg.shape[-1]
    return pl.pallas_call(
        _ffn_body,
        grid=(n // tm, f // tf),
        in_specs=[
            pl.BlockSpec((tm, d), lambda i, j: (i, 0)),
            pl.BlockSpec((None, 1, d), lambda i, j: (layer, 0, 0)),
            pl.BlockSpec((None, d, tf), lambda i, j: (layer, 0, j)),
            pl.BlockSpec((None, d, tf), lambda i, j: (layer, 0, j)),
            pl.BlockSpec((None, tf, d), lambda i, j: (layer, j, 0)),
        ],
        out_specs=pl.BlockSpec((tm, d), lambda i, j: (i, 0)),
        out_shape=jax.ShapeDtypeStruct((n, d), F32),
        scratch_shapes=[pltpu.VMEM((tm, d), BF16)],
        compiler_params=_cparams(("parallel", "arbitrary")),
        name="ffn",
    )(x, g, wg, wu, wd)


def _rope(y, cos, s1, s2):
    return y * cos + pltpu.roll(y, HEAD_DIM - ROPE_HALF, 1) * s1 + pltpu.roll(y, ROPE_HALF, 1) * s2


def _proj_body(x_ref, g_ref, w_ref, gain_ref, cos_ref, s1_ref, s2_ref, bias_ref, o_ref, h_ref):
    j = pl.program_id(1)

    @pl.when(j == 0)
    def _():
        x = x_ref[...]
        h_ref[...] = (_row_rms(x, x.shape[-1]) * g_ref[...]).astype(BF16)

    acc = _dot(h_ref[...], w_ref[...])
    gain = gain_ref[...]

    is_rope = (j < 4) | (j == 6) | (j == 8)
    is_norm = (j >= 10) & (j < 18)
    is_small = j == N_MAIN_TILES
    is_raw = jnp.logical_not(is_rope | is_norm | is_small)

    def head(hh):
        sl = slice(hh * HEAD_DIM, (hh + 1) * HEAD_DIM)
        return acc[:, sl], gain[:, sl]

    @pl.when(is_rope)
    def _():
        for hh in range(2):
            y, gg = head(hh)
            y = _row_rms(y, HEAD_DIM) * gg
            o_ref[hh] = _rope(y, cos_ref[...], s1_ref[...], s2_ref[...])

    @pl.when(is_norm)
    def _():
        for hh in range(2):
            y, gg = head(hh)
            o_ref[hh] = _row_rms(y, HEAD_DIM) * gg

    @pl.when(is_raw)
    def _():
        for hh in range(2):
            o_ref[hh] = head(hh)[0]

    @pl.when(is_small)
    def _():
        y = acc[:, :HEAD_DIM]
        lane = lax.broadcasted_iota(jnp.int32, y.shape, 1)
        z = y + bias_ref[...]
        logsig = jnp.minimum(z, 0.0) - jnp.log1p(jnp.exp(-jnp.abs(z)))
        o_ref[0] = jnp.where(lane < LANE_LOGF, jax.nn.sigmoid(y), logsig)
        o_ref[1] = jnp.zeros_like(y)


def _proj(x, g, w_all, gains, cos, s1, s2, bias, layer, seq, tm=512):
    n, d = x.shape
    tpb = seq // tm
    return pl.pallas_call(
        _proj_body,
        grid=(n // tm, N_PROJ_TILES),
        in_specs=[
            pl.BlockSpec((tm, d), lambda i, j: (i, 0)),
            pl.BlockSpec((None, 1, d), lambda i, j: (layer, 0, 0)),
            pl.BlockSpec((None, d, PROJ_TN), lambda i, j: (layer, 0, j)),
            pl.BlockSpec((None, None, 1, PROJ_TN), lambda i, j: (layer, j, 0, 0)),
            pl.BlockSpec((tm, HEAD_DIM), lambda i, j: (i % tpb, 0)),
            pl.BlockSpec((tm, HEAD_DIM), lambda i, j: (i % tpb, 0)),
            pl.BlockSpec((tm, HEAD_DIM), lambda i, j: (i % tpb, 0)),
            pl.BlockSpec((None, 1, HEAD_DIM), lambda i, j: (layer, 0, 0)),
        ],
        out_specs=pl.BlockSpec((2, tm, HEAD_DIM), lambda i, j: (j, i, 0)),
        out_shape=jax.ShapeDtypeStruct((N_SLABS, n, HEAD_DIM), F32),
        scratch_shapes=[pltpu.VMEM((tm, d), BF16)],
        compiler_params=_cparams(("parallel", "arbitrary")),
        name="proj",
    )(x, g, w_all, gains, cos, s1, s2, bias)


def _cumsum_body(x_ref, c_ref, ct_ref):
    x = x_ref[0, 0]
    t = x.shape[0]
    row = lax.broadcasted_iota(jnp.int32, x.shape, 0)
    s = 1
    while s < t:
        x = x + jnp.where(row >= s, pltpu.roll(x, s, 0), 0.0)
        s *= 2
    c_ref[0] = x
    ct_ref[0] = x.T[LANE_LOGF:LANE_LOGF + N_HEADS_FOX, :]


def _cumsum(p4):
    _, b, t, _ = p4.shape
    return pl.pallas_call(
        _cumsum_body,
        grid=(b,),
        in_specs=[pl.BlockSpec((1, 1, t, HEAD_DIM), lambda i: (SLAB_SMALL, i, 0, 0))],
        out_specs=[
            pl.BlockSpec((1, t, HEAD_DIM), lambda i: (i, 0, 0)),
            pl.BlockSpec((1, N_HEADS_FOX, t), lambda i: (i, 0, 0)),
        ],
        out_shape=[
            jax.ShapeDtypeStruct((b, t, HEAD_DIM), F32),
            jax.ShapeDtypeStruct((b, N_HEADS_FOX, t), F32),
        ],
        compiler_params=_cparams(("parallel",)),
        name="fox_cumsum",
    )(p4)


def _gelu_tanh(x):
    c = float(np.sqrt(2.0 / np.pi))
    return x * (0.5 * (1.0 + jnp.tanh(c * (x + 0.044715 * (x * x * x)))))


def _compress_body(r_ref, pos_ref, w1_ref, w2_ref, kn_ref, cos_ref, s1_ref, s2_ref, o_ref):
    is_key = pl.program_id(0) == 0
    r = r_ref[0, 0]
    half = r.shape[-1]
    a = _dot((r + pos_ref[0]).astype(BF16), w1_ref[:half, :])
    bm = _dot((r + pos_ref[1]).astype(BF16), w1_ref[half:, :])
    h = a + pltpu.roll(bm, bm.shape[0] - 1, 0)
    y = _dot(_gelu_tanh(h).astype(BF16), w2_ref[...])

    @pl.when(is_key)
    def _():
        yk = _row_rms(y, HEAD_DIM) * kn_ref[...]
        o_ref[0, 0, 0] = _rope(yk, cos_ref[...], s1_ref[...], s2_ref[...])

    @pl.when(jnp.logical_not(is_key))
    def _():
        o_ref[0, 0, 0] = y


def _compress(pr, pos, w1, w2, knorm0, cos_c, s1_c, s2_c, layer):
    _, b, nr, width = pr.shape
    return pl.pallas_call(
        _compress_body,
        grid=(2, b, N_KV_NSA),
        in_specs=[
            pl.BlockSpec((1, 1, nr, width), lambda ty, i, k: (SLAB_KV + ty * 2 + k, i, 0, 0)),
            pl.BlockSpec((None, None, 2, 1, width), lambda ty, i, k: (layer, ty, 0, 0, 0)),
            pl.BlockSpec((None, None, 2 * width, CMP_HIDDEN), lambda ty, i, k: (layer, ty, 0, 0)),
            pl.BlockSpec((None, None, CMP_HIDDEN, HEAD_DIM), lambda ty, i, k: (layer, ty, 0, 0)),
            pl.BlockSpec((None, 1, HEAD_DIM), lambda ty, i, k: (layer, 0, 0)),
            pl.BlockSpec((nr, HEAD_DIM), lambda ty, i, k: (0, 0)),
            pl.BlockSpec((nr, HEAD_DIM), lambda ty, i, k: (0, 0)),
            pl.BlockSpec((nr, HEAD_DIM), lambda ty, i, k: (0, 0)),
        ],
        out_specs=pl.BlockSpec((1, 1, 1, nr, HEAD_DIM), lambda ty, i, k: (ty, i, k, 0, 0)),
        out_shape=jax.ShapeDtypeStruct((2, b, N_KV_NSA, nr, HEAD_DIM), F32),
        compiler_params=_cparams(("arbitrary", "arbitrary", "arbitrary")),
        name="compress",
    )(pr, pos, w1, w2, knorm0, cos_c, s1_c, s2_c)


def _split3(x):
    hi = x.astype(BF16)
    r1 = x - hi.astype(F32)
    mid = r1.astype(BF16)
    lo = (r1 - mid.astype(F32)).astype(BF16)
    return hi, mid, lo


def _cmp_body(q_ref, kc_ref, vc_ref, ov_ref, o_ref, sel_ref, *, tq, n_blk):
    i = pl.program_id(2)
    g = q_ref.shape[0]
    rows = g * tq
    q = q_ref[...].reshape(rows, HEAD_DIM).astype(BF16)
    kc = kc_ref[0, 0, 0].astype(BF16)
    vc = vc_ref[0, 0, 0].astype(BF16)
    s = _dot_nt(q, kc) * SCALE
    row_g = lax.broadcasted_iota(jnp.int32, (rows, HEAD_DIM), 0)
    lane_g = lax.broadcasted_iota(jnp.int32, (rows, HEAD_DIM), 1)
    t_g = i * tq + (row_g & (tq - 1))
    valid = (lane_g * CMP_STRIDE + (CMP_BLOCK - 1)) <= t_g
    s = jnp.where(valid, s, NEG_INF)
    m = jnp.max(s, axis=-1, keepdims=True)
    p = jnp.where(valid, jnp.exp(s - m), 0.0)
    p = p / jnp.maximum(jnp.sum(p, axis=-1, keepdims=True), 1e-30)
    o = _dot(p.astype(BF16), vc)
    psum = p[0:tq]
    for gg in range(g):
        o_ref[0, :, gg * HEAD_DIM:(gg + 1) * HEAD_DIM] = o[gg * tq:(gg + 1) * tq]
        if gg:
            psum = psum + p[gg * tq:(gg + 1) * tq]

    ov = ov_ref[...]
    hi, mid, lo = _split3(psum)
    imp = (_dot(hi, ov) + _dot(mid, ov)) + _dot(lo, ov)
    row = lax.broadcasted_iota(jnp.int32, (tq, HEAD_DIM), 0)
    lane = lax.broadcasted_iota(jnp.int32, (tq, HEAD_DIM), 1)
    t_blk = (i * tq + row) // SLC_BLOCK
    causal = lane <= t_blk
    forced = (lane == 0) | (causal & (lane > t_blk - N_LOCAL_SLC))
    score = jnp.where(forced, FORCED_SCORE, jnp.where(causal, imp, NEG_INF))
    score = jnp.where(lane < n_blk, score, -3e38)
    cnt = jnp.zeros((tq, HEAD_DIM), jnp.int32)
    for c in range(n_blk):
        col = score[:, c:c + 1]
        ahead = (col > score) | ((col == score) & (lane > c))
        cnt = cnt + ahead.astype(jnp.int32)
    sel_ref[0, 0] = jnp.where((cnt < SLC_TOPK) & (lane < n_blk), 1.0, 0.0)


def _cmp_select(p4, kvc, ov, tq=256):
    _, b, t, _ = p4.shape
    nr = kvc.shape[3]
    width = GQA_GROUP * HEAD_DIM
    return pl.pallas_call(
        functools.partial(_cmp_body, tq=tq, n_blk=t // SLC_BLOCK),
        grid=(b, N_KV_NSA, t // tq),
        in_specs=[
            pl.BlockSpec((GQA_GROUP, None, tq, HEAD_DIM), lambda bb, k, i: (k, bb, i, 0)),
            pl.BlockSpec((1, 1, 1, nr, HEAD_DIM), lambda bb, k, i: (0, bb, k, 0, 0)),
            pl.BlockSpec((1, 1, 1, nr, HEAD_DIM), lambda bb, k, i: (1, bb, k, 0, 0)),
            pl.BlockSpec((HEAD_DIM, HEAD_DIM), lambda bb, k, i: (0, 0)),
        ],
        out_specs=[
            pl.BlockSpec((1, tq, width), lambda bb, k, i: (bb, i, k)),
            pl.BlockSpec((1, 1, tq, HEAD_DIM), lambda bb, k, i: (bb, k, i, 0)),
        ],
        out_shape=[
            jax.ShapeDtypeStruct((b, t, NSA_WIDTH), F32),
            jax.ShapeDtypeStruct((b, N_KV_NSA, t, HEAD_DIM), F32),
        ],
        compiler_params=_cparams(("parallel", "parallel", "arbitrary")),
        name="cmp_select",
    )(p4, kvc, kvc, ov)


def _flash_body(*refs, mode, tq, tk):
    if mode == "slc":
        q_ref, k_ref, v_ref, sel_ref, e_ref, o_ref, m_ref, l_ref, acc_ref = refs
    elif mode == "win":
        q_ref, k_ref, v_ref, o_ref, m_ref, l_ref, acc_ref = refs
    else:
        q_ref, k_ref, v_ref, c_ref, ct_ref, o_ref, m_ref, l_ref, acc_ref = refs
    i = pl.program_id(2)
    g = q_ref.shape[0]
    rows = g * tq
    q = q_ref[...].reshape(rows, HEAD_DIM).astype(BF16)
    m_ref[...] = jnp.full(m_ref.shape, NEG_INF, F32)
    l_ref[...] = jnp.zeros(l_ref.shape, F32)
    acc_ref[...] = jnp.zeros(acc_ref.shape, F32)

    row = lax.broadcasted_iota(jnp.int32, (rows, tk), 0)
    lane = lax.broadcasted_iota(jnp.int32, (rows, tk), 1)
    t = i * tq + (row & (tq - 1))

    if mode == "slc":
        sel = sel_ref[0, 0].astype(BF16)
    if mode == "fox":
        hh = pl.program_id(1)
        lane_c = lax.broadcasted_iota(jnp.int32, (tq, HEAD_DIM), 1)
        cq = jnp.sum(jnp.where(lane_c == LANE_LOGF + hh, c_ref[0], 0.0), axis=-1, keepdims=True)

    def step(kc, carry):
        start = pl.multiple_of(kc * tk, tk)
        k = k_ref[0, 0, pl.ds(start, tk), :].astype(BF16)
        v = v_ref[0, 0, pl.ds(start, tk), :].astype(BF16)
        s = _dot_nt(q, k) * SCALE
        kpos = start + lane
        mask = kpos <= t
        if mode == "slc":
            picked = _dot(sel, e_ref[kc])
            mask = mask & (jnp.concatenate([picked] * g, axis=0) > 0.5)
        elif mode == "win":
            mask = mask & (kpos > t - WINDOW)
        else:
            s = s + cq - ct_ref[0, pl.ds(kc, 1), :]
        s = jnp.where(mask, s, NEG_INF)
        m_prev = m_ref[...]
        m_new = jnp.maximum(m_prev, jnp.max(s, axis=-1, keepdims=True))
        p = jnp.where(mask, jnp.exp(s - m_new), 0.0)
        alpha = jnp.exp(m_prev - m_new)
        l_ref[...] = alpha * l_ref[...] + jnp.sum(p, axis=-1, keepdims=True)
        acc_ref[...] = alpha * acc_ref[...] + _dot(p.astype(BF16), v)
        m_ref[...] = m_new
        return carry

    hi = ((i + 1) * tq) // tk
    lo = jnp.maximum(i * tq - WINDOW, 0) // tk if mode == "win" else 0
    lax.fori_loop(lo, hi, step, 0)

    o = acc_ref[...] / jnp.maximum(l_ref[...], 1e-30)
    for gg in range(g):
        o_ref[0, :, gg * HEAD_DIM:(gg + 1) * HEAD_DIM] = o[gg * tq:(gg + 1) * tq]


def _flash_scratch(rows):
    return [pltpu.VMEM((rows, 1), F32), pltpu.VMEM((rows, 1), F32), pltpu.VMEM((rows, HEAD_DIM), F32)]


def _nsa_flash(p4, mode, sel=None, e=None, tq=128, tk=128):
    _, b, t, _ = p4.shape
    branch = 1 if mode == "slc" else 2
    k_slab = SLAB_KV + branch * 4
    v_slab = k_slab + 2
    width = GQA_GROUP * HEAD_DIM
    in_specs = [
        pl.BlockSpec((GQA_GROUP, None, tq, HEAD_DIM), lambda bb, k, i: (k, bb, i, 0)),
        pl.BlockSpec((1, 1, t, HEAD_DIM), lambda bb, k, i: (k_slab + k, bb, 0, 0)),
        pl.BlockSpec((1, 1, t, HEAD_DIM), lambda bb, k, i: (v_slab + k, bb, 0, 0)),
    ]
    args = [p4, p4, p4]
    if mode == "slc":
        in_specs += [
            pl.BlockSpec((1, 1, tq, HEAD_DIM), lambda bb, k, i: (bb, k, i, 0)),
            pl.BlockSpec(e.shape, lambda bb, k, i: (0, 0, 0)),
        ]
        args += [sel, e]
    return pl.pallas_call(
        functools.partial(_flash_body, mode=mode, tq=tq, tk=tk),
        grid=(b, N_KV_NSA, t // tq),
        in_specs=in_specs,
        out_specs=pl.BlockSpec((1, tq, width), lambda bb, k, i: (bb, i, k)),
        out_shape=jax.ShapeDtypeStruct((b, t, NSA_WIDTH), F32),
        scratch_shapes=_flash_scratch(GQA_GROUP * tq),
        compiler_params=_cparams(("parallel", "parallel", "arbitrary")),
        name="nsa_" + mode,
    )(*args)


def _fox_flash(p4, c, ctr, tq=128, tk=128):
    _, b, t, _ = p4.shape
    nh = N_HEADS_FOX
    return pl.pallas_call(
        functools.partial(_flash_body, mode="fox", tq=tq, tk=tk),
        grid=(b, nh, t // tq),
        in_specs=[
            pl.BlockSpec((1, None, tq, HEAD_DIM), lambda bb, h, i: (SLAB_FOX + h, bb, i, 0)),
            pl.BlockSpec((1, 1, t, HEAD_DIM), lambda bb, h, i: (SLAB_FOX + nh + h, bb, 0, 0)),
            pl.BlockSpec((1, 1, t, HEAD_DIM), lambda bb, h, i: (SLAB_FOX + 2 * nh + h, bb, 0, 0)),
            pl.BlockSpec((1, tq, HEAD_DIM), lambda bb, h, i: (bb, i, 0)),
            pl.BlockSpec((1, t // tk, tk), lambda bb, h, i: (bb * nh + h, 0, 0)),
        ],
        out_specs=pl.BlockSpec((1, tq, HEAD_DIM), lambda bb, h, i: (bb, i, h)),
        out_shape=jax.ShapeDtypeStruct((b, t, FOX_WIDTH), F32),
        scratch_shapes=_flash_scratch(tq),
        compiler_params=_cparams(("parallel", "parallel", "arbitrary")),
        name="fox_attn",
    )(p4, p4, p4, c, ctr)


def _out_body(oc_ref, os_ref, ow_ref, of_ref, gt_ref, x_ref, nn_ref, fn_ref, w_ref, o_ref, cat_ref):
    gt = gt_ref[0]
    tm = gt.shape[0]
    ss = jnp.zeros((tm, 1), F32)
    for hq in range(N_HEADS_NSA):
        sl = slice(hq * HEAD_DIM, (hq + 1) * HEAD_DIM)
        y = (gt[:, 3 * hq:3 * hq + 1] * oc_ref[:, sl] + gt[:, 3 * hq + 1:3 * hq + 2] * os_ref[:, sl]
             + gt[:, 3 * hq + 2:3 * hq + 3] * ow_ref[:, sl])
        ss = ss + jnp.sum(y * y, axis=-1, keepdims=True)
        cat_ref[:, sl] = y
    inv = lax.rsqrt(ss * (1.0 / NSA_WIDTH) + EPS)
    nsa = (cat_ref[:, :NSA_WIDTH] * inv * nn_ref[...]).astype(BF16)
    fox = (_row_rms(of_ref[...], FOX_WIDTH) * fn_ref[...]).astype(BF16)
    o_ref[...] = x_ref[...] + (_dot(nsa, w_ref[:NSA_WIDTH, :]) + _dot(fox, w_ref[NSA_WIDTH:, :]))


def _out(oc, osl, ow, of, p3, x, nn, fn, w, layer, tm=256):
    n, d = x.shape
    return pl.pallas_call(
        _out_body,
        grid=(n // tm,),
        in_specs=[
            pl.BlockSpec((tm, NSA_WIDTH), lambda i: (i, 0)),
            pl.BlockSpec((tm, NSA_WIDTH), lambda i: (i, 0)),
            pl.BlockSpec((tm, NSA_WIDTH), lambda i: (i, 0)),
            pl.BlockSpec((tm, FOX_WIDTH), lambda i: (i, 0)),
            pl.BlockSpec((1, tm, HEAD_DIM), lambda i: (SLAB_SMALL, i, 0)),
            pl.BlockSpec((tm, d), lambda i: (i, 0)),
            pl.BlockSpec((None, 1, NSA_WIDTH), lambda i: (layer, 0, 0)),
            pl.BlockSpec((None, 1, FOX_WIDTH), lambda i: (layer, 0, 0)),
            pl.BlockSpec((None, NSA_WIDTH + FOX_WIDTH, d), lambda i: (layer, 0, 0)),
        ],
        out_specs=pl.BlockSpec((tm, d), lambda i: (i, 0)),
        out_shape=jax.ShapeDtypeStruct((n, d), F32),
        scratch_shapes=[pltpu.VMEM((tm, NSA_WIDTH), F32)],
        compiler_params=_cparams(("parallel",)),
        name="out_proj",
    )(oc, osl, ow, of, p3, x, nn, fn, w)


def _rope_tables(pos):
    inv = ROPE_THETA ** (-jnp.arange(0, ROPE_DIM, 2, dtype=F32) / ROPE_DIM)
    ang = pos.astype(F32)[:, None] * inv[None, :]
    cos, sin = jnp.cos(ang), jnp.sin(ang)
    n = pos.shape[0]
    pad = HEAD_DIM - ROPE_DIM
    cos_t = jnp.concatenate([cos, cos, jnp.ones((n, pad), F32)], axis=-1)
    s1 = jnp.concatenate([-sin, jnp.zeros((n, HEAD_DIM - ROPE_HALF), F32)], axis=-1)
    s2 = jnp.concatenate([jnp.zeros((n, ROPE_HALF), F32), sin, jnp.zeros((n, pad), F32)], axis=-1)
    return cos_t, s1, s2


def _overlap_matrix(t):
    n_cmp = (t - CMP_BLOCK) // CMP_STRIDE + 1
    n_slc = t // SLC_BLOCK
    cmp_start = np.arange(HEAD_DIM) * CMP_STRIDE
    slc_start = np.arange(HEAD_DIM) * SLC_BLOCK
    ov = ((cmp_start[:, None] < slc_start[None, :] + SLC_BLOCK)
          & (cmp_start[:, None] + CMP_BLOCK > slc_start[None, :]))
    ov = ov & (np.arange(HEAD_DIM)[:, None] < n_cmp) & (np.arange(HEAD_DIM)[None, :] < n_slc)
    return jnp.asarray(ov.astype(np.float32), BF16)


def _expand_matrix(t, tk):
    kpos = np.arange(t).reshape(t // tk, 1, tk)
    j = np.arange(HEAD_DIM).reshape(1, HEAD_DIM, 1)
    return jnp.asarray((kpos // SLC_BLOCK == j).astype(np.float32), BF16)


def _mixer(x2, b, t, layer, wts, tabs):
    n = x2.shape[0]
    p3 = _proj(x2, wts["mix_norm"], wts["w_all"], wts["gains"], tabs["cos"], tabs["s1"], tabs["s2"],
               wts["fbias"], layer, t)
    p4 = p3.reshape(N_SLABS, b, t, HEAD_DIM)
    pr = p3.reshape(N_SLABS, b, t // CMP_STRIDE, CMP_STRIDE * HEAD_DIM)
    c, ct = _cumsum(p4)
    ctr = ct.reshape(b * N_HEADS_FOX, t // 128, 128)
    kvc = _compress(pr, wts["pos"], wts["cmp_w1"], wts["cmp_w2"], wts["knorm0"],
                    tabs["cos_c"], tabs["s1_c"], tabs["s2_c"], layer)
    o_cmp, sel = _cmp_select(p4, kvc, tabs["ov"])
    o_slc = _nsa_flash(p4, "slc", sel, tabs["e"])
    o_win = _nsa_flash(p4, "win")
    o_fox = _fox_flash(p4, c, ctr)
    return _out(o_cmp.reshape(n, NSA_WIDTH), o_slc.reshape(n, NSA_WIDTH), o_win.reshape(n, NSA_WIDTH),
                o_fox.reshape(n, FOX_WIDTH), p3, x2, wts["nsa_out_norm"], wts["fox_out_norm"],
                wts["w_out"], layer)


def _prep_weights(ffn1_norm, ffn1_w_gate, ffn1_w_up, ffn1_w_down, mix_norm, w_in, nsa_q_norm, nsa_k_norm,
                  cmp_pos_emb, cmp_w1, cmp_w2, nsa_out_norm, fox_q_norm, fox_k_norm, fox_forget_bias,
                  fox_out_norm, w_out, ffn2_norm, ffn2_w_gate, ffn2_w_up, ffn2_w_down):
    depth, d, _ = w_in.shape
    c0 = COLS_NSA_Q + COLS_NSA_KV
    c1 = c0 + COLS_NSA_GATE
    c2 = c1 + COLS_FOX_QKV
    small = jnp.concatenate(
        [w_in[:, :, c0:c1], w_in[:, :, c2:], jnp.zeros((depth, d, PROJ_TN - COLS_NSA_GATE - COLS_FOX_F), F32)],
        axis=-1)
    w_all = jnp.concatenate([w_in[:, :, :c0], w_in[:, :, c1:c2], small], axis=-1).astype(BF16)

    ones = jnp.ones((depth, PROJ_TN), F32)

    def two(v):
        return jnp.concatenate([v, v], axis=-1)

    gains = [ones] * N_PROJ_TILES
    for j in range(4):
        gains[j] = two(nsa_q_norm)
    gains[6] = two(nsa_k_norm[:, 1])
    gains[8] = two(nsa_k_norm[:, 2])
    for j in range(10, 14):
        gains[j] = two(fox_q_norm)
    for j in range(14, 18):
        gains[j] = two(fox_k_norm)
    gains = jnp.stack(gains, axis=1)[:, :, None, :]

    fbias = jnp.zeros((depth, 1, HEAD_DIM), F32).at[:, 0, LANE_LOGF:LANE_LOGF + N_HEADS_FOX].set(
        fox_forget_bias)
    half = CMP_BLOCK * HEAD_DIM // 2
    pos = cmp_pos_emb.reshape(depth, 2, 2, 1, half)
    return dict(
        ffn1=(ffn1_norm[:, None, :], ffn1_w_gate.astype(BF16), ffn1_w_up.astype(BF16), ffn1_w_down.astype(BF16)),
        ffn2=(ffn2_norm[:, None, :], ffn2_w_gate.astype(BF16), ffn2_w_up.astype(BF16), ffn2_w_down.astype(BF16)),
        mix_norm=mix_norm[:, None, :], w_all=w_all, gains=gains, fbias=fbias, pos=pos,
        cmp_w1=cmp_w1.astype(BF16), cmp_w2=cmp_w2.astype(BF16), knorm0=nsa_k_norm[:, 0][:, None, :],
        nsa_out_norm=nsa_out_norm[:, None, :], fox_out_norm=fox_out_norm[:, None, :],
        w_out=w_out.astype(BF16),
    )


def _tables(t):
    cos, s1, s2 = _rope_tables(jnp.arange(t))
    cmp_end = jnp.arange(t // CMP_STRIDE) * CMP_STRIDE + (CMP_BLOCK - 1)
    cos_c, s1_c, s2_c = _rope_tables(cmp_end)
    return dict(cos=cos, s1=s1, s2=s2, cos_c=cos_c, s1_c=s1_c, s2_c=s2_c,
                ov=_overlap_matrix(t), e=_expand_matrix(t, 128))


def kernel(x, ffn1_norm, ffn1_w_gate, ffn1_w_up, ffn1_w_down, mix_norm, w_in, nsa_q_norm, nsa_k_norm, cmp_pos_emb, cmp_w1, cmp_w2, nsa_out_norm, fox_q_norm, fox_k_norm, fox_forget_bias, fox_out_norm, w_out, ffn2_norm, ffn2_w_gate, ffn2_w_up, ffn2_w_down):
    b, t, d = x.shape
    depth = w_in.shape[0]
    wts = _prep_weights(ffn1_norm, ffn1_w_gate, ffn1_w_up, ffn1_w_down, mix_norm, w_in, nsa_q_norm,
                        nsa_k_norm, cmp_pos_emb, cmp_w1, cmp_w2, nsa_out_norm, fox_q_norm, fox_k_norm,
                        fox_forget_bias, fox_out_norm, w_out, ffn2_norm, ffn2_w_gate, ffn2_w_up, ffn2_w_down)
    tabs = _tables(t)
    x2 = x.reshape(b * t, d)
    for layer in range(depth):
        x2 = _ffn(x2, *wts["ffn1"], layer)
        x2 = _mixer(x2, b, t, layer, wts, tabs)
        x2 = _ffn(x2, *wts["ffn2"], layer)
    return x2.reshape(b, t, d)
```

```python
import functools

import numpy as np
import jax
import jax.numpy as jnp
from jax import lax
from jax.experimental import pallas as pl
from jax.experimental.pallas import tpu as pltpu

F32 = jnp.float32
BF16 = jnp.bfloat16

HEAD_DIM = 128
N_HEADS_NSA = 8
N_KV_NSA = 2
GQA_GROUP = 4
N_HEADS_FOX = 8
NSA_WIDTH = N_HEADS_NSA * HEAD_DIM
FOX_WIDTH = N_HEADS_FOX * HEAD_DIM
ROPE_DIM = 32
ROPE_HALF = ROPE_DIM // 2
ROPE_THETA = 500000.0
CMP_BLOCK = 32
CMP_STRIDE = 16
CMP_HIDDEN = 256
SLC_BLOCK = 64
SLC_TOPK = 8
N_LOCAL_SLC = 2
WINDOW = 512
EPS = 1e-6
NEG_INF = -1e30
FORCED_SCORE = 1e9
SCALE = HEAD_DIM ** -0.5

COLS_NSA_Q = NSA_WIDTH
COLS_NSA_KV = 3 * 2 * N_KV_NSA * HEAD_DIM
COLS_NSA_GATE = 3 * N_HEADS_NSA
COLS_FOX_QKV = 3 * FOX_WIDTH
COLS_FOX_F = N_HEADS_FOX
COLS_MAIN = COLS_NSA_Q + COLS_NSA_KV + COLS_FOX_QKV

HEADS_PER_TILE = 4
PROJ_TN = HEADS_PER_TILE * HEAD_DIM
N_PROJ_TILES = COLS_MAIN // PROJ_TN
N_SLABS = COLS_MAIN // HEAD_DIM
SLAB_Q = 0
SLAB_KV = 8
SLAB_FOX = 20
TILE_CMP_KV = 2
LANE_LOGF = COLS_NSA_GATE

VMEM_LIMIT_BYTES = 56 * 1024 * 1024


def _cparams(n_axes):
    return pltpu.CompilerParams(dimension_semantics=("arbitrary",) * n_axes,
                                vmem_limit_bytes=VMEM_LIMIT_BYTES)


def _dot(a, b):
    return jnp.dot(a, b, preferred_element_type=F32)


def _dot_nt(a, b):
    return lax.dot_general(a, b, (((1,), (1,)), ((), ())), preferred_element_type=F32)


def _row_rms(x, width):
    return x * lax.rsqrt(jnp.sum(x * x, axis=-1, keepdims=True) * (1.0 / width) + EPS)


def _ffn_body(x_ref, g_ref, wg_ref, wu_ref, wd_ref, o_ref, h_ref):
    j = pl.program_id(1)

    @pl.when(j == 0)
    def _():
        x = x_ref[...]
        h_ref[...] = (_row_rms(x, x.shape[-1]) * g_ref[...]).astype(BF16)
        o_ref[...] = x

    h = h_ref[...]
    a = _dot(h, wg_ref[...])
    b = _dot(h, wu_ref[...])
    u = (0.5 * a * jax.nn.sigmoid(a)) * b
    o_ref[...] += _dot(u.astype(BF16), wd_ref[...])


def _ffn(x, g, wg, wu, wd, layer, tm=512, tf=512):
    n, d = x.shape
    f = wg.shape[-1]
    return pl.pallas_call(
        _ffn_body,
        grid=(n // tm, f // tf),
        in_specs=[
            pl.BlockSpec((tm, d), lambda i, j: (i, 0)),
            pl.BlockSpec((None, 1, d), lambda i, j: (layer, 0, 0)),
            pl.BlockSpec((None, d, tf), lambda i, j: (layer, 0, j)),
            pl.BlockSpec((None, d, tf), lambda i, j: (layer, 0, j)),
            pl.BlockSpec((None, tf, d), lambda i, j: (layer, j, 0)),
        ],
        out_specs=pl.BlockSpec((tm, d), lambda i, j: (i, 0)),
        out_shape=jax.ShapeDtypeStruct((n, d), F32),
        scratch_shapes=[pltpu.VMEM((tm, d), BF16)],
        compiler_params=_cparams(2),
        name="ffn",
    )(x, g, wg, wu, wd)


def _rope(y, cos, s1, s2):
    return y * cos + pltpu.roll(y, HEAD_DIM - ROPE_HALF, 1) * s1 + pltpu.roll(y, ROPE_HALF, 1) * s2


def _proj_body(x_ref, g_ref, w_ref, ws_ref, gain_ref, cos_ref, s1_ref, s2_ref, bias_ref,
               p_ref, kc_ref, sm_ref, h_ref):
    j = pl.program_id(1)

    @pl.when(j == 0)
    def _():
        x = x_ref[...]
        h = (_row_rms(x, x.shape[-1]) * g_ref[...]).astype(BF16)
        h_ref[...] = h
        y = _dot(h, ws_ref[...])
        lane = lax.broadcasted_iota(jnp.int32, y.shape, 1)
        z = y + bias_ref[...]
        logsig = jnp.minimum(z, 0.0) - jnp.log1p(jnp.exp(-jnp.abs(z)))
        sm_ref[...] = jnp.where(lane < LANE_LOGF, jax.nn.sigmoid(y), logsig)

    acc = _dot(h_ref[...], w_ref[...])
    gain = gain_ref[...]

    def raw(hh):
        return acc[:, hh * HEAD_DIM:(hh + 1) * HEAD_DIM]

    def normed(hh):
        return _row_rms(raw(hh), HEAD_DIM) * gain[:, hh * HEAD_DIM:(hh + 1) * HEAD_DIM]

    def roped(hh):
        return _rope(normed(hh), cos_ref[...], s1_ref[...], s2_ref[...])

    def emit(fns):
        for hh, fn in enumerate(fns):
            p_ref[hh] = fn(hh).astype(p_ref.dtype)

    @pl.when(j < 2)
    def _():
        emit([roped] * 4)

    @pl.when(j == TILE_CMP_KV)
    def _():
        emit([raw] * 4)
        for hh in range(HEADS_PER_TILE):
            kc_ref[hh] = raw(hh)

    @pl.when((j == 3) | (j == 4))
    def _():
        emit([roped, roped, raw, raw])

    @pl.when((j >= 5) & (j < 9))
    def _():
        emit([normed] * 4)

    @pl.when(j >= 9)
    def _():
        emit([raw] * 4)


def _proj(x, g, w_main, w_small, gains, cos, s1, s2, bias, layer, seq, tm=1024):
    n, d = x.shape
    tm = min(tm, seq)
    tpb = seq // tm
    hp = HEADS_PER_TILE
    return pl.pallas_call(
        _proj_body,
        grid=(n // tm, N_PROJ_TILES),
        in_specs=[
            pl.BlockSpec((tm, d), lambda i, j: (i, 0)),
            pl.BlockSpec((None, 1, d), lambda i, j: (layer, 0, 0)),
            pl.BlockSpec((None, d, PROJ_TN), lambda i, j: (layer, 0, j)),
            pl.BlockSpec((None, d, HEAD_DIM), lambda i, j: (layer, 0, 0)),
            pl.BlockSpec((None, None, 1, PROJ_TN), lambda i, j: (layer, j, 0, 0)),
            pl.BlockSpec((tm, HEAD_DIM), lambda i, j: (i % tpb, 0)),
            pl.BlockSpec((tm, HEAD_DIM), lambda i, j: (i % tpb, 0)),
            pl.BlockSpec((tm, HEAD_DIM), lambda i, j: (i % tpb, 0)),
            pl.BlockSpec((None, 1, HEAD_DIM), lambda i, j: (layer, 0, 0)),
        ],
        out_specs=[
            pl.BlockSpec((hp, tm, HEAD_DIM), lambda i, j: (j, i, 0)),
            pl.BlockSpec((hp, tm, HEAD_DIM), lambda i, j: (0, i, 0)),
            pl.BlockSpec((tm, HEAD_DIM), lambda i, j: (i, 0)),
        ],
        out_shape=[
            jax.ShapeDtypeStruct((N_SLABS, n, HEAD_DIM), BF16),
            jax.ShapeDtypeStruct((hp, n, HEAD_DIM), F32),
            jax.ShapeDtypeStruct((n, HEAD_DIM), F32),
        ],
        scratch_shapes=[pltpu.VMEM((tm, d), BF16)],
        compiler_params=_cparams(2),
        name="proj",
    )(x, g, w_main, w_small, gains, cos, s1, s2, bias)


def _cumsum_body(x_ref, ct_ref):
    x = x_ref[0]
    t = x.shape[0]
    row = lax.broadcasted_iota(jnp.int32, x.shape, 0)
    s = 1
    while s < t:
        x = x + jnp.where(row >= s, pltpu.roll(x, s, 0), 0.0)
        s *= 2
    ct_ref[0] = x.T[LANE_LOGF:LANE_LOGF + N_HEADS_FOX, :]


def _cumsum(sm3):
    b, t, _ = sm3.shape
    return pl.pallas_call(
        _cumsum_body,
        grid=(b,),
        in_specs=[pl.BlockSpec((1, t, HEAD_DIM), lambda i: (i, 0, 0))],
        out_specs=pl.BlockSpec((1, N_HEADS_FOX, t), lambda i: (i, 0, 0)),
        out_shape=jax.ShapeDtypeStruct((b, N_HEADS_FOX, t), F32),
        compiler_params=_cparams(1),
        name="fox_cumsum",
    )(sm3)


def _gelu_tanh(x):
    c = float(np.sqrt(2.0 / np.pi))
    return x * (0.5 * (1.0 + jnp.tanh(c * (x + 0.044715 * (x * x * x)))))


def _compress_body(r_ref, pos_ref, w1_ref, w2_ref, kn_ref, cos_ref, s1_ref, s2_ref, o_ref):
    is_key = pl.program_id(0) == 0
    r = r_ref[0, 0]
    half = r.shape[-1]
    a = _dot((r + pos_ref[0]).astype(BF16), w1_ref[:half, :])
    bm = _dot((r + pos_ref[1]).astype(BF16), w1_ref[half:, :])
    h = a + pltpu.roll(bm, bm.shape[0] - 1, 0)
    y = _dot(_gelu_tanh(h).astype(BF16), w2_ref[...])

    @pl.when(is_key)
    def _():
        yk = _row_rms(y, HEAD_DIM) * kn_ref[...]
        o_ref[0, 0, 0] = _rope(yk, cos_ref[...], s1_ref[...], s2_ref[...]).astype(o_ref.dtype)

    @pl.when(jnp.logical_not(is_key))
    def _():
        o_ref[0, 0, 0] = y.astype(o_ref.dtype)


def _compress(kr, pos, w1, w2, knorm0, cos_c, s1_c, s2_c, layer):
    _, b, nr, width = kr.shape
    return pl.pallas_call(
        _compress_body,
        grid=(2, b, N_KV_NSA),
        in_specs=[
            pl.BlockSpec((1, 1, nr, width), lambda ty, i, k: (ty * 2 + k, i, 0, 0)),
            pl.BlockSpec((None, None, 2, 1, width), lambda ty, i, k: (layer, ty, 0, 0, 0)),
            pl.BlockSpec((None, None, 2 * width, CMP_HIDDEN), lambda ty, i, k: (layer, ty, 0, 0)),
            pl.BlockSpec((None, None, CMP_HIDDEN, HEAD_DIM), lambda ty, i, k: (layer, ty, 0, 0)),
            pl.BlockSpec((None, 1, HEAD_DIM), lambda ty, i, k: (layer, 0, 0)),
            pl.BlockSpec((nr, HEAD_DIM), lambda ty, i, k: (0, 0)),
            pl.BlockSpec((nr, HEAD_DIM), lambda ty, i, k: (0, 0)),
            pl.BlockSpec((nr, HEAD_DIM), lambda ty, i, k: (0, 0)),
        ],
        out_specs=pl.BlockSpec((1, 1, 1, nr, HEAD_DIM), lambda ty, i, k: (ty, i, k, 0, 0)),
        out_shape=jax.ShapeDtypeStruct((2, b, N_KV_NSA, nr, HEAD_DIM), BF16),
        compiler_params=_cparams(3),
        name="compress",
    )(kr, pos, w1, w2, knorm0, cos_c, s1_c, s2_c)


def _split3(x):
    hi = x.astype(BF16)
    r1 = x - hi.astype(F32)
    mid = r1.astype(BF16)
    lo = (r1 - mid.astype(F32)).astype(BF16)
    return hi, mid, lo


def _cmp_body(q_ref, kc_ref, vc_ref, ov_ref, o_ref, sel_ref, *, tq, n_blk):
    i = pl.program_id(2)
    g = q_ref.shape[0]
    rows = g * tq
    q = q_ref[...].reshape(rows, HEAD_DIM)
    kc = kc_ref[0, 0, 0]
    vc = vc_ref[0, 0, 0]
    s = _dot_nt(q, kc)
    row_g = lax.broadcasted_iota(jnp.int32, (rows, HEAD_DIM), 0)
    lane_g = lax.broadcasted_iota(jnp.int32, (rows, HEAD_DIM), 1)
    t_g = i * tq + (row_g & (tq - 1))
    valid = (lane_g * CMP_STRIDE + (CMP_BLOCK - 1)) <= t_g
    s = jnp.where(valid, s, NEG_INF)
    m = jnp.max(s, axis=-1, keepdims=True)
    p = jnp.where(valid, jnp.exp(s - m), 0.0)
    p = p / jnp.maximum(jnp.sum(p, axis=-1, keepdims=True), 1e-30)
    o = _dot(p.astype(BF16), vc)
    psum = p[0:tq]
    for gg in range(g):
        o_ref[0, :, gg * HEAD_DIM:(gg + 1) * HEAD_DIM] = o[gg * tq:(gg + 1) * tq]
        if gg:
            psum = psum + p[gg * tq:(gg + 1) * tq]

    ov = ov_ref[...]
    hi, mid, lo = _split3(psum)
    imp = (_dot(hi, ov) + _dot(mid, ov)) + _dot(lo, ov)
    row = lax.broadcasted_iota(jnp.int32, (tq, HEAD_DIM), 0)
    lane = lax.broadcasted_iota(jnp.int32, (tq, HEAD_DIM), 1)
    t_blk = (i * tq + row) // SLC_BLOCK
    causal = lane <= t_blk
    forced = (lane == 0) | (causal & (lane > t_blk - N_LOCAL_SLC))
    score = jnp.where(forced, FORCED_SCORE, jnp.where(causal, imp, NEG_INF))
    score = jnp.where(lane < n_blk, score, -3e38)
    cnt = jnp.zeros((tq, HEAD_DIM), jnp.int32)
    for c in range(n_blk):
        col = score[:, c:c + 1]
        ahead = (col > score) | ((col == score) & (lane > c))
        cnt = cnt + ahead.astype(jnp.int32)
    sel_ref[0, 0] = jnp.where((cnt < SLC_TOPK) & (lane < n_blk), 1.0, 0.0).astype(sel_ref.dtype)


def _cmp_select(p4, kvc, ov, tq=256):
    _, b, t, _ = p4.shape
    nr = kvc.shape[3]
    width = GQA_GROUP * HEAD_DIM
    return pl.pallas_call(
        functools.partial(_cmp_body, tq=tq, n_blk=t // SLC_BLOCK),
        grid=(b, N_KV_NSA, t // tq),
        in_specs=[
            pl.BlockSpec((GQA_GROUP, None, tq, HEAD_DIM), lambda bb, k, i: (k, bb, i, 0)),
            pl.BlockSpec((1, 1, 1, nr, HEAD_DIM), lambda bb, k, i: (0, bb, k, 0, 0)),
            pl.BlockSpec((1, 1, 1, nr, HEAD_DIM), lambda bb, k, i: (1, bb, k, 0, 0)),
            pl.BlockSpec((HEAD_DIM, HEAD_DIM), lambda bb, k, i: (0, 0)),
        ],
        out_specs=[
            pl.BlockSpec((1, tq, width), lambda bb, k, i: (bb, i, k)),
            pl.BlockSpec((1, 1, tq, HEAD_DIM), lambda bb, k, i: (bb, k, i, 0)),
        ],
        out_shape=[
            jax.ShapeDtypeStruct((b, t, NSA_WIDTH), F32),
            jax.ShapeDtypeStruct((b, N_KV_NSA, t, HEAD_DIM), BF16),
        ],
        compiler_params=_cparams(3),
        name="cmp_select",
    )(p4, kvc, kvc, ov)


def _online_update(s, v, m_ref, l_ref, acc_ref):
    cols = [s[:, c:c + HEAD_DIM] for c in range(0, s.shape[1], HEAD_DIM)]
    m_el = functools.reduce(jnp.maximum, cols)
    m_prev = m_ref[...]
    m_new = jnp.maximum(m_prev, jnp.max(m_el, axis=-1, keepdims=True))
    alpha = jnp.exp(m_prev - m_new)
    ps = [jnp.exp(c - m_new) for c in cols]
    l_ref[...] = alpha * l_ref[...] + functools.reduce(jnp.add, ps)
    p = jnp.concatenate([x.astype(BF16) for x in ps], axis=1)
    acc_ref[...] = alpha * acc_ref[...] + _dot(p, v)
    m_ref[...] = m_new


def _normalize(acc_ref, l_ref):
    l = jnp.sum(l_ref[...], axis=-1, keepdims=True)
    return acc_ref[...] / jnp.maximum(l, 1e-30)


def _init_online(m_ref, l_ref, acc_ref):
    m_ref[...] = jnp.full(m_ref.shape, NEG_INF, F32)
    l_ref[...] = jnp.zeros(l_ref.shape, F32)
    acc_ref[...] = jnp.zeros(acc_ref.shape, F32)


def _slc_body(q_ref, k_ref, v_ref, sel_ref, e_ref, o_ref, m_ref, l_ref, acc_ref, *, tq, tk):
    i = pl.program_id(2)
    g = q_ref.shape[0]
    rows = g * tq
    q = q_ref[...].reshape(rows, HEAD_DIM)
    sel = sel_ref[0, 0]
    _init_online(m_ref, l_ref, acc_ref)
    row = lax.broadcasted_iota(jnp.int32, (tq, tk), 0)
    col = lax.broadcasted_iota(jnp.int32, (tq, tk), 1)
    last = (i * tq) // tk

    def step(kc, diagonal):
        start = pl.multiple_of(kc * tk, tk)
        k = k_ref[0, 0, pl.ds(start, tk), :]
        v = v_ref[0, 0, pl.ds(start, tk), :]
        live = _dot(sel, e_ref[kc]) > 0.5
        if diagonal:
            live = live & ((start + col) <= (i * tq + row))
        bias = jnp.where(live, 0.0, NEG_INF)
        s = _dot_nt(q, k).reshape(g, tq, tk) + bias[None]
        _online_update(s.reshape(rows, tk), v, m_ref, l_ref, acc_ref)

    def body(kc, carry):
        step(kc, False)
        return carry

    lax.fori_loop(0, last, body, 0)
    step(last, True)
    o = _normalize(acc_ref, l_ref)
    for gg in range(g):
        o_ref[0, :, gg * HEAD_DIM:(gg + 1) * HEAD_DIM] = o[gg * tq:(gg + 1) * tq]


def _flash_scratch(rows):
    return [pltpu.VMEM((rows, HEAD_DIM), F32)] * 3


def _kv_specs(t, k_slab, v_slab):
    return [
        pl.BlockSpec((1, 1, t, HEAD_DIM), lambda bb, k, i: (k_slab + k, bb, 0, 0)),
        pl.BlockSpec((1, 1, t, HEAD_DIM), lambda bb, k, i: (v_slab + k, bb, 0, 0)),
    ]


def _nsa_slc(p4, sel, e, tq=256):
    _, b, t, _ = p4.shape
    tk = e.shape[-1]
    width = GQA_GROUP * HEAD_DIM
    return pl.pallas_call(
        functools.partial(_slc_body, tq=tq, tk=tk),
        grid=(b, N_KV_NSA, t // tq),
        in_specs=[pl.BlockSpec((GQA_GROUP, None, tq, HEAD_DIM), lambda bb, k, i: (k, bb, i, 0))]
        + _kv_specs(t, SLAB_KV + 4, SLAB_KV + 6)
        + [pl.BlockSpec((1, 1, tq, HEAD_DIM), lambda bb, k, i: (bb, k, i, 0)),
           pl.BlockSpec(e.shape, lambda bb, k, i: (0, 0, 0))],
        out_specs=pl.BlockSpec((1, tq, width), lambda bb, k, i: (bb, i, k)),
        out_shape=jax.ShapeDtypeStruct((b, t, NSA_WIDTH), F32),
        scratch_shapes=_flash_scratch(GQA_GROUP * tq),
        compiler_params=_cparams(3),
        name="nsa_slc",
    )(p4, p4, p4, sel, e)


def _win_body(q_ref, k_ref, v_ref, o_ref, *, tq, band):
    i = pl.program_id(2)
    g = q_ref.shape[0]
    rows = g * tq
    q = q_ref[...].reshape(rows, HEAD_DIM)
    start = pl.multiple_of(jnp.maximum(i * tq - WINDOW, 0), tq)
    k = k_ref[0, 0, pl.ds(start, band), :]
    v = v_ref[0, 0, pl.ds(start, band), :]
    row = lax.broadcasted_iota(jnp.int32, (tq, band), 0)
    col = lax.broadcasted_iota(jnp.int32, (tq, band), 1)
    kpos = start + col
    tpos = i * tq + row
    live = (kpos <= tpos) & (kpos > tpos - WINDOW)
    bias = jnp.where(live, 0.0, NEG_INF)
    s = (_dot_nt(q, k).reshape(g, tq, band) + bias[None]).reshape(rows, band)
    m = jnp.max(s, axis=-1, keepdims=True)
    p = jnp.exp(s - m)
    l = jnp.sum(p, axis=-1, keepdims=True)
    o = _dot(p.astype(BF16), v) / jnp.maximum(l, 1e-30)
    for gg in range(g):
        o_ref[0, :, gg * HEAD_DIM:(gg + 1) * HEAD_DIM] = o[gg * tq:(gg + 1) * tq]


def _nsa_win(p4, tq=256):
    _, b, t, _ = p4.shape
    band = min(WINDOW + tq, t)
    width = GQA_GROUP * HEAD_DIM
    return pl.pallas_call(
        functools.partial(_win_body, tq=tq, band=band),
        grid=(b, N_KV_NSA, t // tq),
        in_specs=[pl.BlockSpec((GQA_GROUP, None, tq, HEAD_DIM), lambda bb, k, i: (k, bb, i, 0))]
        + _kv_specs(t, SLAB_KV + 8, SLAB_KV + 10),
        out_specs=pl.BlockSpec((1, tq, width), lambda bb, k, i: (bb, i, k)),
        out_shape=jax.ShapeDtypeStruct((b, t, NSA_WIDTH), F32),
        compiler_params=_cparams(3),
        name="nsa_win",
    )(p4, p4, p4)


def _fox_body(q_ref, k_ref, v_ref, ct_ref, o_ref, m_ref, l_ref, acc_ref, *, tq):
    i = pl.program_id(2)
    q = q_ref[0]
    _init_online(m_ref, l_ref, acc_ref)
    row = lax.broadcasted_iota(jnp.int32, (tq, tq), 0)
    col = lax.broadcasted_iota(jnp.int32, (tq, tq), 1)

    def step(kc, diagonal):
        start = pl.multiple_of(kc * tq, tq)
        k = k_ref[0, 0, pl.ds(start, tq), :]
        v = v_ref[0, 0, pl.ds(start, tq), :]
        s = _dot_nt(q, k) - ct_ref[0, pl.ds(kc, 1), :]
        if diagonal:
            s = jnp.where(col <= row, s, NEG_INF)
        _online_update(s, v, m_ref, l_ref, acc_ref)

    def body(kc, carry):
        step(kc, False)
        return carry

    lax.fori_loop(0, i, body, 0)
    step(i, True)
    o_ref[0] = _normalize(acc_ref, l_ref)


def _fox_attn(p4, ctr, tq=512):
    _, b, t, _ = p4.shape
    nh = N_HEADS_FOX
    return pl.pallas_call(
        functools.partial(_fox_body, tq=tq),
        grid=(b, nh, t // tq),
        in_specs=[
            pl.BlockSpec((1, None, tq, HEAD_DIM), lambda bb, h, i: (SLAB_FOX + h, bb, i, 0)),
            pl.BlockSpec((1, 1, t, HEAD_DIM), lambda bb, h, i: (SLAB_FOX + nh + h, bb, 0, 0)),
            pl.BlockSpec((1, 1, t, HEAD_DIM), lambda bb, h, i: (SLAB_FOX + 2 * nh + h, bb, 0, 0)),
            pl.BlockSpec((1, t // tq, tq), lambda bb, h, i: (bb * nh + h, 0, 0)),
        ],
        out_specs=pl.BlockSpec((1, tq, HEAD_DIM), lambda bb, h, i: (bb, i, h)),
        out_shape=jax.ShapeDtypeStruct((b, t, FOX_WIDTH), F32),
        scratch_shapes=_flash_scratch(tq),
        compiler_params=_cparams(3),
        name="fox_attn",
    )(p4, p4, p4, ctr)


def _out_body(oc_ref, os_ref, ow_ref, of_ref, gt_ref, x_ref, nn_ref, fn_ref, w_ref, o_ref, cat_ref):
    gt = gt_ref[...]
    tm = gt.shape[0]
    ss = jnp.zeros((tm, 1), F32)
    for hq in range(N_HEADS_NSA):
        sl = slice(hq * HEAD_DIM, (hq + 1) * HEAD_DIM)
        y = (gt[:, 3 * hq:3 * hq + 1] * oc_ref[:, sl] + gt[:, 3 * hq + 1:3 * hq + 2] * os_ref[:, sl]
             + gt[:, 3 * hq + 2:3 * hq + 3] * ow_ref[:, sl])
        ss = ss + jnp.sum(y * y, axis=-1, keepdims=True)
        cat_ref[:, sl] = y
    inv = lax.rsqrt(ss * (1.0 / NSA_WIDTH) + EPS)
    nsa = (cat_ref[:, :NSA_WIDTH] * inv * nn_ref[...]).astype(BF16)
    fox = (_row_rms(of_ref[...], FOX_WIDTH) * fn_ref[...]).astype(BF16)
    o_ref[...] = x_ref[...] + (_dot(nsa, w_ref[:NSA_WIDTH, :]) + _dot(fox, w_ref[NSA_WIDTH:, :]))


def _out(oc, osl, ow, of, sm, x, nn, fn, w, layer, tm=256):
    n, d = x.shape
    return pl.pallas_call(
        _out_body,
        grid=(n // tm,),
        in_specs=[
            pl.BlockSpec((tm, NSA_WIDTH), lambda i: (i, 0)),
            pl.BlockSpec((tm, NSA_WIDTH), lambda i: (i, 0)),
            pl.BlockSpec((tm, NSA_WIDTH), lambda i: (i, 0)),
            pl.BlockSpec((tm, FOX_WIDTH), lambda i: (i, 0)),
            pl.BlockSpec((tm, HEAD_DIM), lambda i: (i, 0)),
            pl.BlockSpec((tm, d), lambda i: (i, 0)),
            pl.BlockSpec((None, 1, NSA_WIDTH), lambda i: (layer, 0, 0)),
            pl.BlockSpec((None, 1, FOX_WIDTH), lambda i: (layer, 0, 0)),
            pl.BlockSpec((None, NSA_WIDTH + FOX_WIDTH, d), lambda i: (layer, 0, 0)),
        ],
        out_specs=pl.BlockSpec((tm, d), lambda i: (i, 0)),
        out_shape=jax.ShapeDtypeStruct((n, d), F32),
        scratch_shapes=[pltpu.VMEM((tm, NSA_WIDTH), F32)],
        compiler_params=_cparams(1),
        name="out_proj",
    )(oc, osl, ow, of, sm, x, nn, fn, w)


def _rope_tables(pos):
    inv = ROPE_THETA ** (-jnp.arange(0, ROPE_DIM, 2, dtype=F32) / ROPE_DIM)
    ang = pos.astype(F32)[:, None] * inv[None, :]
    cos, sin = jnp.cos(ang), jnp.sin(ang)
    n = pos.shape[0]
    pad = HEAD_DIM - ROPE_DIM
    cos_t = jnp.concatenate([cos, cos, jnp.ones((n, pad), F32)], axis=-1)
    s1 = jnp.concatenate([-sin, jnp.zeros((n, HEAD_DIM - ROPE_HALF), F32)], axis=-1)
    s2 = jnp.concatenate([jnp.zeros((n, ROPE_HALF), F32), sin, jnp.zeros((n, pad), F32)], axis=-1)
    return cos_t, s1, s2


def _overlap_matrix(t):
    n_cmp = (t - CMP_BLOCK) // CMP_STRIDE + 1
    n_slc = t // SLC_BLOCK
    cmp_start = np.arange(HEAD_DIM) * CMP_STRIDE
    slc_start = np.arange(HEAD_DIM) * SLC_BLOCK
    ov = ((cmp_start[:, None] < slc_start[None, :] + SLC_BLOCK)
          & (cmp_start[:, None] + CMP_BLOCK > slc_start[None, :]))
    ov = ov & (np.arange(HEAD_DIM)[:, None] < n_cmp) & (np.arange(HEAD_DIM)[None, :] < n_slc)
    return jnp.asarray(ov.astype(np.float32), BF16)


def _expand_matrix(t, tk):
    kpos = np.arange(t).reshape(t // tk, 1, tk)
    j = np.arange(HEAD_DIM).reshape(1, HEAD_DIM, 1)
    return jnp.asarray((kpos // SLC_BLOCK == j).astype(np.float32), BF16)


def _mixer(x2, b, t, layer, wts, tabs):
    n = x2.shape[0]
    p3, kc32, sm = _proj(x2, wts["mix_norm"], wts["w_main"], wts["w_small"], wts["gains"],
                         tabs["cos"], tabs["s1"], tabs["s2"], wts["fbias"], layer, t)
    p4 = p3.reshape(N_SLABS, b, t, HEAD_DIM)
    kr = kc32.reshape(HEADS_PER_TILE, b, t // CMP_STRIDE, CMP_STRIDE * HEAD_DIM)
    ct = _cumsum(sm.reshape(b, t, HEAD_DIM))
    fox_tq = min(512, t)
    ctr = ct.reshape(b * N_HEADS_FOX, t // fox_tq, fox_tq)
    kvc = _compress(kr, wts["pos"], wts["cmp_w1"], wts["cmp_w2"], wts["knorm0"],
                    tabs["cos_c"], tabs["s1_c"], tabs["s2_c"], layer)
    o_cmp, sel = _cmp_select(p4, kvc, tabs["ov"])
    o_slc = _nsa_slc(p4, sel, tabs["e"])
    o_win = _nsa_win(p4)
    o_fox = _fox_attn(p4, ctr, fox_tq)
    return _out(o_cmp.reshape(n, NSA_WIDTH), o_slc.reshape(n, NSA_WIDTH), o_win.reshape(n, NSA_WIDTH),
                o_fox.reshape(n, FOX_WIDTH), sm, x2, wts["nsa_out_norm"], wts["fox_out_norm"],
                wts["w_out"], layer)


def _prep_weights(ffn1_norm, ffn1_w_gate, ffn1_w_up, ffn1_w_down, mix_norm, w_in, nsa_q_norm, nsa_k_norm,
                  cmp_pos_emb, cmp_w1, cmp_w2, nsa_out_norm, fox_q_norm, fox_k_norm, fox_forget_bias,
                  fox_out_norm, w_out, ffn2_norm, ffn2_w_gate, ffn2_w_up, ffn2_w_down):
    depth, d, _ = w_in.shape
    c0 = COLS_NSA_Q + COLS_NSA_KV
    c1 = c0 + COLS_NSA_GATE
    c2 = c1 + COLS_FOX_QKV
    w_small = jnp.concatenate(
        [w_in[:, :, c0:c1], w_in[:, :, c2:], jnp.zeros((depth, d, HEAD_DIM - COLS_NSA_GATE - COLS_FOX_F), F32)],
        axis=-1).astype(BF16)
    w_main = jnp.concatenate([w_in[:, :, :c0], w_in[:, :, c1:c2]], axis=-1).astype(BF16)

    def tile(v):
        return jnp.concatenate([v] * HEADS_PER_TILE, axis=-1)

    one = jnp.ones((depth, HEAD_DIM), F32)
    gains = [tile(one)] * N_PROJ_TILES
    gains[0] = gains[1] = tile(nsa_q_norm * SCALE)
    gains[3] = jnp.concatenate([nsa_k_norm[:, 1]] * 2 + [one] * 2, axis=-1)
    gains[4] = jnp.concatenate([nsa_k_norm[:, 2]] * 2 + [one] * 2, axis=-1)
    gains[5] = gains[6] = tile(fox_q_norm * SCALE)
    gains[7] = gains[8] = tile(fox_k_norm)
    gains = jnp.stack(gains, axis=1)[:, :, None, :]

    fbias = jnp.zeros((depth, 1, HEAD_DIM), F32).at[:, 0, LANE_LOGF:LANE_LOGF + N_HEADS_FOX].set(
        fox_forget_bias)
    half = CMP_BLOCK * HEAD_DIM // 2
    pos = cmp_pos_emb.reshape(depth, 2, 2, 1, half)
    return dict(
        ffn1=(ffn1_norm[:, None, :], ffn1_w_gate.astype(BF16), ffn1_w_up.astype(BF16), ffn1_w_down.astype(BF16)),
        ffn2=(ffn2_norm[:, None, :], ffn2_w_gate.astype(BF16), ffn2_w_up.astype(BF16), ffn2_w_down.astype(BF16)),
        mix_norm=mix_norm[:, None, :], w_main=w_main, w_small=w_small, gains=gains, fbias=fbias, pos=pos,
        cmp_w1=cmp_w1.astype(BF16), cmp_w2=cmp_w2.astype(BF16), knorm0=nsa_k_norm[:, 0][:, None, :],
        nsa_out_norm=nsa_out_norm[:, None, :], fox_out_norm=fox_out_norm[:, None, :],
        w_out=w_out.astype(BF16),
    )


def _tables(t):
    cos, s1, s2 = _rope_tables(jnp.arange(t))
    cmp_end = jnp.arange(t // CMP_STRIDE) * CMP_STRIDE + (CMP_BLOCK - 1)
    cos_c, s1_c, s2_c = _rope_tables(cmp_end)
    return dict(cos=cos, s1=s1, s2=s2, cos_c=cos_c, s1_c=s1_c, s2_c=s2_c,
                ov=_overlap_matrix(t), e=_expand_matrix(t, min(512, t)))


def kernel(x, ffn1_norm, ffn1_w_gate, ffn1_w_up, ffn1_w_down, mix_norm, w_in, nsa_q_norm, nsa_k_norm, cmp_pos_emb, cmp_w1, cmp_w2, nsa_out_norm, fox_q_norm, fox_k_norm, fox_forget_bias, fox_out_norm, w_out, ffn2_norm, ffn2_w_gate, ffn2_w_up, ffn2_w_down):
    b, t, d = x.shape
    depth = w_in.shape[0]
    wts = _prep_weights(ffn1_norm, ffn1_w_gate, ffn1_w_up, ffn1_w_down, mix_norm, w_in, nsa_q_norm,
                        nsa_k_norm, cmp_pos_emb, cmp_w1, cmp_w2, nsa_out_norm, fox_q_norm, fox_k_norm,
                        fox_forget_bias, fox_out_norm, w_out, ffn2_norm, ffn2_w_gate, ffn2_w_up, ffn2_w_down)
    tabs = _tables(t)
    x2 = x.reshape(b * t, d)
    for layer in range(depth):
        x2 = _ffn(x2, *wts["ffn1"], layer)
        x2 = _mixer(x2, b, t, layer, wts, tabs)
        x2 = _ffn(x2, *wts["ffn2"], layer)
    return x2.reshape(b, t, d)
```

```python
import functools

import numpy as np
import jax
import jax.numpy as jnp
from jax import lax
from jax.experimental import pallas as pl
from jax.experimental.pallas import tpu as pltpu

F32 = jnp.float32
BF16 = jnp.bfloat16

HEAD_DIM = 128
N_HEADS_NSA = 8
N_KV_NSA = 2
GQA_GROUP = 4
N_HEADS_FOX = 8
NSA_WIDTH = N_HEADS_NSA * HEAD_DIM
FOX_WIDTH = N_HEADS_FOX * HEAD_DIM
ROPE_DIM = 32
ROPE_HALF = ROPE_DIM // 2
ROPE_THETA = 500000.0
CMP_BLOCK = 32
CMP_STRIDE = 16
CMP_HIDDEN = 256
SLC_BLOCK = 64
SLC_TOPK = 8
N_LOCAL_SLC = 2
WINDOW = 512
EPS = 1e-6
NEG_INF = -1e30
FORCED_SCORE = 1e9
SCALE = HEAD_DIM ** -0.5

COLS_NSA_Q = NSA_WIDTH
COLS_NSA_KV = 3 * 2 * N_KV_NSA * HEAD_DIM
COLS_NSA_GATE = 3 * N_HEADS_NSA
COLS_FOX_QKV = 3 * FOX_WIDTH
COLS_FOX_F = N_HEADS_FOX
COLS_MAIN = COLS_NSA_Q + COLS_NSA_KV + COLS_FOX_QKV

HEADS_PER_TILE = 4
PROJ_TN = HEADS_PER_TILE * HEAD_DIM
N_PROJ_TILES = COLS_MAIN // PROJ_TN
N_SLABS = COLS_MAIN // HEAD_DIM
SLAB_Q = 0
SLAB_KV = 8
SLAB_FOX = 20
TILE_CMP_KV = 2
LANE_LOGF = COLS_NSA_GATE

VMEM_LIMIT_BYTES = 56 * 1024 * 1024


def _cparams(n_axes):
    return pltpu.CompilerParams(dimension_semantics=("arbitrary",) * n_axes,
                                vmem_limit_bytes=VMEM_LIMIT_BYTES)


def _dot(a, b):
    return jnp.dot(a, b, preferred_element_type=F32)


def _dot_nt(a, b):
    return lax.dot_general(a, b, (((1,), (1,)), ((), ())), preferred_element_type=F32)


def _row_rms(x, width):
    return x * lax.rsqrt(jnp.sum(x * x, axis=-1, keepdims=True) * (1.0 / width) + EPS)


def _ffn_body(x_ref, g_ref, wg_ref, wu_ref, wd_ref, o_ref, h_ref):
    j = pl.program_id(1)

    @pl.when(j == 0)
    def _():
        x = x_ref[...]
        h_ref[...] = (_row_rms(x, x.shape[-1]) * g_ref[...]).astype(BF16)
        o_ref[...] = x

    h = h_ref[...]
    a = _dot(h, wg_ref[...])
    b = _dot(h, wu_ref[...])
    u = (0.5 * a * jax.nn.sigmoid(a)) * b
    o_ref[...] += _dot(u.astype(BF16), wd_ref[...])


def _ffn(x, g, wg, wu, wd, layer, tm=1024, tf=512):
    n, d = x.shape
    f = wg.shape[-1]
    return pl.pallas_call(
        _ffn_body,
        grid=(n // tm, f // tf),
        in_specs=[
            pl.BlockSpec((tm, d), lambda i, j: (i, 0)),
            pl.BlockSpec((None, 1, d), lambda i, j: (layer, 0, 0)),
            pl.BlockSpec((None, d, tf), lambda i, j: (layer, 0, j)),
            pl.BlockSpec((None, d, tf), lambda i, j: (layer, 0, j)),
            pl.BlockSpec((None, tf, d), lambda i, j: (layer, j, 0)),
        ],
        out_specs=pl.BlockSpec((tm, d), lambda i, j: (i, 0)),
        out_shape=jax.ShapeDtypeStruct((n, d), F32),
        scratch_shapes=[pltpu.VMEM((tm, d), BF16)],
        compiler_params=_cparams(2),
        name="ffn",
    )(x, g, wg, wu, wd)


def _rope(y, cos, s1, s2):
    return y * cos + pltpu.roll(y, HEAD_DIM - ROPE_HALF, 1) * s1 + pltpu.roll(y, ROPE_HALF, 1) * s2


def _proj_body(x_ref, g_ref, w_ref, ws_ref, gain_ref, cos_ref, s1_ref, s2_ref, bias_ref,
               p_ref, kc_ref, sm_ref, h_ref):
    j = pl.program_id(1)

    @pl.when(j == 0)
    def _():
        x = x_ref[...]
        h = (_row_rms(x, x.shape[-1]) * g_ref[...]).astype(BF16)
        h_ref[...] = h
        y = _dot(h, ws_ref[...])
        lane = lax.broadcasted_iota(jnp.int32, y.shape, 1)
        z = y + bias_ref[...]
        logsig = jnp.minimum(z, 0.0) - jnp.log1p(jnp.exp(-jnp.abs(z)))
        small = jnp.where(lane < LANE_LOGF, jax.nn.sigmoid(y), logsig)
        sm_ref[0] = small
        sm_ref[1] = pltpu.roll(small, HEAD_DIM - 3 * GQA_GROUP, 1)

    acc = _dot(h_ref[...], w_ref[...])
    gain = gain_ref[...]

    def raw(hh):
        return acc[:, hh * HEAD_DIM:(hh + 1) * HEAD_DIM]

    def normed(hh):
        return _row_rms(raw(hh), HEAD_DIM) * gain[:, hh * HEAD_DIM:(hh + 1) * HEAD_DIM]

    def roped(hh):
        return _rope(normed(hh), cos_ref[...], s1_ref[...], s2_ref[...])

    def emit(fns):
        for hh, fn in enumerate(fns):
            p_ref[hh] = fn(hh).astype(p_ref.dtype)

    @pl.when(j < 2)
    def _():
        emit([roped] * 4)

    @pl.when(j == TILE_CMP_KV)
    def _():
        emit([raw] * 4)
        for hh in range(HEADS_PER_TILE):
            kc_ref[hh] = raw(hh)

    @pl.when((j == 3) | (j == 4))
    def _():
        emit([roped, roped, raw, raw])

    @pl.when((j >= 5) & (j < 9))
    def _():
        emit([normed] * 4)

    @pl.when(j >= 9)
    def _():
        emit([raw] * 4)


def _proj(x, g, w_main, w_small, gains, cos, s1, s2, bias, layer, seq, tm=1024):
    n, d = x.shape
    tm = min(tm, seq)
    tpb = seq // tm
    hp = HEADS_PER_TILE
    return pl.pallas_call(
        _proj_body,
        grid=(n // tm, N_PROJ_TILES),
        in_specs=[
            pl.BlockSpec((tm, d), lambda i, j: (i, 0)),
            pl.BlockSpec((None, 1, d), lambda i, j: (layer, 0, 0)),
            pl.BlockSpec((None, d, PROJ_TN), lambda i, j: (layer, 0, j)),
            pl.BlockSpec((None, d, HEAD_DIM), lambda i, j: (layer, 0, 0)),
            pl.BlockSpec((None, None, 1, PROJ_TN), lambda i, j: (layer, j, 0, 0)),
            pl.BlockSpec((tm, HEAD_DIM), lambda i, j: (i % tpb, 0)),
            pl.BlockSpec((tm, HEAD_DIM), lambda i, j: (i % tpb, 0)),
            pl.BlockSpec((tm, HEAD_DIM), lambda i, j: (i % tpb, 0)),
            pl.BlockSpec((None, 1, HEAD_DIM), lambda i, j: (layer, 0, 0)),
        ],
        out_specs=[
            pl.BlockSpec((hp, tm, HEAD_DIM), lambda i, j: (j, i, 0)),
            pl.BlockSpec((hp, tm, HEAD_DIM), lambda i, j: (0, i, 0)),
            pl.BlockSpec((N_KV_NSA, tm, HEAD_DIM), lambda i, j: (0, i, 0)),
        ],
        out_shape=[
            jax.ShapeDtypeStruct((N_SLABS, n, HEAD_DIM), BF16),
            jax.ShapeDtypeStruct((hp, n, HEAD_DIM), F32),
            jax.ShapeDtypeStruct((N_KV_NSA, n, HEAD_DIM), F32),
        ],
        scratch_shapes=[pltpu.VMEM((tm, d), BF16)],
        compiler_params=_cparams(2),
        name="proj",
    )(x, g, w_main, w_small, gains, cos, s1, s2, bias)


def _cumsum_body(x_ref, ct_ref):
    x = x_ref[0, 0]
    t = x.shape[0]
    row = lax.broadcasted_iota(jnp.int32, x.shape, 0)
    s = 1
    while s < t:
        x = x + jnp.where(row >= s, pltpu.roll(x, s, 0), 0.0)
        s *= 2
    ct_ref[0] = x.T[LANE_LOGF:LANE_LOGF + N_HEADS_FOX, :]


def _cumsum(sm4):
    _, b, t, _ = sm4.shape
    return pl.pallas_call(
        _cumsum_body,
        grid=(b,),
        in_specs=[pl.BlockSpec((1, 1, t, HEAD_DIM), lambda i: (0, i, 0, 0))],
        out_specs=pl.BlockSpec((1, N_HEADS_FOX, t), lambda i: (i, 0, 0)),
        out_shape=jax.ShapeDtypeStruct((b, N_HEADS_FOX, t), F32),
        compiler_params=_cparams(1),
        name="fox_cumsum",
    )(sm4)


def _gelu_tanh(x):
    c = float(np.sqrt(2.0 / np.pi))
    return x * (0.5 * (1.0 + jnp.tanh(c * (x + 0.044715 * (x * x * x)))))


def _compress_body(r_ref, pos_ref, w1_ref, w2_ref, kn_ref, cos_ref, s1_ref, s2_ref, o_ref):
    is_key = pl.program_id(0) == 0
    r = r_ref[0, 0]
    half = r.shape[-1]
    a = _dot((r + pos_ref[0]).astype(BF16), w1_ref[:half, :])
    bm = _dot((r + pos_ref[1]).astype(BF16), w1_ref[half:, :])
    h = a + pltpu.roll(bm, bm.shape[0] - 1, 0)
    y = _dot(_gelu_tanh(h).astype(BF16), w2_ref[...])

    @pl.when(is_key)
    def _():
        yk = _row_rms(y, HEAD_DIM) * kn_ref[...]
        o_ref[0, 0, 0] = _rope(yk, cos_ref[...], s1_ref[...], s2_ref[...]).astype(o_ref.dtype)

    @pl.when(jnp.logical_not(is_key))
    def _():
        o_ref[0, 0, 0] = y.astype(o_ref.dtype)


def _compress(kr, pos, w1, w2, knorm0, cos_c, s1_c, s2_c, layer):
    _, b, nr, width = kr.shape
    return pl.pallas_call(
        _compress_body,
        grid=(2, b, N_KV_NSA),
        in_specs=[
            pl.BlockSpec((1, 1, nr, width), lambda ty, i, k: (ty * 2 + k, i, 0, 0)),
            pl.BlockSpec((None, None, 2, 1, width), lambda ty, i, k: (layer, ty, 0, 0, 0)),
            pl.BlockSpec((None, None, 2 * width, CMP_HIDDEN), lambda ty, i, k: (layer, ty, 0, 0)),
            pl.BlockSpec((None, None, CMP_HIDDEN, HEAD_DIM), lambda ty, i, k: (layer, ty, 0, 0)),
            pl.BlockSpec((None, 1, HEAD_DIM), lambda ty, i, k: (layer, 0, 0)),
            pl.BlockSpec((nr, HEAD_DIM), lambda ty, i, k: (0, 0)),
            pl.BlockSpec((nr, HEAD_DIM), lambda ty, i, k: (0, 0)),
            pl.BlockSpec((nr, HEAD_DIM), lambda ty, i, k: (0, 0)),
        ],
        out_specs=pl.BlockSpec((1, 1, 1, nr, HEAD_DIM), lambda ty, i, k: (ty, i, k, 0, 0)),
        out_shape=jax.ShapeDtypeStruct((2, b, N_KV_NSA, nr, HEAD_DIM), BF16),
        compiler_params=_cparams(3),
        name="compress",
    )(kr, pos, w1, w2, knorm0, cos_c, s1_c, s2_c)


def _split3(x):
    hi = x.astype(BF16)
    r1 = x - hi.astype(F32)
    mid = r1.astype(BF16)
    lo = (r1 - mid.astype(F32)).astype(BF16)
    return hi, mid, lo


def _cmp_branch(q, kc, vc, ovt, i, tq, n_blk):
    rows = q.shape[0]
    g = rows // tq
    s = _dot_nt(q, kc)
    row_g = lax.broadcasted_iota(jnp.int32, (rows, HEAD_DIM), 0)
    lane_g = lax.broadcasted_iota(jnp.int32, (rows, HEAD_DIM), 1)
    t_g = i * tq + (row_g & (tq - 1))
    valid = (lane_g * CMP_STRIDE + (CMP_BLOCK - 1)) <= t_g
    s = jnp.where(valid, s, NEG_INF)
    m = jnp.max(s, axis=-1, keepdims=True)
    p = jnp.where(valid, jnp.exp(s - m), 0.0)
    p = p / jnp.maximum(jnp.sum(p, axis=-1, keepdims=True), 1e-30)
    o = _dot(p.astype(BF16), vc)
    psum = p[0:tq]
    for gg in range(1, g):
        psum = psum + p[gg * tq:(gg + 1) * tq]

    hi, mid, lo = _split3(psum)
    imp = ((_dot_nt(ovt, hi) + _dot_nt(ovt, mid)) + _dot_nt(ovt, lo))[:n_blk]
    blk = lax.broadcasted_iota(jnp.int32, (n_blk, tq), 0)
    t_blk = (i * tq + lax.broadcasted_iota(jnp.int32, (n_blk, tq), 1)) // SLC_BLOCK
    causal = blk <= t_blk
    forced = (blk == 0) | (causal & (blk > t_blk - N_LOCAL_SLC))
    score = jnp.where(forced, FORCED_SCORE, jnp.where(causal, imp, NEG_INF))
    cnt = jnp.zeros((n_blk, tq), jnp.int32)
    for c in range(n_blk):
        other = score[c:c + 1, :]
        ahead = (other > score) | ((other == score) & (blk > c))
        cnt = cnt + ahead.astype(jnp.int32)
    picked = jnp.where(cnt < SLC_TOPK, 1.0, 0.0)
    picked = jnp.concatenate([picked, jnp.zeros((HEAD_DIM - n_blk, tq), F32)], axis=0)
    return o, picked.T.astype(BF16)


def _online_update(s, v, m_ref, l_ref, acc_ref, rows=slice(None)):
    cols = [s[:, c:c + HEAD_DIM] for c in range(0, s.shape[1], HEAD_DIM)]
    m_el = functools.reduce(jnp.maximum, cols)
    m_prev = m_ref[rows]
    m_new = jnp.maximum(m_prev, jnp.max(m_el, axis=-1, keepdims=True))
    alpha = jnp.exp(m_prev - m_new)
    ps = [jnp.exp(c - m_new) for c in cols]
    l_ref[rows] = alpha * l_ref[rows] + functools.reduce(jnp.add, ps)
    p = jnp.concatenate([x.astype(BF16) for x in ps], axis=1)
    acc_ref[rows] = alpha * acc_ref[rows] + _dot(p, v)
    m_ref[rows] = m_new


def _normalize(acc_ref, l_ref):
    l = jnp.sum(l_ref[...], axis=-1, keepdims=True)
    return acc_ref[...] / jnp.maximum(l, 1e-30)


def _init_online(m_ref, l_ref, acc_ref):
    m_ref[...] = jnp.full(m_ref.shape, NEG_INF, F32)
    l_ref[...] = jnp.zeros(l_ref.shape, F32)
    acc_ref[...] = jnp.zeros(acc_ref.shape, F32)


def _flash_scratch(rows):
    return [pltpu.VMEM((rows, HEAD_DIM), F32)] * 3


def _win_branch(q, k_ref, v_ref, i, tq, band):
    rows = q.shape[0]
    g = rows // tq
    start = pl.multiple_of(jnp.maximum(i * tq - WINDOW, 0), tq)
    k = k_ref[0, 0, pl.ds(start, band), :]
    v = v_ref[0, 0, pl.ds(start, band), :]
    row = lax.broadcasted_iota(jnp.int32, (tq, band), 0)
    col = lax.broadcasted_iota(jnp.int32, (tq, band), 1)
    kpos = start + col
    tpos = i * tq + row
    live = (kpos <= tpos) & (kpos > tpos - WINDOW)
    bias = jnp.where(live, 0.0, NEG_INF)
    s = (_dot_nt(q, k).reshape(g, tq, band) + bias[None]).reshape(rows, band)
    m = jnp.max(s, axis=-1, keepdims=True)
    p = jnp.exp(s - m)
    l = jnp.sum(p, axis=-1, keepdims=True)
    return _dot(p.astype(BF16), v) / jnp.maximum(l, 1e-30)


def _nsa_body(q_ref, kc_ref, vc_ref, ks_ref, vs_ref, kw_ref, vw_ref, gt_ref, ov_ref, e_ref, o_ref,
              m_ref, l_ref, acc_ref, part_ref, sel_ref, *, tq, tk, band, n_blk):
    i = pl.program_id(2)
    g = q_ref.shape[0]
    rows = g * tq
    q = q_ref[...].reshape(rows, HEAD_DIM)
    gt = gt_ref[0, 0]

    def gate(branch):
        return jnp.concatenate(
            [jnp.broadcast_to(gt[:, 3 * gg + branch:3 * gg + branch + 1], (tq, HEAD_DIM)) for gg in range(g)],
            axis=0)

    o_cmp, sel = _cmp_branch(q, kc_ref[0, 0, 0], vc_ref[0, 0, 0], ov_ref[...], i, tq, n_blk)
    sel_ref[...] = sel
    part_ref[...] = gate(0) * o_cmp
    part_ref[...] += gate(2) * _win_branch(q, kw_ref, vw_ref, i, tq, band)

    _init_online(m_ref, l_ref, acc_ref)
    row = lax.broadcasted_iota(jnp.int32, (tq, tk), 0)
    col = lax.broadcasted_iota(jnp.int32, (tq, tk), 1)
    last = (i * tq) // tk

    def step(kc, diagonal):
        start = pl.multiple_of(kc * tk, tk)
        k = ks_ref[0, 0, pl.ds(start, tk), :]
        v = vs_ref[0, 0, pl.ds(start, tk), :]
        live = _dot(sel_ref[...], e_ref[kc]) > 0.5
        if diagonal:
            live = live & ((start + col) <= (i * tq + row))
        bias = jnp.where(live, 0.0, NEG_INF)
        s = _dot_nt(q_ref[...].reshape(rows, HEAD_DIM), k).reshape(g, tq, tk) + bias[None]
        _online_update(s.reshape(rows, tk), v, m_ref, l_ref, acc_ref)

    def body(kc, carry):
        step(kc, False)
        return carry

    lax.fori_loop(0, last, body, 0)
    step(last, True)
    y = part_ref[...] + gate(1) * _normalize(acc_ref, l_ref)
    for gg in range(g):
        o_ref[0, :, gg * HEAD_DIM:(gg + 1) * HEAD_DIM] = y[gg * tq:(gg + 1) * tq]


def _nsa(p4, kvc, sm4, ovt, e, tq=256):
    _, b, t, _ = p4.shape
    nr = kvc.shape[3]
    tk = e.shape[-1]
    band = min(WINDOW + tq, t)
    width = GQA_GROUP * HEAD_DIM
    rows = GQA_GROUP * tq

    def kv(slab):
        return pl.BlockSpec((1, 1, t, HEAD_DIM), lambda bb, k, i: (slab + k, bb, 0, 0))

    return pl.pallas_call(
        functools.partial(_nsa_body, tq=tq, tk=tk, band=band, n_blk=t // SLC_BLOCK),
        grid=(b, N_KV_NSA, t // tq),
        in_specs=[
            pl.BlockSpec((GQA_GROUP, None, tq, HEAD_DIM), lambda bb, k, i: (k, bb, i, 0)),
            pl.BlockSpec((1, 1, 1, nr, HEAD_DIM), lambda bb, k, i: (0, bb, k, 0, 0)),
            pl.BlockSpec((1, 1, 1, nr, HEAD_DIM), lambda bb, k, i: (1, bb, k, 0, 0)),
            kv(SLAB_KV + 4), kv(SLAB_KV + 6), kv(SLAB_KV + 8), kv(SLAB_KV + 10),
            pl.BlockSpec((1, 1, tq, HEAD_DIM), lambda bb, k, i: (k, bb, i, 0)),
            pl.BlockSpec((HEAD_DIM, HEAD_DIM), lambda bb, k, i: (0, 0)),
            pl.BlockSpec(e.shape, lambda bb, k, i: (0, 0, 0)),
        ],
        out_specs=pl.BlockSpec((1, tq, width), lambda bb, k, i: (bb, i, k)),
        out_shape=jax.ShapeDtypeStruct((b, t, NSA_WIDTH), F32),
        scratch_shapes=_flash_scratch(rows) + [pltpu.VMEM((rows, HEAD_DIM), F32),
                                               pltpu.VMEM((tq, HEAD_DIM), BF16)],
        compiler_params=_cparams(3),
        name="nsa",
    )(p4, kvc, kvc, p4, p4, p4, p4, sm4, ovt, e)


def _fox_body(q_ref, k_ref, v_ref, ct_ref, o_ref, m_ref, l_ref, acc_ref, *, tq):
    i = pl.program_id(2)
    nh = q_ref.shape[0]
    _init_online(m_ref, l_ref, acc_ref)
    row = lax.broadcasted_iota(jnp.int32, (tq, tq), 0)
    col = lax.broadcasted_iota(jnp.int32, (tq, tq), 1)

    def step(kc, diagonal):
        start = pl.multiple_of(kc * tq, tq)
        for hh in range(nh):
            k = k_ref[hh, 0, pl.ds(start, tq), :]
            v = v_ref[hh, 0, pl.ds(start, tq), :]
            s = _dot_nt(q_ref[hh], k) - ct_ref[hh, pl.ds(kc, 1), :]
            if diagonal:
                s = jnp.where(col <= row, s, NEG_INF)
            _online_update(s, v, m_ref, l_ref, acc_ref, slice(hh * tq, (hh + 1) * tq))

    def body(kc, carry):
        step(kc, False)
        return carry

    lax.fori_loop(0, i, body, 0)
    step(i, True)
    o = _normalize(acc_ref, l_ref)
    for hh in range(nh):
        o_ref[0, :, hh * HEAD_DIM:(hh + 1) * HEAD_DIM] = o[hh * tq:(hh + 1) * tq]


FOX_HEADS_PER_STEP = 2


def _fox_attn(p4, ctr, tq=512):
    _, b, t, _ = p4.shape
    nh = N_HEADS_FOX
    hs = FOX_HEADS_PER_STEP
    q0, k0, v0 = (SLAB_FOX // hs, (SLAB_FOX + nh) // hs, (SLAB_FOX + 2 * nh) // hs)
    return pl.pallas_call(
        functools.partial(_fox_body, tq=tq),
        grid=(b, nh // hs, t // tq),
        in_specs=[
            pl.BlockSpec((hs, None, tq, HEAD_DIM), lambda bb, h, i: (q0 + h, bb, i, 0)),
            pl.BlockSpec((hs, 1, t, HEAD_DIM), lambda bb, h, i: (k0 + h, bb, 0, 0)),
            pl.BlockSpec((hs, 1, t, HEAD_DIM), lambda bb, h, i: (v0 + h, bb, 0, 0)),
            pl.BlockSpec((hs, t // tq, tq), lambda bb, h, i: (bb * (nh // hs) + h, 0, 0)),
        ],
        out_specs=pl.BlockSpec((1, tq, hs * HEAD_DIM), lambda bb, h, i: (bb, i, h)),
        out_shape=jax.ShapeDtypeStruct((b, t, FOX_WIDTH), F32),
        scratch_shapes=_flash_scratch(hs * tq),
        compiler_params=_cparams(3),
        name="fox_attn",
    )(p4, p4, p4, ctr)


def _out_body(on_ref, of_ref, x_ref, nn_ref, fn_ref, w_ref, o_ref):
    nsa = (_row_rms(on_ref[...], NSA_WIDTH) * nn_ref[...]).astype(BF16)
    fox = (_row_rms(of_ref[...], FOX_WIDTH) * fn_ref[...]).astype(BF16)
    o_ref[...] = x_ref[...] + (_dot(nsa, w_ref[:NSA_WIDTH, :]) + _dot(fox, w_ref[NSA_WIDTH:, :]))


def _out(on, of, x, nn, fn, w, layer, tm=512):
    n, d = x.shape
    return pl.pallas_call(
        _out_body,
        grid=(n // tm,),
        in_specs=[
            pl.BlockSpec((tm, NSA_WIDTH), lambda i: (i, 0)),
            pl.BlockSpec((tm, FOX_WIDTH), lambda i: (i, 0)),
            pl.BlockSpec((tm, d), lambda i: (i, 0)),
            pl.BlockSpec((None, 1, NSA_WIDTH), lambda i: (layer, 0, 0)),
            pl.BlockSpec((None, 1, FOX_WIDTH), lambda i: (layer, 0, 0)),
            pl.BlockSpec((None, NSA_WIDTH + FOX_WIDTH, d), lambda i: (layer, 0, 0)),
        ],
        out_specs=pl.BlockSpec((tm, d), lambda i: (i, 0)),
        out_shape=jax.ShapeDtypeStruct((n, d), F32),
        compiler_params=_cparams(1),
        name="out_proj",
    )(on, of, x, nn, fn, w)


def _rope_tables(pos):
    inv = ROPE_THETA ** (-jnp.arange(0, ROPE_DIM, 2, dtype=F32) / ROPE_DIM)
    ang = pos.astype(F32)[:, None] * inv[None, :]
    cos, sin = jnp.cos(ang), jnp.sin(ang)
    n = pos.shape[0]
    pad = HEAD_DIM - ROPE_DIM
    cos_t = jnp.concatenate([cos, cos, jnp.ones((n, pad), F32)], axis=-1)
    s1 = jnp.concatenate([-sin, jnp.zeros((n, HEAD_DIM - ROPE_HALF), F32)], axis=-1)
    s2 = jnp.concatenate([jnp.zeros((n, ROPE_HALF), F32), sin, jnp.zeros((n, pad), F32)], axis=-1)
    return cos_t, s1, s2


def _overlap_matrix(t):
    n_cmp = (t - CMP_BLOCK) // CMP_STRIDE + 1
    n_slc = t // SLC_BLOCK
    cmp_start = np.arange(HEAD_DIM) * CMP_STRIDE
    slc_start = np.arange(HEAD_DIM) * SLC_BLOCK
    ov = ((cmp_start[:, None] < slc_start[None, :] + SLC_BLOCK)
          & (cmp_start[:, None] + CMP_BLOCK > slc_start[None, :]))
    ov = ov & (np.arange(HEAD_DIM)[:, None] < n_cmp) & (np.arange(HEAD_DIM)[None, :] < n_slc)
    return jnp.asarray(ov.T.astype(np.float32), BF16)


def _expand_matrix(t, tk):
    kpos = np.arange(t).reshape(t // tk, 1, tk)
    j = np.arange(HEAD_DIM).reshape(1, HEAD_DIM, 1)
    return jnp.asarray((kpos // SLC_BLOCK == j).astype(np.float32), BF16)


def _mixer(x2, b, t, layer, wts, tabs):
    n = x2.shape[0]
    p3, kc32, sm = _proj(x2, wts["mix_norm"], wts["w_main"], wts["w_small"], wts["gains"],
                         tabs["cos"], tabs["s1"], tabs["s2"], wts["fbias"], layer, t)
    p4 = p3.reshape(N_SLABS, b, t, HEAD_DIM)
    kr = kc32.reshape(HEADS_PER_TILE, b, t // CMP_STRIDE, CMP_STRIDE * HEAD_DIM)
    sm4 = sm.reshape(N_KV_NSA, b, t, HEAD_DIM)
    ct = _cumsum(sm4)
    fox_tq = min(512, t)
    ctr = ct.reshape(b * N_HEADS_FOX, t // fox_tq, fox_tq)
    kvc = _compress(kr, wts["pos"], wts["cmp_w1"], wts["cmp_w2"], wts["knorm0"],
                    tabs["cos_c"], tabs["s1_c"], tabs["s2_c"], layer)
    o_nsa = _nsa(p4, kvc, sm4, tabs["ov"], tabs["e"])
    o_fox = _fox_attn(p4, ctr, fox_tq)
    return _out(o_nsa.reshape(n, NSA_WIDTH), o_fox.reshape(n, FOX_WIDTH), x2, wts["nsa_out_norm"],
                wts["fox_out_norm"], wts["w_out"], layer)


def _prep_weights(ffn1_norm, ffn1_w_gate, ffn1_w_up, ffn1_w_down, mix_norm, w_in, nsa_q_norm, nsa_k_norm,
                  cmp_pos_emb, cmp_w1, cmp_w2, nsa_out_norm, fox_q_norm, fox_k_norm, fox_forget_bias,
                  fox_out_norm, w_out, ffn2_norm, ffn2_w_gate, ffn2_w_up, ffn2_w_down):
    depth, d, _ = w_in.shape
    c0 = COLS_NSA_Q + COLS_NSA_KV
    c1 = c0 + COLS_NSA_GATE
    c2 = c1 + COLS_FOX_QKV
    w_small = jnp.concatenate(
        [w_in[:, :, c0:c1], w_in[:, :, c2:], jnp.zeros((depth, d, HEAD_DIM - COLS_NSA_GATE - COLS_FOX_F), F32)],
        axis=-1).astype(BF16)
    w_main = jnp.concatenate([w_in[:, :, :c0], w_in[:, :, c1:c2]], axis=-1).astype(BF16)

    def tile(v):
        return jnp.concatenate([v] * HEADS_PER_TILE, axis=-1)

    one = jnp.ones((depth, HEAD_DIM), F32)
    gains = [tile(one)] * N_PROJ_TILES
    gains[0] = gains[1] = tile(nsa_q_norm * SCALE)
    gains[3] = jnp.concatenate([nsa_k_norm[:, 1]] * 2 + [one] * 2, axis=-1)
    gains[4] = jnp.concatenate([nsa_k_norm[:, 2]] * 2 + [one] * 2, axis=-1)
    gains[5] = gains[6] = tile(fox_q_norm * SCALE)
    gains[7] = gains[8] = tile(fox_k_norm)
    gains = jnp.stack(gains, axis=1)[:, :, None, :]

    fbias = jnp.zeros((depth, 1, HEAD_DIM), F32).at[:, 0, LANE_LOGF:LANE_LOGF + N_HEADS_FOX].set(
        fox_forget_bias)
    half = CMP_BLOCK * HEAD_DIM // 2
    pos = cmp_pos_emb.reshape(depth, 2, 2, 1, half)
    return dict(
        ffn1=(ffn1_norm[:, None, :], ffn1_w_gate.astype(BF16), ffn1_w_up.astype(BF16), ffn1_w_down.astype(BF16)),
        ffn2=(ffn2_norm[:, None, :], ffn2_w_gate.astype(BF16), ffn2_w_up.astype(BF16), ffn2_w_down.astype(BF16)),
        mix_norm=mix_norm[:, None, :], w_main=w_main, w_small=w_small, gains=gains, fbias=fbias, pos=pos,
        cmp_w1=cmp_w1.astype(BF16), cmp_w2=cmp_w2.astype(BF16), knorm0=nsa_k_norm[:, 0][:, None, :],
        nsa_out_norm=nsa_out_norm[:, None, :], fox_out_norm=fox_out_norm[:, None, :],
        w_out=w_out.astype(BF16),
    )


def _tables(t):
    cos, s1, s2 = _rope_tables(jnp.arange(t))
    cmp_end = jnp.arange(t // CMP_STRIDE) * CMP_STRIDE + (CMP_BLOCK - 1)
    cos_c, s1_c, s2_c = _rope_tables(cmp_end)
    return dict(cos=cos, s1=s1, s2=s2, cos_c=cos_c, s1_c=s1_c, s2_c=s2_c,
                ov=_overlap_matrix(t), e=_expand_matrix(t, min(512, t)))


def kernel(x, ffn1_norm, ffn1_w_gate, ffn1_w_up, ffn1_w_down, mix_norm, w_in, nsa_q_norm, nsa_k_norm, cmp_pos_emb, cmp_w1, cmp_w2, nsa_out_norm, fox_q_norm, fox_k_norm, fox_forget_bias, fox_out_norm, w_out, ffn2_norm, ffn2_w_gate, ffn2_w_up, ffn2_w_down):
    b, t, d = x.shape
    depth = w_in.shape[0]
    wts = _prep_weights(ffn1_norm, ffn1_w_gate, ffn1_w_up, ffn1_w_down, mix_norm, w_in, nsa_q_norm,
                        nsa_k_norm, cmp_pos_emb, cmp_w1, cmp_w2, nsa_out_norm, fox_q_norm, fox_k_norm,
                        fox_forget_bias, fox_out_norm, w_out, ffn2_norm, ffn2_w_gate, ffn2_w_up, ffn2_w_down)
    tabs = _tables(t)
    x2 = x.reshape(b * t, d)
    for layer in range(depth):
        x2 = _ffn(x2, *wts["ffn1"], layer)
        x2 = _mixer(x2, b, t, layer, wts, tabs)
        x2 = _ffn(x2, *wts["ffn2"], layer)
    return x2.reshape(b, t, d)
```

```python
import functools

import numpy as np
import jax
import jax.numpy as jnp
from jax import lax
from jax.experimental import pallas as pl
from jax.experimental.pallas import tpu as pltpu

F32 = jnp.float32
BF16 = jnp.bfloat16

HEAD_DIM = 128
N_HEADS_NSA = 8
N_KV_NSA = 2
GQA_GROUP = 4
N_HEADS_FOX = 8
NSA_WIDTH = N_HEADS_NSA * HEAD_DIM
FOX_WIDTH = N_HEADS_FOX * HEAD_DIM
ROPE_DIM = 32
ROPE_HALF = ROPE_DIM // 2
ROPE_THETA = 500000.0
CMP_BLOCK = 32
CMP_STRIDE = 16
CMP_HIDDEN = 256
SLC_BLOCK = 64
SLC_TOPK = 8
N_LOCAL_SLC = 2
WINDOW = 512
EPS = 1e-6
NEG_INF = -1e30
FORCED_SCORE = 1e9
SCALE = HEAD_DIM ** -0.5
LOG2E = float(np.log2(np.e))
Q_SCALE = SCALE * LOG2E

COLS_NSA_Q = NSA_WIDTH
COLS_NSA_KV = 3 * 2 * N_KV_NSA * HEAD_DIM
COLS_NSA_GATE = 3 * N_HEADS_NSA
COLS_FOX_QKV = 3 * FOX_WIDTH
COLS_FOX_F = N_HEADS_FOX
COLS_MAIN = COLS_NSA_Q + COLS_NSA_KV + COLS_FOX_QKV

HEADS_PER_TILE = 4
PROJ_TN = HEADS_PER_TILE * HEAD_DIM
N_PROJ_TILES = COLS_MAIN // PROJ_TN
N_SLABS = COLS_MAIN // HEAD_DIM
SLAB_Q = 0
SLAB_KV = 8
SLAB_FOX = 20
TILE_CMP_KV = 2
PROJ_ROW_CHUNKS = 4
FFN_ROW_CHUNKS = 4
LANE_LOGF = COLS_NSA_GATE

VMEM_LIMIT_BYTES = 56 * 1024 * 1024


def _cparams(n_axes):
    return pltpu.CompilerParams(dimension_semantics=("arbitrary",) * n_axes,
                                vmem_limit_bytes=VMEM_LIMIT_BYTES)


def _dot(a, b):
    return jnp.dot(a, b, preferred_element_type=F32)


def _dot_nt(a, b):
    return lax.dot_general(a, b, (((1,), (1,)), ((), ())), preferred_element_type=F32)


def _row_rms(x, width):
    return x * lax.rsqrt(jnp.sum(x * x, axis=-1, keepdims=True) * (1.0 / width) + EPS)


def _ffn_body(x_ref, g_ref, wg_ref, wu_ref, wd_ref, o_ref, h_ref):
    j = pl.program_id(1)

    def step(first):
        chunks = FFN_ROW_CHUNKS if first else 1
        cm = o_ref.shape[0] // chunks
        for c in range(chunks):
            rs = slice(c * cm, (c + 1) * cm)
            if first:
                base = x_ref[rs, :]
                h = (_row_rms(base, base.shape[-1]) * g_ref[...]).astype(BF16)
                h_ref[rs, :] = h
            else:
                base = o_ref[rs, :]
                h = h_ref[rs, :]
            a = _dot(h, wg_ref[...])
            b = _dot(h, wu_ref[...])
            u = (0.5 * a * jax.nn.sigmoid(a)) * b
            o_ref[rs, :] = base + _dot(u.astype(BF16), wd_ref[...])

    @pl.when(j == 0)
    def _():
        step(True)

    @pl.when(j > 0)
    def _():
        step(False)


def _ffn(x, g, wg, wu, wd, layer, tm=1024, tf=512):
    n, d = x.shape
    f = wg.shape[-1]
    return pl.pallas_call(
        _ffn_body,
        grid=(n // tm, f // tf),
        in_specs=[
            pl.BlockSpec((tm, d), lambda i, j: (i, 0)),
            pl.BlockSpec((None, 1, d), lambda i, j: (layer, 0, 0)),
            pl.BlockSpec((None, d, tf), lambda i, j: (layer, 0, j)),
            pl.BlockSpec((None, d, tf), lambda i, j: (layer, 0, j)),
            pl.BlockSpec((None, tf, d), lambda i, j: (layer, j, 0)),
        ],
        out_specs=pl.BlockSpec((tm, d), lambda i, j: (i, 0)),
        out_shape=jax.ShapeDtypeStruct((n, d), F32),
        scratch_shapes=[pltpu.VMEM((tm, d), BF16)],
        compiler_params=_cparams(2),
        name="ffn",
    )(x, g, wg, wu, wd)


def _rope(y, cos, s1, s2):
    return y * cos + pltpu.roll(y, HEAD_DIM - ROPE_HALF, 1) * s1 + pltpu.roll(y, ROPE_HALF, 1) * s2


def _proj_body(x_ref, g_ref, w_ref, ws_ref, gain_ref, cos_ref, s1_ref, s2_ref, bias_ref,
               p_ref, kc_ref, sm_ref, h_ref):
    j = pl.program_id(1)

    @pl.when(j == 0)
    def _():
        x = x_ref[...]
        h = (_row_rms(x, x.shape[-1]) * g_ref[...]).astype(BF16)
        h_ref[...] = h
        y = _dot(h, ws_ref[...])
        lane = lax.broadcasted_iota(jnp.int32, y.shape, 1)
        z = y + bias_ref[...]
        logsig = jnp.minimum(z, 0.0) - jnp.log1p(jnp.exp(-jnp.abs(z)))
        small = jnp.where(lane < LANE_LOGF, jax.nn.sigmoid(y), logsig)
        sm_ref[0] = small
        sm_ref[1] = pltpu.roll(small, HEAD_DIM - 3 * GQA_GROUP, 1)

    gain = gain_ref[...]
    tm = h_ref.shape[0]
    cm = tm // PROJ_ROW_CHUNKS

    def raw(y, hh, rs):
        return y

    def normed(y, hh, rs):
        return _row_rms(y, HEAD_DIM) * gain[:, hh * HEAD_DIM:(hh + 1) * HEAD_DIM]

    def roped(y, hh, rs):
        return _rope(normed(y, hh, rs), cos_ref[rs, :], s1_ref[rs, :], s2_ref[rs, :])

    def emit(fns, keep_f32=False):
        for c in range(PROJ_ROW_CHUNKS):
            rs = slice(c * cm, (c + 1) * cm)
            acc = _dot(h_ref[rs, :], w_ref[...])
            for hh, fn in enumerate(fns):
                y = acc[:, hh * HEAD_DIM:(hh + 1) * HEAD_DIM]
                p_ref[hh, rs, :] = fn(y, hh, rs).astype(p_ref.dtype)
                if keep_f32:
                    kc_ref[hh, rs, :] = y

    @pl.when(j < 2)
    def _():
        emit([roped] * 4)

    @pl.when(j == TILE_CMP_KV)
    def _():
        emit([raw] * 4, keep_f32=True)

    @pl.when((j == 3) | (j == 4))
    def _():
        emit([roped, roped, raw, raw])

    @pl.when((j >= 5) & (j < 9))
    def _():
        emit([normed] * 4)

    @pl.when(j >= 9)
    def _():
        emit([raw] * 4)


def _proj(x, g, w_main, w_small, gains, cos, s1, s2, bias, layer, seq, tm=1024):
    n, d = x.shape
    tm = min(tm, seq)
    tpb = seq // tm
    hp = HEADS_PER_TILE
    return pl.pallas_call(
        _proj_body,
        grid=(n // tm, N_PROJ_TILES),
        in_specs=[
            pl.BlockSpec((tm, d), lambda i, j: (i, 0)),
            pl.BlockSpec((None, 1, d), lambda i, j: (layer, 0, 0)),
            pl.BlockSpec((None, d, PROJ_TN), lambda i, j: (layer, 0, j)),
            pl.BlockSpec((None, d, HEAD_DIM), lambda i, j: (layer, 0, 0)),
            pl.BlockSpec((None, None, 1, PROJ_TN), lambda i, j: (layer, j, 0, 0)),
            pl.BlockSpec((tm, HEAD_DIM), lambda i, j: (i % tpb, 0)),
            pl.BlockSpec((tm, HEAD_DIM), lambda i, j: (i % tpb, 0)),
            pl.BlockSpec((tm, HEAD_DIM), lambda i, j: (i % tpb, 0)),
            pl.BlockSpec((None, 1, HEAD_DIM), lambda i, j: (layer, 0, 0)),
        ],
        out_specs=[
            pl.BlockSpec((hp, tm, HEAD_DIM), lambda i, j: (j, i, 0)),
            pl.BlockSpec((hp, tm, HEAD_DIM), lambda i, j: (0, i, 0)),
            pl.BlockSpec((N_KV_NSA, tm, HEAD_DIM), lambda i, j: (0, i, 0)),
        ],
        out_shape=[
            jax.ShapeDtypeStruct((N_SLABS, n, HEAD_DIM), BF16),
            jax.ShapeDtypeStruct((hp, n, HEAD_DIM), F32),
            jax.ShapeDtypeStruct((N_KV_NSA, n, HEAD_DIM), F32),
        ],
        scratch_shapes=[pltpu.VMEM((tm, d), BF16)],
        compiler_params=_cparams(2),
        name="proj",
    )(x, g, w_main, w_small, gains, cos, s1, s2, bias)


def _cumsum_body(x_ref, ct_ref):
    x = x_ref[0, 0]
    t = x.shape[0]
    row = lax.broadcasted_iota(jnp.int32, x.shape, 0)
    s = 1
    while s < t:
        x = x + jnp.where(row >= s, pltpu.roll(x, s, 0), 0.0)
        s *= 2
    ct_ref[0] = x.T[LANE_LOGF:LANE_LOGF + N_HEADS_FOX, :] * LOG2E


def _cumsum(sm4):
    _, b, t, _ = sm4.shape
    return pl.pallas_call(
        _cumsum_body,
        grid=(b,),
        in_specs=[pl.BlockSpec((1, 1, t, HEAD_DIM), lambda i: (0, i, 0, 0))],
        out_specs=pl.BlockSpec((1, N_HEADS_FOX, t), lambda i: (i, 0, 0)),
        out_shape=jax.ShapeDtypeStruct((b, N_HEADS_FOX, t), F32),
        compiler_params=_cparams(1),
        name="fox_cumsum",
    )(sm4)


def _gelu_tanh(x):
    c = float(np.sqrt(2.0 / np.pi))
    return x * (0.5 * (1.0 + jnp.tanh(c * (x + 0.044715 * (x * x * x)))))


def _compress_body(r_ref, pos_ref, w1_ref, w2_ref, kn_ref, cos_ref, s1_ref, s2_ref, o_ref):
    is_key = pl.program_id(0) == 0
    r = r_ref[0, 0]
    half = r.shape[-1]
    a = _dot((r + pos_ref[0]).astype(BF16), w1_ref[:half, :])
    bm = _dot((r + pos_ref[1]).astype(BF16), w1_ref[half:, :])
    h = a + pltpu.roll(bm, bm.shape[0] - 1, 0)
    y = _dot(_gelu_tanh(h).astype(BF16), w2_ref[...])

    @pl.when(is_key)
    def _():
        yk = _row_rms(y, HEAD_DIM) * kn_ref[...]
        o_ref[0, 0, 0] = _rope(yk, cos_ref[...], s1_ref[...], s2_ref[...]).astype(o_ref.dtype)

    @pl.when(jnp.logical_not(is_key))
    def _():
        o_ref[0, 0, 0] = y.astype(o_ref.dtype)


def _compress(kr, pos, w1, w2, knorm0, cos_c, s1_c, s2_c, layer):
    _, b, nr, width = kr.shape
    return pl.pallas_call(
        _compress_body,
        grid=(2, b, N_KV_NSA),
        in_specs=[
            pl.BlockSpec((1, 1, nr, width), lambda ty, i, k: (ty * 2 + k, i, 0, 0)),
            pl.BlockSpec((None, None, 2, 1, width), lambda ty, i, k: (layer, ty, 0, 0, 0)),
            pl.BlockSpec((None, None, 2 * width, CMP_HIDDEN), lambda ty, i, k: (layer, ty, 0, 0)),
            pl.BlockSpec((None, None, CMP_HIDDEN, HEAD_DIM), lambda ty, i, k: (layer, ty, 0, 0)),
            pl.BlockSpec((None, 1, HEAD_DIM), lambda ty, i, k: (layer, 0, 0)),
            pl.BlockSpec((nr, HEAD_DIM), lambda ty, i, k: (0, 0)),
            pl.BlockSpec((nr, HEAD_DIM), lambda ty, i, k: (0, 0)),
            pl.BlockSpec((nr, HEAD_DIM), lambda ty, i, k: (0, 0)),
        ],
        out_specs=pl.BlockSpec((1, 1, 1, nr, HEAD_DIM), lambda ty, i, k: (ty, i, k, 0, 0)),
        out_shape=jax.ShapeDtypeStruct((2, b, N_KV_NSA, nr, HEAD_DIM), BF16),
        compiler_params=_cparams(3),
        name="compress",
    )(kr, pos, w1, w2, knorm0, cos_c, s1_c, s2_c)


def _split3(x):
    hi = x.astype(BF16)
    r1 = x - hi.astype(F32)
    mid = r1.astype(BF16)
    lo = (r1 - mid.astype(F32)).astype(BF16)
    return hi, mid, lo


def _cmp_branch(q, kc, vc, ovt, i, tq, n_blk):
    rows = q.shape[0]
    g = rows // tq
    s = _dot_nt(q, kc)
    row_g = lax.broadcasted_iota(jnp.int32, (rows, HEAD_DIM), 0)
    lane_g = lax.broadcasted_iota(jnp.int32, (rows, HEAD_DIM), 1)
    t_g = i * tq + (row_g & (tq - 1))
    valid = (lane_g * CMP_STRIDE + (CMP_BLOCK - 1)) <= t_g
    s = jnp.where(valid, s, NEG_INF)
    m = jnp.max(s, axis=-1, keepdims=True)
    p = jnp.where(valid, jnp.exp2(s - m), 0.0)
    p = p / jnp.maximum(jnp.sum(p, axis=-1, keepdims=True), 1e-30)
    o = _dot(p.astype(BF16), vc)
    psum = p[0:tq]
    for gg in range(1, g):
        psum = psum + p[gg * tq:(gg + 1) * tq]

    hi, mid, lo = _split3(psum)
    imp = ((_dot_nt(ovt, hi) + _dot_nt(ovt, mid)) + _dot_nt(ovt, lo))[:n_blk]
    blk = lax.broadcasted_iota(jnp.int32, (n_blk, tq), 0)
    t_blk = (i * tq + lax.broadcasted_iota(jnp.int32, (n_blk, tq), 1)) // SLC_BLOCK
    causal = blk <= t_blk
    forced = (blk == 0) | (causal & (blk > t_blk - N_LOCAL_SLC))
    score = jnp.where(forced, FORCED_SCORE, jnp.where(causal, imp, NEG_INF))
    cnt = jnp.zeros((n_blk, tq), jnp.int32)
    for c in range(n_blk):
        other = score[c:c + 1, :]
        ahead = (other > score) | ((other == score) & (blk > c))
        cnt = cnt + ahead.astype(jnp.int32)
    picked = jnp.where(cnt < SLC_TOPK, 1.0, 0.0)
    picked = jnp.concatenate([picked, jnp.zeros((HEAD_DIM - n_blk, tq), F32)], axis=0)
    return o, picked.T.astype(BF16)


def _online_update(s, v, m_ref, l_ref, acc_ref, rows=slice(None)):
    cols = [s[:, c:c + HEAD_DIM] for c in range(0, s.shape[1], HEAD_DIM)]
    m_el = functools.reduce(jnp.maximum, cols)
    m_prev = m_ref[rows]
    m_new = jnp.maximum(m_prev, jnp.max(m_el, axis=-1, keepdims=True))
    alpha = jnp.exp2(m_prev - m_new)
    ps = [jnp.exp2(c - m_new) for c in cols]
    l_ref[rows] = alpha * l_ref[rows] + functools.reduce(jnp.add, ps)
    p = jnp.concatenate([x.astype(BF16) for x in ps], axis=1)
    acc_ref[rows] = alpha * acc_ref[rows] + _dot(p, v)
    m_ref[rows] = m_new


def _normalize(acc_ref, l_ref):
    l = jnp.sum(l_ref[...], axis=-1, keepdims=True)
    return acc_ref[...] / jnp.maximum(l, 1e-30)


def _init_online(m_ref, l_ref, acc_ref):
    m_ref[...] = jnp.full(m_ref.shape, NEG_INF, F32)
    l_ref[...] = jnp.zeros(l_ref.shape, F32)
    acc_ref[...] = jnp.zeros(acc_ref.shape, F32)


def _flash_scratch(rows):
    return [pltpu.VMEM((rows, HEAD_DIM), F32)] * 3


def _win_branch(q, k_ref, v_ref, i, tq, band):
    rows = q.shape[0]
    g = rows // tq
    start = pl.multiple_of(jnp.maximum(i * tq - WINDOW, 0), tq)
    k = k_ref[0, 0, pl.ds(start, band), :]
    v = v_ref[0, 0, pl.ds(start, band), :]
    row = lax.broadcasted_iota(jnp.int32, (tq, band), 0)
    col = lax.broadcasted_iota(jnp.int32, (tq, band), 1)
    kpos = start + col
    tpos = i * tq + row
    live = (kpos <= tpos) & (kpos > tpos - WINDOW)
    bias = jnp.where(live, 0.0, NEG_INF)
    s = (_dot_nt(q, k).reshape(g, tq, band) + bias[None]).reshape(rows, band)
    m = jnp.max(s, axis=-1, keepdims=True)
    p = jnp.exp2(s - m)
    l = jnp.sum(p, axis=-1, keepdims=True)
    return _dot(p.astype(BF16), v) / jnp.maximum(l, 1e-30)


def _nsa_body(q_ref, kc_ref, vc_ref, ks_ref, vs_ref, kw_ref, vw_ref, gt_ref, ov_ref, e_ref, o_ref,
              m_ref, l_ref, acc_ref, part_ref, sel_ref, *, tq, tk, band, n_blk):
    i = pl.program_id(2)
    g = q_ref.shape[0]
    rows = g * tq
    q = q_ref[...].reshape(rows, HEAD_DIM)
    gt = gt_ref[0, 0]

    def gate(branch):
        return jnp.concatenate(
            [jnp.broadcast_to(gt[:, 3 * gg + branch:3 * gg + branch + 1], (tq, HEAD_DIM)) for gg in range(g)],
            axis=0)

    o_cmp, sel = _cmp_branch(q, kc_ref[0, 0, 0], vc_ref[0, 0, 0], ov_ref[...], i, tq, n_blk)
    sel_ref[...] = sel
    part_ref[...] = gate(0) * o_cmp
    part_ref[...] += gate(2) * _win_branch(q, kw_ref, vw_ref, i, tq, band)

    _init_online(m_ref, l_ref, acc_ref)
    row = lax.broadcasted_iota(jnp.int32, (tq, tk), 0)
    col = lax.broadcasted_iota(jnp.int32, (tq, tk), 1)
    last = (i * tq) // tk

    def step(kc, diagonal):
        start = pl.multiple_of(kc * tk, tk)
        k = ks_ref[0, 0, pl.ds(start, tk), :]
        v = vs_ref[0, 0, pl.ds(start, tk), :]
        live = _dot(sel_ref[...], e_ref[kc]) > 0.5
        if diagonal:
            live = live & ((start + col) <= (i * tq + row))
        bias = jnp.where(live, 0.0, NEG_INF)
        s = _dot_nt(q_ref[...].reshape(rows, HEAD_DIM), k).reshape(g, tq, tk) + bias[None]
        _online_update(s.reshape(rows, tk), v, m_ref, l_ref, acc_ref)

    def body(kc, carry):
        step(kc, False)
        return carry

    lax.fori_loop(0, last, body, 0)
    step(last, True)
    y = part_ref[...] + gate(1) * _normalize(acc_ref, l_ref)
    for gg in range(g):
        o_ref[0, :, gg * HEAD_DIM:(gg + 1) * HEAD_DIM] = y[gg * tq:(gg + 1) * tq]


def _nsa(p4, kvc, sm4, ovt, e, tq=256):
    _, b, t, _ = p4.shape
    nr = kvc.shape[3]
    tk = e.shape[-1]
    band = min(WINDOW + tq, t)
    width = GQA_GROUP * HEAD_DIM
    rows = GQA_GROUP * tq

    def kv(slab):
        return pl.BlockSpec((1, 1, t, HEAD_DIM), lambda bb, k, i: (slab + k, bb, 0, 0))

    return pl.pallas_call(
        functools.partial(_nsa_body, tq=tq, tk=tk, band=band, n_blk=t // SLC_BLOCK),
        grid=(b, N_KV_NSA, t // tq),
        in_specs=[
            pl.BlockSpec((GQA_GROUP, None, tq, HEAD_DIM), lambda bb, k, i: (k, bb, i, 0)),
            pl.BlockSpec((1, 1, 1, nr, HEAD_DIM), lambda bb, k, i: (0, bb, k, 0, 0)),
            pl.BlockSpec((1, 1, 1, nr, HEAD_DIM), lambda bb, k, i: (1, bb, k, 0, 0)),
            kv(SLAB_KV + 4), kv(SLAB_KV + 6), kv(SLAB_KV + 8), kv(SLAB_KV + 10),
            pl.BlockSpec((1, 1, tq, HEAD_DIM), lambda bb, k, i: (k, bb, i, 0)),
            pl.BlockSpec((HEAD_DIM, HEAD_DIM), lambda bb, k, i: (0, 0)),
            pl.BlockSpec(e.shape, lambda bb, k, i: (0, 0, 0)),
        ],
        out_specs=pl.BlockSpec((1, tq, width), lambda bb, k, i: (bb, i, k)),
        out_shape=jax.ShapeDtypeStruct((b, t, NSA_WIDTH), F32),
        scratch_shapes=_flash_scratch(rows) + [pltpu.VMEM((rows, HEAD_DIM), F32),
                                               pltpu.VMEM((tq, HEAD_DIM), BF16)],
        compiler_params=_cparams(3),
        name="nsa",
    )(p4, kvc, kvc, p4, p4, p4, p4, sm4, ovt, e)


def _fox_body(q_ref, k_ref, v_ref, ct_ref, o_ref, m_ref, l_ref, acc_ref, *, tq):
    i = pl.program_id(2)
    nh = q_ref.shape[0]
    _init_online(m_ref, l_ref, acc_ref)
    row = lax.broadcasted_iota(jnp.int32, (tq, tq), 0)
    col = lax.broadcasted_iota(jnp.int32, (tq, tq), 1)

    def step(kc, diagonal):
        start = pl.multiple_of(kc * tq, tq)
        for hh in range(nh):
            k = k_ref[hh, 0, pl.ds(start, tq), :]
            v = v_ref[hh, 0, pl.ds(start, tq), :]
            s = _dot_nt(q_ref[hh], k) - ct_ref[hh, pl.ds(kc, 1), :]
            if diagonal:
                s = jnp.where(col <= row, s, NEG_INF)
            _online_update(s, v, m_ref, l_ref, acc_ref, slice(hh * tq, (hh + 1) * tq))

    def body(kc, carry):
        step(kc, False)
        return carry

    lax.fori_loop(0, i, body, 0)
    step(i, True)
    o = _normalize(acc_ref, l_ref)
    for hh in range(nh):
        o_ref[0, :, hh * HEAD_DIM:(hh + 1) * HEAD_DIM] = o[hh * tq:(hh + 1) * tq]


FOX_HEADS_PER_STEP = 4


def _fox_attn(p4, ctr, tq=512):
    _, b, t, _ = p4.shape
    nh = N_HEADS_FOX
    hs = FOX_HEADS_PER_STEP
    q0, k0, v0 = (SLAB_FOX // hs, (SLAB_FOX + nh) // hs, (SLAB_FOX + 2 * nh) // hs)
    return pl.pallas_call(
        functools.partial(_fox_body, tq=tq),
        grid=(b, nh // hs, t // tq),
        in_specs=[
            pl.BlockSpec((hs, None, tq, HEAD_DIM), lambda bb, h, i: (q0 + h, bb, i, 0)),
            pl.BlockSpec((hs, 1, t, HEAD_DIM), lambda bb, h, i: (k0 + h, bb, 0, 0)),
            pl.BlockSpec((hs, 1, t, HEAD_DIM), lambda bb, h, i: (v0 + h, bb, 0, 0)),
            pl.BlockSpec((hs, t // tq, tq), lambda bb, h, i: (bb * (nh // hs) + h, 0, 0)),
        ],
        out_specs=pl.BlockSpec((1, tq, hs * HEAD_DIM), lambda bb, h, i: (bb, i, h)),
        out_shape=jax.ShapeDtypeStruct((b, t, FOX_WIDTH), F32),
        scratch_shapes=_flash_scratch(hs * tq),
        compiler_params=_cparams(3),
        name="fox_attn",
    )(p4, p4, p4, ctr)


def _out_body(on_ref, of_ref, x_ref, nn_ref, fn_ref, w_ref, o_ref):
    nsa = (_row_rms(on_ref[...], NSA_WIDTH) * nn_ref[...]).astype(BF16)
    fox = (_row_rms(of_ref[...], FOX_WIDTH) * fn_ref[...]).astype(BF16)
    o_ref[...] = x_ref[...] + (_dot(nsa, w_ref[:NSA_WIDTH, :]) + _dot(fox, w_ref[NSA_WIDTH:, :]))


def _out(on, of, x, nn, fn, w, layer, tm=512):
    n, d = x.shape
    return pl.pallas_call(
        _out_body,
        grid=(n // tm,),
        in_specs=[
            pl.BlockSpec((tm, NSA_WIDTH), lambda i: (i, 0)),
            pl.BlockSpec((tm, FOX_WIDTH), lambda i: (i, 0)),
            pl.BlockSpec((tm, d), lambda i: (i, 0)),
            pl.BlockSpec((None, 1, NSA_WIDTH), lambda i: (layer, 0, 0)),
            pl.BlockSpec((None, 1, FOX_WIDTH), lambda i: (layer, 0, 0)),
            pl.BlockSpec((None, NSA_WIDTH + FOX_WIDTH, d), lambda i: (layer, 0, 0)),
        ],
        out_specs=pl.BlockSpec((tm, d), lambda i: (i, 0)),
        out_shape=jax.ShapeDtypeStruct((n, d), F32),
        compiler_params=_cparams(1),
        name="out_proj",
    )(on, of, x, nn, fn, w)


def _rope_tables(pos):
    inv = ROPE_THETA ** (-jnp.arange(0, ROPE_DIM, 2, dtype=F32) / ROPE_DIM)
    ang = pos.astype(F32)[:, None] * inv[None, :]
    cos, sin = jnp.cos(ang), jnp.sin(ang)
    n = pos.shape[0]
    pad = HEAD_DIM - ROPE_DIM
    cos_t = jnp.concatenate([cos, cos, jnp.ones((n, pad), F32)], axis=-1)
    s1 = jnp.concatenate([-sin, jnp.zeros((n, HEAD_DIM - ROPE_HALF), F32)], axis=-1)
    s2 = jnp.concatenate([jnp.zeros((n, ROPE_HALF), F32), sin, jnp.zeros((n, pad), F32)], axis=-1)
    return cos_t, s1, s2


def _overlap_matrix(t):
    n_cmp = (t - CMP_BLOCK) // CMP_STRIDE + 1
    n_slc = t // SLC_BLOCK
    cmp_start = np.arange(HEAD_DIM) * CMP_STRIDE
    slc_start = np.arange(HEAD_DIM) * SLC_BLOCK
    ov = ((cmp_start[:, None] < slc_start[None, :] + SLC_BLOCK)
          & (cmp_start[:, None] + CMP_BLOCK > slc_start[None, :]))
    ov = ov & (np.arange(HEAD_DIM)[:, None] < n_cmp) & (np.arange(HEAD_DIM)[None, :] < n_slc)
    return jnp.asarray(ov.T.astype(np.float32), BF16)


def _expand_matrix(t, tk):
    kpos = np.arange(t).reshape(t // tk, 1, tk)
    j = np.arange(HEAD_DIM).reshape(1, HEAD_DIM, 1)
    return jnp.asarray((kpos // SLC_BLOCK == j).astype(np.float32), BF16)


def _mixer(x2, b, t, layer, wts, tabs):
    n = x2.shape[0]
    p3, kc32, sm = _proj(x2, wts["mix_norm"], wts["w_main"], wts["w_small"], wts["gains"],
                         tabs["cos"], tabs["s1"], tabs["s2"], wts["fbias"], layer, t)
    p4 = p3.reshape(N_SLABS, b, t, HEAD_DIM)
    kr = kc32.reshape(HEADS_PER_TILE, b, t // CMP_STRIDE, CMP_STRIDE * HEAD_DIM)
    sm4 = sm.reshape(N_KV_NSA, b, t, HEAD_DIM)
    ct = _cumsum(sm4)
    fox_tq = min(512, t)
    ctr = ct.reshape(b * N_HEADS_FOX, t // fox_tq, fox_tq)
    kvc = _compress(kr, wts["pos"], wts["cmp_w1"], wts["cmp_w2"], wts["knorm0"],
                    tabs["cos_c"], tabs["s1_c"], tabs["s2_c"], layer)
    o_nsa = _nsa(p4, kvc, sm4, tabs["ov"], tabs["e"])
    o_fox = _fox_attn(p4, ctr, fox_tq)
    return _out(o_nsa.reshape(n, NSA_WIDTH), o_fox.reshape(n, FOX_WIDTH), x2, wts["nsa_out_norm"],
                wts["fox_out_norm"], wts["w_out"], layer)


def _prep_weights(ffn1_norm, ffn1_w_gate, ffn1_w_up, ffn1_w_down, mix_norm, w_in, nsa_q_norm, nsa_k_norm,
                  cmp_pos_emb, cmp_w1, cmp_w2, nsa_out_norm, fox_q_norm, fox_k_norm, fox_forget_bias,
                  fox_out_norm, w_out, ffn2_norm, ffn2_w_gate, ffn2_w_up, ffn2_w_down):
    depth, d, _ = w_in.shape
    c0 = COLS_NSA_Q + COLS_NSA_KV
    c1 = c0 + COLS_NSA_GATE
    c2 = c1 + COLS_FOX_QKV
    w_small = jnp.concatenate(
        [w_in[:, :, c0:c1], w_in[:, :, c2:], jnp.zeros((depth, d, HEAD_DIM - COLS_NSA_GATE - COLS_FOX_F), F32)],
        axis=-1).astype(BF16)
    w_main = jnp.concatenate([w_in[:, :, :c0], w_in[:, :, c1:c2]], axis=-1).astype(BF16)

    def tile(v):
        return jnp.concatenate([v] * HEADS_PER_TILE, axis=-1)

    one = jnp.ones((depth, HEAD_DIM), F32)
    gains = [tile(one)] * N_PROJ_TILES
    gains[0] = gains[1] = tile(nsa_q_norm * Q_SCALE)
    gains[3] = jnp.concatenate([nsa_k_norm[:, 1]] * 2 + [one] * 2, axis=-1)
    gains[4] = jnp.concatenate([nsa_k_norm[:, 2]] * 2 + [one] * 2, axis=-1)
    gains[5] = gains[6] = tile(fox_q_norm * Q_SCALE)
    gains[7] = gains[8] = tile(fox_k_norm)
    gains = jnp.stack(gains, axis=1)[:, :, None, :]

    fbias = jnp.zeros((depth, 1, HEAD_DIM), F32).at[:, 0, LANE_LOGF:LANE_LOGF + N_HEADS_FOX].set(
        fox_forget_bias)
    half = CMP_BLOCK * HEAD_DIM // 2
    pos = cmp_pos_emb.reshape(depth, 2, 2, 1, half)
    return dict(
        ffn1=(ffn1_norm[:, None, :], ffn1_w_gate.astype(BF16), ffn1_w_up.astype(BF16), ffn1_w_down.astype(BF16)),
        ffn2=(ffn2_norm[:, None, :], ffn2_w_gate.astype(BF16), ffn2_w_up.astype(BF16), ffn2_w_down.astype(BF16)),
        mix_norm=mix_norm[:, None, :], w_main=w_main, w_small=w_small, gains=gains, fbias=fbias, pos=pos,
        cmp_w1=cmp_w1.astype(BF16), cmp_w2=cmp_w2.astype(BF16), knorm0=nsa_k_norm[:, 0][:, None, :],
        nsa_out_norm=nsa_out_norm[:, None, :], fox_out_norm=fox_out_norm[:, None, :],
        w_out=w_out.astype(BF16),
    )


def _tables(t):
    cos, s1, s2 = _rope_tables(jnp.arange(t))
    cmp_end = jnp.arange(t // CMP_STRIDE) * CMP_STRIDE + (CMP_BLOCK - 1)
    cos_c, s1_c, s2_c = _rope_tables(cmp_end)
    return dict(cos=cos, s1=s1, s2=s2, cos_c=cos_c, s1_c=s1_c, s2_c=s2_c,
                ov=_overlap_matrix(t), e=_expand_matrix(t, min(512, t)))


def kernel(x, ffn1_norm, ffn1_w_gate, ffn1_w_up, ffn1_w_down, mix_norm, w_in, nsa_q_norm, nsa_k_norm, cmp_pos_emb, cmp_w1, cmp_w2, nsa_out_norm, fox_q_norm, fox_k_norm, fox_forget_bias, fox_out_norm, w_out, ffn2_norm, ffn2_w_gate, ffn2_w_up, ffn2_w_down):
    b, t, d = x.shape
    depth = w_in.shape[0]
    wts = _prep_weights(ffn1_norm, ffn1_w_gate, ffn1_w_up, ffn1_w_down, mix_norm, w_in, nsa_q_norm,
                        nsa_k_norm, cmp_pos_emb, cmp_w1, cmp_w2, nsa_out_norm, fox_q_norm, fox_k_norm,
                        fox_forget_bias, fox_out_norm, w_out, ffn2_norm, ffn2_w_gate, ffn2_w_up, ffn2_w_down)
    tabs = _tables(t)
    x2 = x.reshape(b * t, d)
    for layer in range(depth):
        x2 = _ffn(x2, *wts["ffn1"], layer)
        x2 = _mixer(x2, b, t, layer, wts, tabs)
        x2 = _ffn(x2, *wts["ffn2"], layer)
    return x2.reshape(b, t, d)
```

```python
import functools

import numpy as np
import jax
import jax.numpy as jnp
from jax import lax
from jax.experimental import pallas as pl
from jax.experimental.pallas import tpu as pltpu

F32 = jnp.float32
BF16 = jnp.bfloat16

HEAD_DIM = 128
N_HEADS_NSA = 8
N_KV_NSA = 2
GQA_GROUP = 4
N_HEADS_FOX = 8
NSA_WIDTH = N_HEADS_NSA * HEAD_DIM
FOX_WIDTH = N_HEADS_FOX * HEAD_DIM
ROPE_DIM = 32
ROPE_HALF = ROPE_DIM // 2
ROPE_THETA = 500000.0
CMP_BLOCK = 32
CMP_STRIDE = 16
CMP_HIDDEN = 256
SLC_BLOCK = 64
SLC_TOPK = 8
N_LOCAL_SLC = 2
WINDOW = 512
EPS = 1e-6
NEG_INF = -1e30
FORCED_SCORE = 1e9
SCALE = HEAD_DIM ** -0.5
LOG2E = float(np.log2(np.e))
Q_SCALE = SCALE * LOG2E

COLS_NSA_Q = NSA_WIDTH
COLS_NSA_KV = 3 * 2 * N_KV_NSA * HEAD_DIM
COLS_NSA_GATE = 3 * N_HEADS_NSA
COLS_FOX_QKV = 3 * FOX_WIDTH
COLS_FOX_F = N_HEADS_FOX
COLS_MAIN = COLS_NSA_Q + COLS_NSA_KV + COLS_FOX_QKV

HEADS_PER_TILE = 4
PROJ_TN = HEADS_PER_TILE * HEAD_DIM
N_PROJ_TILES = COLS_MAIN // PROJ_TN
N_SLABS = COLS_MAIN // HEAD_DIM
SLAB_Q = 0
SLAB_KV = 8
SLAB_FOX = 20
TILE_CMP_KV = 2
PROJ_ROW_CHUNKS = 4
FFN_ROW_CHUNKS = 4
LANE_LOGF = COLS_NSA_GATE

VMEM_LIMIT_BYTES = 56 * 1024 * 1024


def _cparams(n_axes):
    return pltpu.CompilerParams(dimension_semantics=("arbitrary",) * n_axes,
                                vmem_limit_bytes=VMEM_LIMIT_BYTES)


def _dot(a, b):
    return jnp.dot(a, b, preferred_element_type=F32)


def _dot_nt(a, b):
    return lax.dot_general(a, b, (((1,), (1,)), ((), ())), preferred_element_type=F32)


def _row_rms(x, width):
    return x * lax.rsqrt(jnp.sum(x * x, axis=-1, keepdims=True) * (1.0 / width) + EPS)


def _ffn_body(x_ref, g_ref, wg_ref, wu_ref, wd_ref, o_ref, h_ref):
    j = pl.program_id(1)

    def step(first):
        chunks = FFN_ROW_CHUNKS if first else 1
        cm = o_ref.shape[0] // chunks
        for c in range(chunks):
            rs = slice(c * cm, (c + 1) * cm)
            if first:
                base = x_ref[rs, :]
                h = (_row_rms(base, base.shape[-1]) * g_ref[...]).astype(BF16)
                h_ref[rs, :] = h
            else:
                base = o_ref[rs, :]
                h = h_ref[rs, :]
            a = _dot(h, wg_ref[...])
            b = _dot(h, wu_ref[...])
            u = (0.5 * a * jax.nn.sigmoid(a)) * b
            o_ref[rs, :] = base + _dot(u.astype(BF16), wd_ref[...])

    @pl.when(j == 0)
    def _():
        step(True)

    @pl.when(j > 0)
    def _():
        step(False)


def _ffn(x, g, wg, wu, wd, layer, tm=1024, tf=512):
    n, d = x.shape
    f = wg.shape[-1]
    return pl.pallas_call(
        _ffn_body,
        grid=(n // tm, f // tf),
        in_specs=[
            pl.BlockSpec((tm, d), lambda i, j: (i, 0)),
            pl.BlockSpec((None, 1, d), lambda i, j: (layer, 0, 0)),
            pl.BlockSpec((None, d, tf), lambda i, j: (layer, 0, j)),
            pl.BlockSpec((None, d, tf), lambda i, j: (layer, 0, j)),
            pl.BlockSpec((None, tf, d), lambda i, j: (layer, j, 0)),
        ],
        out_specs=pl.BlockSpec((tm, d), lambda i, j: (i, 0)),
        out_shape=jax.ShapeDtypeStruct((n, d), F32),
        scratch_shapes=[pltpu.VMEM((tm, d), BF16)],
        compiler_params=_cparams(2),
        name="ffn",
    )(x, g, wg, wu, wd)


def _rope(y, cos, s1, s2):
    return y * cos + pltpu.roll(y, HEAD_DIM - ROPE_HALF, 1) * s1 + pltpu.roll(y, ROPE_HALF, 1) * s2


def _proj_body(x_ref, g_ref, w_ref, ws_ref, gain_ref, cos_ref, s1_ref, s2_ref, bias_ref,
               p_ref, kc_ref, sm_ref, h_ref):
    j = pl.program_id(1)

    @pl.when(j == 0)
    def _():
        x = x_ref[...]
        h = (_row_rms(x, x.shape[-1]) * g_ref[...]).astype(BF16)
        h_ref[...] = h
        y = _dot(h, ws_ref[...])
        lane = lax.broadcasted_iota(jnp.int32, y.shape, 1)
        z = y + bias_ref[...]
        logsig = jnp.minimum(z, 0.0) - jnp.log1p(jnp.exp(-jnp.abs(z)))
        small = jnp.where(lane < LANE_LOGF, jax.nn.sigmoid(y), logsig)
        sm_ref[0] = small
        sm_ref[1] = pltpu.roll(small, HEAD_DIM - 3 * GQA_GROUP, 1)

    gain = gain_ref[...]
    tm = h_ref.shape[0]

    def raw(y, hh, rs):
        return y

    def normed(y, hh, rs):
        return _row_rms(y, HEAD_DIM) * gain[:, hh * HEAD_DIM:(hh + 1) * HEAD_DIM]

    def roped(y, hh, rs):
        return _rope(normed(y, hh, rs), cos_ref[rs, :], s1_ref[rs, :], s2_ref[rs, :])

    def emit(fns, keep_f32=False):
        chunks = 1 if all(fn is raw for fn in fns) else PROJ_ROW_CHUNKS
        cm = tm // chunks
        for c in range(chunks):
            rs = slice(c * cm, (c + 1) * cm)
            acc = _dot(h_ref[rs, :], w_ref[...])
            for hh, fn in enumerate(fns):
                y = acc[:, hh * HEAD_DIM:(hh + 1) * HEAD_DIM]
                p_ref[hh, rs, :] = fn(y, hh, rs).astype(p_ref.dtype)
                if keep_f32:
                    kc_ref[hh, rs, :] = y

    @pl.when(j < 2)
    def _():
        emit([roped] * 4)

    @pl.when(j == TILE_CMP_KV)
    def _():
        emit([raw] * 4, keep_f32=True)

    @pl.when((j == 3) | (j == 4))
    def _():
        emit([roped, roped, raw, raw])

    @pl.when((j >= 5) & (j < 9))
    def _():
        emit([normed] * 4)

    @pl.when(j >= 9)
    def _():
        emit([raw] * 4)


def _proj(x, g, w_main, w_small, gains, cos, s1, s2, bias, layer, seq, tm=1024):
    n, d = x.shape
    tm = min(tm, seq)
    tpb = seq // tm
    hp = HEADS_PER_TILE
    return pl.pallas_call(
        _proj_body,
        grid=(n // tm, N_PROJ_TILES),
        in_specs=[
            pl.BlockSpec((tm, d), lambda i, j: (i, 0)),
            pl.BlockSpec((None, 1, d), lambda i, j: (layer, 0, 0)),
            pl.BlockSpec((None, d, PROJ_TN), lambda i, j: (layer, 0, j)),
            pl.BlockSpec((None, d, HEAD_DIM), lambda i, j: (layer, 0, 0)),
            pl.BlockSpec((None, None, 1, PROJ_TN), lambda i, j: (layer, j, 0, 0)),
            pl.BlockSpec((tm, HEAD_DIM), lambda i, j: (i % tpb, 0)),
            pl.BlockSpec((tm, HEAD_DIM), lambda i, j: (i % tpb, 0)),
            pl.BlockSpec((tm, HEAD_DIM), lambda i, j: (i % tpb, 0)),
            pl.BlockSpec((None, 1, HEAD_DIM), lambda i, j: (layer, 0, 0)),
        ],
        out_specs=[
            pl.BlockSpec((hp, tm, HEAD_DIM), lambda i, j: (j, i, 0)),
            pl.BlockSpec((hp, tm, HEAD_DIM), lambda i, j: (0, i, 0)),
            pl.BlockSpec((N_KV_NSA, tm, HEAD_DIM), lambda i, j: (0, i, 0)),
        ],
        out_shape=[
            jax.ShapeDtypeStruct((N_SLABS, n, HEAD_DIM), BF16),
            jax.ShapeDtypeStruct((hp, n, HEAD_DIM), F32),
            jax.ShapeDtypeStruct((N_KV_NSA, n, HEAD_DIM), F32),
        ],
        scratch_shapes=[pltpu.VMEM((tm, d), BF16)],
        compiler_params=_cparams(2),
        name="proj",
    )(x, g, w_main, w_small, gains, cos, s1, s2, bias)


def _cumsum_body(x_ref, ct_ref):
    x = x_ref[0, 0]
    t = x.shape[0]
    row = lax.broadcasted_iota(jnp.int32, x.shape, 0)
    s = 1
    while s < t:
        x = x + jnp.where(row >= s, pltpu.roll(x, s, 0), 0.0)
        s *= 2
    ct_ref[0] = x.T[LANE_LOGF:LANE_LOGF + N_HEADS_FOX, :] * LOG2E


def _cumsum(sm4):
    _, b, t, _ = sm4.shape
    return pl.pallas_call(
        _cumsum_body,
        grid=(b,),
        in_specs=[pl.BlockSpec((1, 1, t, HEAD_DIM), lambda i: (0, i, 0, 0))],
        out_specs=pl.BlockSpec((1, N_HEADS_FOX, t), lambda i: (i, 0, 0)),
        out_shape=jax.ShapeDtypeStruct((b, N_HEADS_FOX, t), F32),
        compiler_params=_cparams(1),
        name="fox_cumsum",
    )(sm4)


def _gelu_tanh(x):
    c = float(np.sqrt(2.0 / np.pi))
    return x * (0.5 * (1.0 + jnp.tanh(c * (x + 0.044715 * (x * x * x)))))


def _compress_body(k_ref, pos_ref, w1_ref, w2_ref, kn_ref, cos_ref, s1_ref, s2_ref, o_ref):
    is_key = pl.program_id(0) == 0
    nr = o_ref.shape[-2]
    half = CMP_STRIDE * HEAD_DIM
    a = jnp.zeros((nr, CMP_HIDDEN), F32)
    bm = jnp.zeros((nr, CMP_HIDDEN), F32)
    for l in range(CMP_STRIDE):
        tok = k_ref[0, 0, pl.ds(l, nr, stride=CMP_STRIDE), :]
        sl = slice(l * HEAD_DIM, (l + 1) * HEAD_DIM)
        a = a + _dot((tok + pos_ref[0][:, sl]).astype(BF16), w1_ref[sl, :])
        bm = bm + _dot((tok + pos_ref[1][:, sl]).astype(BF16), w1_ref[half + l * HEAD_DIM:half + (l + 1) * HEAD_DIM, :])
    h = a + pltpu.roll(bm, bm.shape[0] - 1, 0)
    y = _dot(_gelu_tanh(h).astype(BF16), w2_ref[...])

    @pl.when(is_key)
    def _():
        yk = _row_rms(y, HEAD_DIM) * kn_ref[...]
        o_ref[0, 0, 0] = _rope(yk, cos_ref[...], s1_ref[...], s2_ref[...]).astype(o_ref.dtype)

    @pl.when(jnp.logical_not(is_key))
    def _():
        o_ref[0, 0, 0] = y.astype(o_ref.dtype)


def _compress(kr, pos, w1, w2, knorm0, cos_c, s1_c, s2_c, layer):
    _, b, t, _ = kr.shape
    nr = t // CMP_STRIDE
    width = CMP_STRIDE * HEAD_DIM
    return pl.pallas_call(
        _compress_body,
        grid=(2, b, N_KV_NSA),
        in_specs=[
            pl.BlockSpec((1, 1, t, HEAD_DIM), lambda ty, i, k: (ty * 2 + k, i, 0, 0)),
            pl.BlockSpec((None, None, 2, 1, width), lambda ty, i, k: (layer, ty, 0, 0, 0)),
            pl.BlockSpec((None, None, 2 * width, CMP_HIDDEN), lambda ty, i, k: (layer, ty, 0, 0)),
            pl.BlockSpec((None, None, CMP_HIDDEN, HEAD_DIM), lambda ty, i, k: (layer, ty, 0, 0)),
            pl.BlockSpec((None, 1, HEAD_DIM), lambda ty, i, k: (layer, 0, 0)),
            pl.BlockSpec((nr, HEAD_DIM), lambda ty, i, k: (0, 0)),
            pl.BlockSpec((nr, HEAD_DIM), lambda ty, i, k: (0, 0)),
            pl.BlockSpec((nr, HEAD_DIM), lambda ty, i, k: (0, 0)),
        ],
        out_specs=pl.BlockSpec((1, 1, 1, nr, HEAD_DIM), lambda ty, i, k: (ty, i, k, 0, 0)),
        out_shape=jax.ShapeDtypeStruct((2, b, N_KV_NSA, nr, HEAD_DIM), BF16),
        compiler_params=_cparams(3),
        name="compress",
    )(kr, pos, w1, w2, knorm0, cos_c, s1_c, s2_c)


def _split3(x):
    hi = x.astype(BF16)
    r1 = x - hi.astype(F32)
    mid = r1.astype(BF16)
    lo = (r1 - mid.astype(F32)).astype(BF16)
    return hi, mid, lo


def _cmp_branch(q, kc, vc, ovt, i, tq, n_blk):
    rows = q.shape[0]
    g = rows // tq
    s = _dot_nt(q, kc)
    row_g = lax.broadcasted_iota(jnp.int32, (rows, HEAD_DIM), 0)
    lane_g = lax.broadcasted_iota(jnp.int32, (rows, HEAD_DIM), 1)
    t_g = i * tq + (row_g & (tq - 1))
    valid = (lane_g * CMP_STRIDE + (CMP_BLOCK - 1)) <= t_g
    s = jnp.where(valid, s, NEG_INF)
    m = jnp.max(s, axis=-1, keepdims=True)
    p = jnp.where(valid, jnp.exp2(s - m), 0.0)
    p = p / jnp.maximum(jnp.sum(p, axis=-1, keepdims=True), 1e-30)
    o = _dot(p.astype(BF16), vc)
    psum = p[0:tq]
    for gg in range(1, g):
        psum = psum + p[gg * tq:(gg + 1) * tq]

    hi, mid, lo = _split3(psum)
    imp = ((_dot_nt(ovt, hi) + _dot_nt(ovt, mid)) + _dot_nt(ovt, lo))[:n_blk]
    blk = lax.broadcasted_iota(jnp.int32, (n_blk, tq), 0)
    t_blk = (i * tq + lax.broadcasted_iota(jnp.int32, (n_blk, tq), 1)) // SLC_BLOCK
    causal = blk <= t_blk
    forced = (blk == 0) | (causal & (blk > t_blk - N_LOCAL_SLC))
    score = jnp.where(forced, FORCED_SCORE, jnp.where(causal, imp, NEG_INF))
    cnt = jnp.zeros((n_blk, tq), jnp.int32)
    for c in range(n_blk):
        other = score[c:c + 1, :]
        ahead = (other > score) | ((other == score) & (blk > c))
        cnt = cnt + ahead.astype(jnp.int32)
    picked = jnp.where(cnt < SLC_TOPK, 1.0, 0.0)
    picked = jnp.concatenate([picked, jnp.zeros((HEAD_DIM - n_blk, tq), F32)], axis=0)
    return o, picked.T.astype(BF16)


def _online_update(s, v, m_ref, l_ref, acc_ref, rows=slice(None)):
    cols = [s[:, c:c + HEAD_DIM] for c in range(0, s.shape[1], HEAD_DIM)]
    m_el = functools.reduce(jnp.maximum, cols)
    m_prev = m_ref[rows]
    m_new = jnp.maximum(m_prev, jnp.max(m_el, axis=-1, keepdims=True))
    alpha = jnp.exp2(m_prev - m_new)
    ps = [jnp.exp2(c - m_new) for c in cols]
    l_ref[rows] = alpha * l_ref[rows] + functools.reduce(jnp.add, ps)
    p = jnp.concatenate([x.astype(BF16) for x in ps], axis=1)
    acc_ref[rows] = alpha * acc_ref[rows] + _dot(p, v)
    m_ref[rows] = m_new


def _normalize(acc_ref, l_ref):
    l = jnp.sum(l_ref[...], axis=-1, keepdims=True)
    return acc_ref[...] / jnp.maximum(l, 1e-30)


def _init_online(m_ref, l_ref, acc_ref):
    m_ref[...] = jnp.full(m_ref.shape, NEG_INF, F32)
    l_ref[...] = jnp.zeros(l_ref.shape, F32)
    acc_ref[...] = jnp.zeros(acc_ref.shape, F32)


def _flash_scratch(rows):
    return [pltpu.VMEM((rows, HEAD_DIM), F32)] * 3


def _win_branch(q, k_ref, v_ref, kh, i, tq, band):
    rows = q.shape[0]
    g = rows // tq
    start = pl.multiple_of(jnp.maximum(i * tq - WINDOW, 0), tq)
    k = k_ref[kh, 0, pl.ds(start, band), :]
    v = v_ref[kh, 0, pl.ds(start, band), :]
    row = lax.broadcasted_iota(jnp.int32, (tq, band), 0)
    col = lax.broadcasted_iota(jnp.int32, (tq, band), 1)
    kpos = start + col
    tpos = i * tq + row
    live = (kpos <= tpos) & (kpos > tpos - WINDOW)
    bias = jnp.where(live, 0.0, NEG_INF)
    s = (_dot_nt(q, k).reshape(g, tq, band) + bias[None]).reshape(rows, band)
    m = jnp.max(s, axis=-1, keepdims=True)
    p = jnp.exp2(s - m)
    l = jnp.sum(p, axis=-1, keepdims=True)
    return _dot(p.astype(BF16), v) / jnp.maximum(l, 1e-30)


def _nsa_body(q_ref, kc_ref, vc_ref, ks_ref, vs_ref, kw_ref, vw_ref, gt_ref, ov_ref, e_ref, o_ref,
              m_ref, l_ref, acc_ref, part_ref, sel_ref, *, tq, tk, band, n_blk):
    i = pl.program_id(1)
    g = GQA_GROUP
    rows = g * tq

    def queries(kh):
        return q_ref[kh * g:(kh + 1) * g].reshape(rows, HEAD_DIM)

    def gate(kh, branch):
        gt = gt_ref[kh, 0]
        return jnp.concatenate(
            [jnp.broadcast_to(gt[:, 3 * gg + branch:3 * gg + branch + 1], (tq, HEAD_DIM)) for gg in range(g)],
            axis=0)

    for kh in range(N_KV_NSA):
        rs = slice(kh * rows, (kh + 1) * rows)
        q = queries(kh)
        o_cmp, sel = _cmp_branch(q, kc_ref[0, 0, kh], vc_ref[0, 0, kh], ov_ref[...], i, tq, n_blk)
        sel_ref[kh] = sel
        part_ref[rs] = gate(kh, 0) * o_cmp + gate(kh, 2) * _win_branch(q, kw_ref, vw_ref, kh, i, tq, band)

    _init_online(m_ref, l_ref, acc_ref)
    row = lax.broadcasted_iota(jnp.int32, (tq, tk), 0)
    col = lax.broadcasted_iota(jnp.int32, (tq, tk), 1)
    last = (i * tq) // tk

    def step(kc, diagonal):
        start = pl.multiple_of(kc * tk, tk)
        for kh in range(N_KV_NSA):
            k = ks_ref[kh, 0, pl.ds(start, tk), :]
            v = vs_ref[kh, 0, pl.ds(start, tk), :]
            live = _dot(sel_ref[kh], e_ref[kc]) > 0.5
            if diagonal:
                live = live & ((start + col) <= (i * tq + row))
            bias = jnp.where(live, 0.0, NEG_INF)
            s = _dot_nt(queries(kh), k).reshape(g, tq, tk) + bias[None]
            _online_update(s.reshape(rows, tk), v, m_ref, l_ref, acc_ref, slice(kh * rows, (kh + 1) * rows))

    def body(kc, carry):
        step(kc, False)
        return carry

    lax.fori_loop(0, last, body, 0)
    step(last, True)
    o_slc = _normalize(acc_ref, l_ref)
    for kh in range(N_KV_NSA):
        rs = slice(kh * rows, (kh + 1) * rows)
        y = part_ref[rs] + gate(kh, 1) * o_slc[rs]
        for gg in range(g):
            hq = kh * g + gg
            o_ref[0, :, hq * HEAD_DIM:(hq + 1) * HEAD_DIM] = y[gg * tq:(gg + 1) * tq]


def _nsa(p4, kvc, sm4, ovt, e, tq=256):
    _, b, t, _ = p4.shape
    nr = kvc.shape[3]
    tk = e.shape[-1]
    band = min(WINDOW + tq, t)
    nk = N_KV_NSA
    rows = N_HEADS_NSA * tq

    def kv(slab):
        return pl.BlockSpec((nk, 1, t, HEAD_DIM), lambda bb, i: (slab // nk, bb, 0, 0))

    return pl.pallas_call(
        functools.partial(_nsa_body, tq=tq, tk=tk, band=band, n_blk=t // SLC_BLOCK),
        grid=(b, t // tq),
        in_specs=[
            pl.BlockSpec((N_HEADS_NSA, None, tq, HEAD_DIM), lambda bb, i: (0, bb, i, 0)),
            pl.BlockSpec((1, 1, nk, nr, HEAD_DIM), lambda bb, i: (0, bb, 0, 0, 0)),
            pl.BlockSpec((1, 1, nk, nr, HEAD_DIM), lambda bb, i: (1, bb, 0, 0, 0)),
            kv(SLAB_KV + 4), kv(SLAB_KV + 6), kv(SLAB_KV + 8), kv(SLAB_KV + 10),
            pl.BlockSpec((nk, 1, tq, HEAD_DIM), lambda bb, i: (0, bb, i, 0)),
            pl.BlockSpec((HEAD_DIM, HEAD_DIM), lambda bb, i: (0, 0)),
            pl.BlockSpec(e.shape, lambda bb, i: (0, 0, 0)),
        ],
        out_specs=pl.BlockSpec((1, tq, NSA_WIDTH), lambda bb, i: (bb, i, 0)),
        out_shape=jax.ShapeDtypeStruct((b, t, NSA_WIDTH), F32),
        scratch_shapes=_flash_scratch(rows) + [pltpu.VMEM((rows, HEAD_DIM), F32),
                                               pltpu.VMEM((nk, tq, HEAD_DIM), BF16)],
        compiler_params=_cparams(2),
        name="nsa",
    )(p4, kvc, kvc, p4, p4, p4, p4, sm4, ovt, e)


def _fox_body(q_ref, k_ref, v_ref, ct_ref, o_ref, m_ref, l_ref, acc_ref, *, tq):
    i = pl.program_id(2)
    nh = q_ref.shape[0]
    _init_online(m_ref, l_ref, acc_ref)
    row = lax.broadcasted_iota(jnp.int32, (tq, tq), 0)
    col = lax.broadcasted_iota(jnp.int32, (tq, tq), 1)

    def step(kc, diagonal):
        start = pl.multiple_of(kc * tq, tq)
        for hh in range(nh):
            k = k_ref[hh, 0, pl.ds(start, tq), :]
            v = v_ref[hh, 0, pl.ds(start, tq), :]
            s = _dot_nt(q_ref[hh], k) - ct_ref[hh, pl.ds(kc, 1), :]
            if diagonal:
                s = jnp.where(col <= row, s, NEG_INF)
            _online_update(s, v, m_ref, l_ref, acc_ref, slice(hh * tq, (hh + 1) * tq))

    def body(kc, carry):
        step(kc, False)
        return carry

    lax.fori_loop(0, i, body, 0)
    step(i, True)
    o = _normalize(acc_ref, l_ref)
    for hh in range(nh):
        o_ref[0, :, hh * HEAD_DIM:(hh + 1) * HEAD_DIM] = o[hh * tq:(hh + 1) * tq]


FOX_HEADS_PER_STEP = 4


def _fox_attn(p4, ctr, tq=512):
    _, b, t, _ = p4.shape
    nh = N_HEADS_FOX
    hs = FOX_HEADS_PER_STEP
    q0, k0, v0 = (SLAB_FOX // hs, (SLAB_FOX + nh) // hs, (SLAB_FOX + 2 * nh) // hs)
    return pl.pallas_call(
        functools.partial(_fox_body, tq=tq),
        grid=(b, nh // hs, t // tq),
        in_specs=[
            pl.BlockSpec((hs, None, tq, HEAD_DIM), lambda bb, h, i: (q0 + h, bb, i, 0)),
            pl.BlockSpec((hs, 1, t, HEAD_DIM), lambda bb, h, i: (k0 + h, bb, 0, 0)),
            pl.BlockSpec((hs, 1, t, HEAD_DIM), lambda bb, h, i: (v0 + h, bb, 0, 0)),
            pl.BlockSpec((hs, t // tq, tq), lambda bb, h, i: (bb * (nh // hs) + h, 0, 0)),
        ],
        out_specs=pl.BlockSpec((1, tq, hs * HEAD_DIM), lambda bb, h, i: (bb, i, h)),
        out_shape=jax.ShapeDtypeStruct((b, t, FOX_WIDTH), F32),
        scratch_shapes=_flash_scratch(hs * tq),
        compiler_params=_cparams(3),
        name="fox_attn",
    )(p4, p4, p4, ctr)


def _out_body(on_ref, of_ref, x_ref, nn_ref, fn_ref, w_ref, o_ref):
    nsa = (_row_rms(on_ref[...], NSA_WIDTH) * nn_ref[...]).astype(BF16)
    fox = (_row_rms(of_ref[...], FOX_WIDTH) * fn_ref[...]).astype(BF16)
    o_ref[...] = x_ref[...] + (_dot(nsa, w_ref[:NSA_WIDTH, :]) + _dot(fox, w_ref[NSA_WIDTH:, :]))


def _out(on, of, x, nn, fn, w, layer, tm=512):
    n, d = x.shape
    return pl.pallas_call(
        _out_body,
        grid=(n // tm,),
        in_specs=[
            pl.BlockSpec((tm, NSA_WIDTH), lambda i: (i, 0)),
            pl.BlockSpec((tm, FOX_WIDTH), lambda i: (i, 0)),
            pl.BlockSpec((tm, d), lambda i: (i, 0)),
            pl.BlockSpec((None, 1, NSA_WIDTH), lambda i: (layer, 0, 0)),
            pl.BlockSpec((None, 1, FOX_WIDTH), lambda i: (layer, 0, 0)),
            pl.BlockSpec((None, NSA_WIDTH + FOX_WIDTH, d), lambda i: (layer, 0, 0)),
        ],
        out_specs=pl.BlockSpec((tm, d), lambda i: (i, 0)),
        out_shape=jax.ShapeDtypeStruct((n, d), F32),
        compiler_params=_cparams(1),
        name="out_proj",
    )(on, of, x, nn, fn, w)


def _rope_tables(pos):
    inv = ROPE_THETA ** (-jnp.arange(0, ROPE_DIM, 2, dtype=F32) / ROPE_DIM)
    ang = pos.astype(F32)[:, None] * inv[None, :]
    cos, sin = jnp.cos(ang), jnp.sin(ang)
    n = pos.shape[0]
    pad = HEAD_DIM - ROPE_DIM
    cos_t = jnp.concatenate([cos, cos, jnp.ones((n, pad), F32)], axis=-1)
    s1 = jnp.concatenate([-sin, jnp.zeros((n, HEAD_DIM - ROPE_HALF), F32)], axis=-1)
    s2 = jnp.concatenate([jnp.zeros((n, ROPE_HALF), F32), sin, jnp.zeros((n, pad), F32)], axis=-1)
    return cos_t, s1, s2


def _overlap_matrix(t):
    n_cmp = (t - CMP_BLOCK) // CMP_STRIDE + 1
    n_slc = t // SLC_BLOCK
    cmp_start = np.arange(HEAD_DIM) * CMP_STRIDE
    slc_start = np.arange(HEAD_DIM) * SLC_BLOCK
    ov = ((cmp_start[:, None] < slc_start[None, :] + SLC_BLOCK)
          & (cmp_start[:, None] + CMP_BLOCK > slc_start[None, :]))
    ov = ov & (np.arange(HEAD_DIM)[:, None] < n_cmp) & (np.arange(HEAD_DIM)[None, :] < n_slc)
    return jnp.asarray(ov.T.astype(np.float32), BF16)


def _expand_matrix(t, tk):
    kpos = np.arange(t).reshape(t // tk, 1, tk)
    j = np.arange(HEAD_DIM).reshape(1, HEAD_DIM, 1)
    return jnp.asarray((kpos // SLC_BLOCK == j).astype(np.float32), BF16)


def _mixer(x2, b, t, layer, wts, tabs):
    n = x2.shape[0]
    p3, kc32, sm = _proj(x2, wts["mix_norm"], wts["w_main"], wts["w_small"], wts["gains"],
                         tabs["cos"], tabs["s1"], tabs["s2"], wts["fbias"], layer, t)
    p4 = p3.reshape(N_SLABS, b, t, HEAD_DIM)
    kr = kc32.reshape(HEADS_PER_TILE, b, t, HEAD_DIM)
    sm4 = sm.reshape(N_KV_NSA, b, t, HEAD_DIM)
    ct = _cumsum(sm4)
    fox_tq = min(512, t)
    ctr = ct.reshape(b * N_HEADS_FOX, t // fox_tq, fox_tq)
    kvc = _compress(kr, wts["pos"], wts["cmp_w1"], wts["cmp_w2"], wts["knorm0"],
                    tabs["cos_c"], tabs["s1_c"], tabs["s2_c"], layer)
    o_nsa = _nsa(p4, kvc, sm4, tabs["ov"], tabs["e"])
    o_fox = _fox_attn(p4, ctr, fox_tq)
    return _out(o_nsa.reshape(n, NSA_WIDTH), o_fox.reshape(n, FOX_WIDTH), x2, wts["nsa_out_norm"],
                wts["fox_out_norm"], wts["w_out"], layer)


def _prep_weights(ffn1_norm, ffn1_w_gate, ffn1_w_up, ffn1_w_down, mix_norm, w_in, nsa_q_norm, nsa_k_norm,
                  cmp_pos_emb, cmp_w1, cmp_w2, nsa_out_norm, fox_q_norm, fox_k_norm, fox_forget_bias,
                  fox_out_norm, w_out, ffn2_norm, ffn2_w_gate, ffn2_w_up, ffn2_w_down):
    depth, d, _ = w_in.shape
    c0 = COLS_NSA_Q + COLS_NSA_KV
    c1 = c0 + COLS_NSA_GATE
    c2 = c1 + COLS_FOX_QKV
    w_small = jnp.concatenate(
        [w_in[:, :, c0:c1], w_in[:, :, c2:], jnp.zeros((depth, d, HEAD_DIM - COLS_NSA_GATE - COLS_FOX_F), F32)],
        axis=-1).astype(BF16)
    w_main = jnp.concatenate([w_in[:, :, :c0], w_in[:, :, c1:c2]], axis=-1).astype(BF16)

    def tile(v):
        return jnp.concatenate([v] * HEADS_PER_TILE, axis=-1)

    one = jnp.ones((depth, HEAD_DIM), F32)
    gains = [tile(one)] * N_PROJ_TILES
    gains[0] = gains[1] = tile(nsa_q_norm * Q_SCALE)
    gains[3] = jnp.concatenate([nsa_k_norm[:, 1]] * 2 + [one] * 2, axis=-1)
    gains[4] = jnp.concatenate([nsa_k_norm[:, 2]] * 2 + [one] * 2, axis=-1)
    gains[5] = gains[6] = tile(fox_q_norm * Q_SCALE)
    gains[7] = gains[8] = tile(fox_k_norm)
    gains = jnp.stack(gains, axis=1)[:, :, None, :]

    fbias = jnp.zeros((depth, 1, HEAD_DIM), F32).at[:, 0, LANE_LOGF:LANE_LOGF + N_HEADS_FOX].set(
        fox_forget_bias)
    half = CMP_BLOCK * HEAD_DIM // 2
    pos = cmp_pos_emb.reshape(depth, 2, 2, 1, half)
    return dict(
        ffn1=(ffn1_norm[:, None, :], ffn1_w_gate.astype(BF16), ffn1_w_up.astype(BF16), ffn1_w_down.astype(BF16)),
        ffn2=(ffn2_norm[:, None, :], ffn2_w_gate.astype(BF16), ffn2_w_up.astype(BF16), ffn2_w_down.astype(BF16)),
        mix_norm=mix_norm[:, None, :], w_main=w_main, w_small=w_small, gains=gains, fbias=fbias, pos=pos,
        cmp_w1=cmp_w1.astype(BF16), cmp_w2=cmp_w2.astype(BF16), knorm0=nsa_k_norm[:, 0][:, None, :],
        nsa_out_norm=nsa_out_norm[:, None, :], fox_out_norm=fox_out_norm[:, None, :],
        w_out=w_out.astype(BF16),
    )


def _tables(t):
    cos, s1, s2 = _rope_tables(jnp.arange(t))
    cmp_end = jnp.arange(t // CMP_STRIDE) * CMP_STRIDE + (CMP_BLOCK - 1)
    cos_c, s1_c, s2_c = _rope_tables(cmp_end)
    return dict(cos=cos, s1=s1, s2=s2, cos_c=cos_c, s1_c=s1_c, s2_c=s2_c,
                ov=_overlap_matrix(t), e=_expand_matrix(t, min(512, t)))


def kernel(x, ffn1_norm, ffn1_w_gate, ffn1_w_up, ffn1_w_down, mix_norm, w_in, nsa_q_norm, nsa_k_norm, cmp_pos_emb, cmp_w1, cmp_w2, nsa_out_norm, fox_q_norm, fox_k_norm, fox_forget_bias, fox_out_norm, w_out, ffn2_norm, ffn2_w_gate, ffn2_w_up, ffn2_w_down):
    b, t, d = x.shape
    depth = w_in.shape[0]
    wts = _prep_weights(ffn1_norm, ffn1_w_gate, ffn1_w_up, ffn1_w_down, mix_norm, w_in, nsa_q_norm,
                        nsa_k_norm, cmp_pos_emb, cmp_w1, cmp_w2, nsa_out_norm, fox_q_norm, fox_k_norm,
                        fox_forget_bias, fox_out_norm, w_out, ffn2_norm, ffn2_w_gate, ffn2_w_up, ffn2_w_down)
    tabs = _tables(t)
    x2 = x.reshape(b * t, d)
    for layer in range(depth):
        x2 = _ffn(x2, *wts["ffn1"], layer)
        x2 = _mixer(x2, b, t, layer, wts, tabs)
        x2 = _ffn(x2, *wts["ffn2"], layer)
    return x2.reshape(b, t, d)
```

```python
import functools

import numpy as np
import jax
import jax.numpy as jnp
from jax import lax
from jax.experimental import pallas as pl
from jax.experimental.pallas import tpu as pltpu

F32 = jnp.float32
BF16 = jnp.bfloat16

HEAD_DIM = 128
N_HEADS_NSA = 8
N_KV_NSA = 2
GQA_GROUP = 4
N_HEADS_FOX = 8
NSA_WIDTH = N_HEADS_NSA * HEAD_DIM
FOX_WIDTH = N_HEADS_FOX * HEAD_DIM
ROPE_DIM = 32
ROPE_HALF = ROPE_DIM // 2
ROPE_THETA = 500000.0
CMP_BLOCK = 32
CMP_STRIDE = 16
CMP_HIDDEN = 256
SLC_BLOCK = 64
SLC_TOPK = 8
N_LOCAL_SLC = 2
WINDOW = 512
EPS = 1e-6
NEG_INF = -1e30
FORCED_SCORE = 1e9
SCALE = HEAD_DIM ** -0.5
LOG2E = float(np.log2(np.e))
Q_SCALE = SCALE * LOG2E

COLS_NSA_Q = NSA_WIDTH
COLS_NSA_KV = 3 * 2 * N_KV_NSA * HEAD_DIM
COLS_NSA_GATE = 3 * N_HEADS_NSA
COLS_FOX_QKV = 3 * FOX_WIDTH
COLS_FOX_F = N_HEADS_FOX
COLS_MAIN = COLS_NSA_Q + COLS_NSA_KV + COLS_FOX_QKV

HEADS_PER_TILE = 4
PROJ_TN = HEADS_PER_TILE * HEAD_DIM
N_PROJ_TILES = COLS_MAIN // PROJ_TN
N_SLABS = COLS_MAIN // HEAD_DIM
SLAB_Q = 0
SLAB_KV = 8
SLAB_FOX = 20
TILE_CMP_KV = 2
PROJ_ROW_CHUNKS = 4
FFN_ROW_CHUNKS = 4
LANE_LOGF = COLS_NSA_GATE

VMEM_LIMIT_BYTES = 56 * 1024 * 1024


def _cparams(n_axes):
    return pltpu.CompilerParams(dimension_semantics=("arbitrary",) * n_axes,
                                vmem_limit_bytes=VMEM_LIMIT_BYTES)


def _dot(a, b):
    return jnp.dot(a, b, preferred_element_type=F32)


def _dot_nt(a, b):
    return lax.dot_general(a, b, (((1,), (1,)), ((), ())), preferred_element_type=F32)


def _row_rms(x, width):
    return x * lax.rsqrt(jnp.sum(x * x, axis=-1, keepdims=True) * (1.0 / width) + EPS)


def _ffn_body(x_ref, g_ref, wg_ref, wu_ref, wd_ref, *rest, cast_steps):
    j = pl.program_id(1)
    if cast_steps:
        *srcs, o_ref, ng_ref, nu_ref, nd_ref, h_ref = rest

        @pl.when(pl.program_id(0) * pl.num_programs(1) + j < cast_steps)
        def _():
            for src, dst in zip(srcs, (ng_ref, nu_ref, nd_ref)):
                dst[...] = src[...].astype(dst.dtype)
    else:
        o_ref, h_ref = rest

    def step(first):
        chunks = FFN_ROW_CHUNKS if first else 1
        cm = o_ref.shape[0] // chunks
        for c in range(chunks):
            rs = slice(c * cm, (c + 1) * cm)
            if first:
                base = x_ref[rs, :]
                h = (_row_rms(base, base.shape[-1]) * g_ref[...]).astype(BF16)
                h_ref[rs, :] = h
            else:
                base = o_ref[rs, :]
                h = h_ref[rs, :]
            a = _dot(h, wg_ref[...])
            b = _dot(h, wu_ref[...])
            u = (0.5 * a * jax.nn.sigmoid(a)) * b
            o_ref[rs, :] = base + _dot(u.astype(BF16), wd_ref[...])

    @pl.when(j == 0)
    def _():
        step(True)

    @pl.when(j > 0)
    def _():
        step(False)


FFN_CAST_ROWS = 16
FFN_CAST_COLS = 2816


def _ffn(x, g, wg, wu, wd, layer, nxt=None, tm=1024, tf=512):
    n, d = x.shape
    f = wg.shape[-1]
    nj = f // tf
    in_specs = [
        pl.BlockSpec((tm, d), lambda i, j: (i, 0)),
        pl.BlockSpec((None, 1, d), lambda i, j: (layer, 0, 0)),
        pl.BlockSpec((d, tf), lambda i, j: (0, j)),
        pl.BlockSpec((d, tf), lambda i, j: (0, j)),
        pl.BlockSpec((tf, d), lambda i, j: (j, 0)),
    ]
    out_specs = [pl.BlockSpec((tm, d), lambda i, j: (i, 0))]
    out_shape = [jax.ShapeDtypeStruct((n, d), F32)]
    args = [x, g, wg, wu, wd]
    cast_steps = 0
    if nxt is not None:
        nl = nxt[3]
        rows = FFN_CAST_ROWS
        for src in nxt[:3]:
            r, c = src.shape[1:]
            ncol = pl.cdiv(c, FFN_CAST_COLS)
            cb = c // ncol
            steps = (r // rows) * ncol
            cast_steps = max(cast_steps, steps)
            assert (n // tm) * nj >= steps, "not enough grid steps to cast the next weights"

            def slab(i, j, steps=steps, ncol=ncol):
                s = jnp.minimum(i * nj + j, steps - 1)
                return s // ncol, s % ncol

            in_specs.append(pl.BlockSpec((None, rows, cb), lambda i, j, slab=slab: (nl,) + slab(i, j)))
            out_specs.append(pl.BlockSpec((rows, cb), slab))
            out_shape.append(jax.ShapeDtypeStruct((r, c), BF16))
            args.append(src)
    outs = pl.pallas_call(
        functools.partial(_ffn_body, cast_steps=cast_steps),
        grid=(n // tm, nj),
        in_specs=in_specs,
        out_specs=out_specs,
        out_shape=out_shape,
        scratch_shapes=[pltpu.VMEM((tm, d), BF16)],
        compiler_params=_cparams(2),
        name="ffn",
    )(*args)
    return outs[0], tuple(outs[1:])


def _rope(y, cos, s1, s2):
    return y * cos + pltpu.roll(y, HEAD_DIM - ROPE_HALF, 1) * s1 + pltpu.roll(y, ROPE_HALF, 1) * s2


def _proj_body(x_ref, g_ref, w_ref, ws_ref, gain_ref, cos_ref, s1_ref, s2_ref, bias_ref,
               p_ref, kc_ref, sm_ref, h_ref):
    j = pl.program_id(1)

    @pl.when(j == 0)
    def _():
        x = x_ref[...]
        h = (_row_rms(x, x.shape[-1]) * g_ref[...]).astype(BF16)
        h_ref[...] = h
        y = _dot(h, ws_ref[...])
        lane = lax.broadcasted_iota(jnp.int32, y.shape, 1)
        z = y + bias_ref[...]
        logsig = jnp.minimum(z, 0.0) - jnp.log1p(jnp.exp(-jnp.abs(z)))
        small = jnp.where(lane < LANE_LOGF, jax.nn.sigmoid(y), logsig)
        sm_ref[0] = small
        sm_ref[1] = pltpu.roll(small, HEAD_DIM - 3 * GQA_GROUP, 1)

    gain = gain_ref[...]
    tm = h_ref.shape[0]

    def raw(y, hh, rs):
        return y

    def normed(y, hh, rs):
        return _row_rms(y, HEAD_DIM) * gain[:, hh * HEAD_DIM:(hh + 1) * HEAD_DIM]

    def roped(y, hh, rs):
        return _rope(normed(y, hh, rs), cos_ref[rs, :], s1_ref[rs, :], s2_ref[rs, :])

    def emit(fns, keep_f32=False):
        chunks = 1 if all(fn is raw for fn in fns) else PROJ_ROW_CHUNKS
        cm = tm // chunks
        for c in range(chunks):
            rs = slice(c * cm, (c + 1) * cm)
            acc = _dot(h_ref[rs, :], w_ref[...])
            for hh, fn in enumerate(fns):
                y = acc[:, hh * HEAD_DIM:(hh + 1) * HEAD_DIM]
                p_ref[hh, rs, :] = fn(y, hh, rs).astype(p_ref.dtype)
                if keep_f32:
                    kc_ref[hh, rs, :] = y

    @pl.when(j < 2)
    def _():
        emit([roped] * 4)

    @pl.when(j == TILE_CMP_KV)
    def _():
        emit([raw] * 4, keep_f32=True)

    @pl.when((j == 3) | (j == 4))
    def _():
        emit([roped, roped, raw, raw])

    @pl.when((j >= 5) & (j < 9))
    def _():
        emit([normed] * 4)

    @pl.when(j >= 9)
    def _():
        emit([raw] * 4)


def _proj(x, g, w_main, w_small, gains, cos, s1, s2, bias, layer, seq, tm=1024):
    n, d = x.shape
    tm = min(tm, seq)
    tpb = seq // tm
    hp = HEADS_PER_TILE
    return pl.pallas_call(
        _proj_body,
        grid=(n // tm, N_PROJ_TILES),
        in_specs=[
            pl.BlockSpec((tm, d), lambda i, j: (i, 0)),
            pl.BlockSpec((None, 1, d), lambda i, j: (layer, 0, 0)),
            pl.BlockSpec((None, d, PROJ_TN), lambda i, j: (layer, 0, j)),
            pl.BlockSpec((None, d, HEAD_DIM), lambda i, j: (layer, 0, 0)),
            pl.BlockSpec((None, None, 1, PROJ_TN), lambda i, j: (layer, j, 0, 0)),
            pl.BlockSpec((tm, HEAD_DIM), lambda i, j: (i % tpb, 0)),
            pl.BlockSpec((tm, HEAD_DIM), lambda i, j: (i % tpb, 0)),
            pl.BlockSpec((tm, HEAD_DIM), lambda i, j: (i % tpb, 0)),
            pl.BlockSpec((None, 1, HEAD_DIM), lambda i, j: (layer, 0, 0)),
        ],
        out_specs=[
            pl.BlockSpec((hp, tm, HEAD_DIM), lambda i, j: (j, i, 0)),
            pl.BlockSpec((hp, tm, HEAD_DIM), lambda i, j: (0, i, 0)),
            pl.BlockSpec((N_KV_NSA, tm, HEAD_DIM), lambda i, j: (0, i, 0)),
        ],
        out_shape=[
            jax.ShapeDtypeStruct((N_SLABS, n, HEAD_DIM), BF16),
            jax.ShapeDtypeStruct((hp, n, HEAD_DIM), F32),
            jax.ShapeDtypeStruct((N_KV_NSA, n, HEAD_DIM), F32),
        ],
        scratch_shapes=[pltpu.VMEM((tm, d), BF16)],
        compiler_params=_cparams(2),
        name="proj",
    )(x, g, w_main, w_small, gains, cos, s1, s2, bias)


def _cumsum_body(x_ref, ct_ref):
    x = x_ref[0, 0]
    t = x.shape[0]
    row = lax.broadcasted_iota(jnp.int32, x.shape, 0)
    s = 1
    while s < t:
        x = x + jnp.where(row >= s, pltpu.roll(x, s, 0), 0.0)
        s *= 2
    ct_ref[0] = x.T[LANE_LOGF:LANE_LOGF + N_HEADS_FOX, :] * LOG2E


def _cumsum(sm4):
    _, b, t, _ = sm4.shape
    return pl.pallas_call(
        _cumsum_body,
        grid=(b,),
        in_specs=[pl.BlockSpec((1, 1, t, HEAD_DIM), lambda i: (0, i, 0, 0))],
        out_specs=pl.BlockSpec((1, N_HEADS_FOX, t), lambda i: (i, 0, 0)),
        out_shape=jax.ShapeDtypeStruct((b, N_HEADS_FOX, t), F32),
        compiler_params=_cparams(1),
        name="fox_cumsum",
    )(sm4)


def _gelu_tanh(x):
    c = float(np.sqrt(2.0 / np.pi))
    return x * (0.5 * (1.0 + jnp.tanh(c * (x + 0.044715 * (x * x * x)))))


def _compress_body(k_ref, pos_ref, w1_ref, w2_ref, kn_ref, cos_ref, s1_ref, s2_ref, o_ref):
    is_key = pl.program_id(0) == 0
    nr = o_ref.shape[-2]
    half = CMP_STRIDE * HEAD_DIM
    a = jnp.zeros((nr, CMP_HIDDEN), F32)
    bm = jnp.zeros((nr, CMP_HIDDEN), F32)
    for l in range(CMP_STRIDE):
        tok = k_ref[0, 0, pl.ds(l, nr, stride=CMP_STRIDE), :]
        sl = slice(l * HEAD_DIM, (l + 1) * HEAD_DIM)
        a = a + _dot((tok + pos_ref[0][:, sl]).astype(BF16), w1_ref[sl, :])
        bm = bm + _dot((tok + pos_ref[1][:, sl]).astype(BF16), w1_ref[half + l * HEAD_DIM:half + (l + 1) * HEAD_DIM, :])
    h = a + pltpu.roll(bm, bm.shape[0] - 1, 0)
    y = _dot(_gelu_tanh(h).astype(BF16), w2_ref[...])

    @pl.when(is_key)
    def _():
        yk = _row_rms(y, HEAD_DIM) * kn_ref[...]
        o_ref[0, 0, 0] = _rope(yk, cos_ref[...], s1_ref[...], s2_ref[...]).astype(o_ref.dtype)

    @pl.when(jnp.logical_not(is_key))
    def _():
        o_ref[0, 0, 0] = y.astype(o_ref.dtype)


def _compress(kr, pos, w1, w2, knorm0, cos_c, s1_c, s2_c, layer):
    _, b, t, _ = kr.shape
    nr = t // CMP_STRIDE
    width = CMP_STRIDE * HEAD_DIM
    return pl.pallas_call(
        _compress_body,
        grid=(2, b, N_KV_NSA),
        in_specs=[
            pl.BlockSpec((1, 1, t, HEAD_DIM), lambda ty, i, k: (ty * 2 + k, i, 0, 0)),
            pl.BlockSpec((None, None, 2, 1, width), lambda ty, i, k: (layer, ty, 0, 0, 0)),
            pl.BlockSpec((None, None, 2 * width, CMP_HIDDEN), lambda ty, i, k: (layer, ty, 0, 0)),
            pl.BlockSpec((None, None, CMP_HIDDEN, HEAD_DIM), lambda ty, i, k: (layer, ty, 0, 0)),
            pl.BlockSpec((None, 1, HEAD_DIM), lambda ty, i, k: (layer, 0, 0)),
            pl.BlockSpec((nr, HEAD_DIM), lambda ty, i, k: (0, 0)),
            pl.BlockSpec((nr, HEAD_DIM), lambda ty, i, k: (0, 0)),
            pl.BlockSpec((nr, HEAD_DIM), lambda ty, i, k: (0, 0)),
        ],
        out_specs=pl.BlockSpec((1, 1, 1, nr, HEAD_DIM), lambda ty, i, k: (ty, i, k, 0, 0)),
        out_shape=jax.ShapeDtypeStruct((2, b, N_KV_NSA, nr, HEAD_DIM), BF16),
        compiler_params=_cparams(3),
        name="compress",
    )(kr, pos, w1, w2, knorm0, cos_c, s1_c, s2_c)


def _split3(x):
    hi = x.astype(BF16)
    r1 = x - hi.astype(F32)
    mid = r1.astype(BF16)
    lo = (r1 - mid.astype(F32)).astype(BF16)
    return hi, mid, lo


def _cmp_branch(q, kc, vc, ovt, i, tq, n_blk):
    rows = q.shape[0]
    g = rows // tq
    s = _dot_nt(q, kc)
    row_g = lax.broadcasted_iota(jnp.int32, (rows, HEAD_DIM), 0)
    lane_g = lax.broadcasted_iota(jnp.int32, (rows, HEAD_DIM), 1)
    t_g = i * tq + (row_g & (tq - 1))
    valid = (lane_g * CMP_STRIDE + (CMP_BLOCK - 1)) <= t_g
    s = jnp.where(valid, s, NEG_INF)
    m = jnp.max(s, axis=-1, keepdims=True)
    p = jnp.where(valid, jnp.exp2(s - m), 0.0)
    p = p / jnp.maximum(jnp.sum(p, axis=-1, keepdims=True), 1e-30)
    o = _dot(p.astype(BF16), vc)
    psum = p[0:tq]
    for gg in range(1, g):
        psum = psum + p[gg * tq:(gg + 1) * tq]

    hi, mid, lo = _split3(psum)
    imp = ((_dot_nt(ovt, hi) + _dot_nt(ovt, mid)) + _dot_nt(ovt, lo))[:n_blk]
    blk = lax.broadcasted_iota(jnp.int32, (n_blk, tq), 0)
    t_blk = (i * tq + lax.broadcasted_iota(jnp.int32, (n_blk, tq), 1)) // SLC_BLOCK
    causal = blk <= t_blk
    forced = (blk == 0) | (causal & (blk > t_blk - N_LOCAL_SLC))
    score = jnp.where(forced, FORCED_SCORE, jnp.where(causal, imp, NEG_INF))
    cnt = jnp.zeros((n_blk, tq), jnp.int32)
    for c in range(n_blk):
        other = score[c:c + 1, :]
        ahead = (other > score) | ((other == score) & (blk > c))
        cnt = cnt + ahead.astype(jnp.int32)
    picked = jnp.where(cnt < SLC_TOPK, 1.0, 0.0)
    picked = jnp.concatenate([picked, jnp.zeros((HEAD_DIM - n_blk, tq), F32)], axis=0)
    return o, picked.T.astype(BF16)


def _online_update(s, v, m_ref, l_ref, acc_ref, rows=slice(None)):
    cols = [s[:, c:c + HEAD_DIM] for c in range(0, s.shape[1], HEAD_DIM)]
    m_el = functools.reduce(jnp.maximum, cols)
    m_prev = m_ref[rows]
    m_new = jnp.maximum(m_prev, jnp.max(m_el, axis=-1, keepdims=True))
    alpha = jnp.exp2(m_prev - m_new)
    ps = [jnp.exp2(c - m_new) for c in cols]
    l_ref[rows] = alpha * l_ref[rows] + functools.reduce(jnp.add, ps)
    p = jnp.concatenate([x.astype(BF16) for x in ps], axis=1)
    acc_ref[rows] = alpha * acc_ref[rows] + _dot(p, v)
    m_ref[rows] = m_new


def _normalize(acc_ref, l_ref):
    l = jnp.sum(l_ref[...], axis=-1, keepdims=True)
    return acc_ref[...] / jnp.maximum(l, 1e-30)


def _init_online(m_ref, l_ref, acc_ref):
    m_ref[...] = jnp.full(m_ref.shape, NEG_INF, F32)
    l_ref[...] = jnp.zeros(l_ref.shape, F32)
    acc_ref[...] = jnp.zeros(acc_ref.shape, F32)


def _flash_scratch(rows):
    return [pltpu.VMEM((rows, HEAD_DIM), F32)] * 3


def _win_branch(q, k_ref, v_ref, kh, i, tq, band):
    rows = q.shape[0]
    g = rows // tq
    start = pl.multiple_of(jnp.maximum(i * tq - WINDOW, 0), tq)
    k = k_ref[kh, 0, pl.ds(start, band), :]
    v = v_ref[kh, 0, pl.ds(start, band), :]
    row = lax.broadcasted_iota(jnp.int32, (tq, band), 0)
    col = lax.broadcasted_iota(jnp.int32, (tq, band), 1)
    kpos = start + col
    tpos = i * tq + row
    live = (kpos <= tpos) & (kpos > tpos - WINDOW)
    bias = jnp.where(live, 0.0, NEG_INF)
    s = (_dot_nt(q, k).reshape(g, tq, band) + bias[None]).reshape(rows, band)
    m = jnp.max(s, axis=-1, keepdims=True)
    p = jnp.exp2(s - m)
    l = jnp.sum(p, axis=-1, keepdims=True)
    return _dot(p.astype(BF16), v) / jnp.maximum(l, 1e-30)


def _nsa_body(q_ref, kc_ref, vc_ref, ks_ref, vs_ref, kw_ref, vw_ref, gt_ref, ov_ref, e_ref, o_ref,
              m_ref, l_ref, acc_ref, part_ref, sel_ref, *, tq, tk, band, n_blk):
    i = pl.program_id(1)
    g = GQA_GROUP
    rows = g * tq

    def queries(kh):
        return q_ref[kh * g:(kh + 1) * g].reshape(rows, HEAD_DIM)

    def gate(kh, branch):
        gt = gt_ref[kh, 0]
        return jnp.concatenate(
            [jnp.broadcast_to(gt[:, 3 * gg + branch:3 * gg + branch + 1], (tq, HEAD_DIM)) for gg in range(g)],
            axis=0)

    for kh in range(N_KV_NSA):
        rs = slice(kh * rows, (kh + 1) * rows)
        q = queries(kh)
        o_cmp, sel = _cmp_branch(q, kc_ref[0, 0, kh], vc_ref[0, 0, kh], ov_ref[...], i, tq, n_blk)
        sel_ref[kh] = sel
        part_ref[rs] = gate(kh, 0) * o_cmp + gate(kh, 2) * _win_branch(q, kw_ref, vw_ref, kh, i, tq, band)

    _init_online(m_ref, l_ref, acc_ref)
    row = lax.broadcasted_iota(jnp.int32, (tq, tk), 0)
    col = lax.broadcasted_iota(jnp.int32, (tq, tk), 1)
    last = (i * tq) // tk

    def step(kc, diagonal):
        start = pl.multiple_of(kc * tk, tk)
        for kh in range(N_KV_NSA):
            k = ks_ref[kh, 0, pl.ds(start, tk), :]
            v = vs_ref[kh, 0, pl.ds(start, tk), :]
            live = _dot(sel_ref[kh], e_ref[kc]) > 0.5
            if diagonal:
                live = live & ((start + col) <= (i * tq + row))
            bias = jnp.where(live, 0.0, NEG_INF)
            s = _dot_nt(queries(kh), k).reshape(g, tq, tk) + bias[None]
            _online_update(s.reshape(rows, tk), v, m_ref, l_ref, acc_ref, slice(kh * rows, (kh + 1) * rows))

    def body(kc, carry):
        step(kc, False)
        return carry

    lax.fori_loop(0, last, body, 0)
    step(last, True)
    o_slc = _normalize(acc_ref, l_ref)
    for kh in range(N_KV_NSA):
        rs = slice(kh * rows, (kh + 1) * rows)
        y = part_ref[rs] + gate(kh, 1) * o_slc[rs]
        for gg in range(g):
            hq = kh * g + gg
            o_ref[0, :, hq * HEAD_DIM:(hq + 1) * HEAD_DIM] = y[gg * tq:(gg + 1) * tq]


def _nsa(p4, kvc, sm4, ovt, e, tq=256):
    _, b, t, _ = p4.shape
    nr = kvc.shape[3]
    tk = e.shape[-1]
    band = min(WINDOW + tq, t)
    nk = N_KV_NSA
    rows = N_HEADS_NSA * tq

    def kv(slab):
        return pl.BlockSpec((nk, 1, t, HEAD_DIM), lambda bb, i: (slab // nk, bb, 0, 0))

    return pl.pallas_call(
        functools.partial(_nsa_body, tq=tq, tk=tk, band=band, n_blk=t // SLC_BLOCK),
        grid=(b, t // tq),
        in_specs=[
            pl.BlockSpec((N_HEADS_NSA, None, tq, HEAD_DIM), lambda bb, i: (0, bb, i, 0)),
            pl.BlockSpec((1, 1, nk, nr, HEAD_DIM), lambda bb, i: (0, bb, 0, 0, 0)),
            pl.BlockSpec((1, 1, nk, nr, HEAD_DIM), lambda bb, i: (1, bb, 0, 0, 0)),
            kv(SLAB_KV + 4), kv(SLAB_KV + 6), kv(SLAB_KV + 8), kv(SLAB_KV + 10),
            pl.BlockSpec((nk, 1, tq, HEAD_DIM), lambda bb, i: (0, bb, i, 0)),
            pl.BlockSpec((HEAD_DIM, HEAD_DIM), lambda bb, i: (0, 0)),
            pl.BlockSpec(e.shape, lambda bb, i: (0, 0, 0)),
        ],
        out_specs=pl.BlockSpec((1, tq, NSA_WIDTH), lambda bb, i: (bb, i, 0)),
        out_shape=jax.ShapeDtypeStruct((b, t, NSA_WIDTH), F32),
        scratch_shapes=_flash_scratch(rows) + [pltpu.VMEM((rows, HEAD_DIM), F32),
                                               pltpu.VMEM((nk, tq, HEAD_DIM), BF16)],
        compiler_params=_cparams(2),
        name="nsa",
    )(p4, kvc, kvc, p4, p4, p4, p4, sm4, ovt, e)


def _fox_body(q_ref, k_ref, v_ref, ct_ref, o_ref, m_ref, l_ref, acc_ref, *, tq):
    i = pl.program_id(2)
    nh = q_ref.shape[0]
    _init_online(m_ref, l_ref, acc_ref)
    row = lax.broadcasted_iota(jnp.int32, (tq, tq), 0)
    col = lax.broadcasted_iota(jnp.int32, (tq, tq), 1)

    def step(kc, diagonal):
        start = pl.multiple_of(kc * tq, tq)
        for hh in range(nh):
            k = k_ref[hh, 0, pl.ds(start, tq), :]
            v = v_ref[hh, 0, pl.ds(start, tq), :]
            s = _dot_nt(q_ref[hh], k) - ct_ref[hh, pl.ds(kc, 1), :]
            if diagonal:
                s = jnp.where(col <= row, s, NEG_INF)
            _online_update(s, v, m_ref, l_ref, acc_ref, slice(hh * tq, (hh + 1) * tq))

    def body(kc, carry):
        step(kc, False)
        return carry

    lax.fori_loop(0, i, body, 0)
    step(i, True)
    o = _normalize(acc_ref, l_ref)
    for hh in range(nh):
        o_ref[0, :, hh * HEAD_DIM:(hh + 1) * HEAD_DIM] = o[hh * tq:(hh + 1) * tq]


FOX_HEADS_PER_STEP = 4
assert SLAB_FOX % FOX_HEADS_PER_STEP == 0 and N_HEADS_FOX % FOX_HEADS_PER_STEP == 0


def _fox_attn(p4, ctr, tq=512):
    _, b, t, _ = p4.shape
    nh = N_HEADS_FOX
    hs = FOX_HEADS_PER_STEP
    q0, k0, v0 = (SLAB_FOX // hs, (SLAB_FOX + nh) // hs, (SLAB_FOX + 2 * nh) // hs)
    return pl.pallas_call(
        functools.partial(_fox_body, tq=tq),
        grid=(b, nh // hs, t // tq),
        in_specs=[
            pl.BlockSpec((hs, None, tq, HEAD_DIM), lambda bb, h, i: (q0 + h, bb, i, 0)),
            pl.BlockSpec((hs, 1, t, HEAD_DIM), lambda bb, h, i: (k0 + h, bb, 0, 0)),
            pl.BlockSpec((hs, 1, t, HEAD_DIM), lambda bb, h, i: (v0 + h, bb, 0, 0)),
            pl.BlockSpec((hs, t // tq, tq), lambda bb, h, i: (bb * (nh // hs) + h, 0, 0)),
        ],
        out_specs=pl.BlockSpec((1, tq, hs * HEAD_DIM), lambda bb, h, i: (bb, i, h)),
        out_shape=jax.ShapeDtypeStruct((b, t, FOX_WIDTH), F32),
        scratch_shapes=_flash_scratch(hs * tq),
        compiler_params=_cparams(3),
        name="fox_attn",
    )(p4, p4, p4, ctr)


def _out_body(on_ref, of_ref, x_ref, nn_ref, fn_ref, w_ref, o_ref):
    nsa = (_row_rms(on_ref[...], NSA_WIDTH) * nn_ref[...]).astype(BF16)
    fox = (_row_rms(of_ref[...], FOX_WIDTH) * fn_ref[...]).astype(BF16)
    o_ref[...] = x_ref[...] + (_dot(nsa, w_ref[:NSA_WIDTH, :]) + _dot(fox, w_ref[NSA_WIDTH:, :]))


def _out(on, of, x, nn, fn, w, layer, tm=512):
    n, d = x.shape
    return pl.pallas_call(
        _out_body,
        grid=(n // tm,),
        in_specs=[
            pl.BlockSpec((tm, NSA_WIDTH), lambda i: (i, 0)),
            pl.BlockSpec((tm, FOX_WIDTH), lambda i: (i, 0)),
            pl.BlockSpec((tm, d), lambda i: (i, 0)),
            pl.BlockSpec((None, 1, NSA_WIDTH), lambda i: (layer, 0, 0)),
            pl.BlockSpec((None, 1, FOX_WIDTH), lambda i: (layer, 0, 0)),
            pl.BlockSpec((None, NSA_WIDTH + FOX_WIDTH, d), lambda i: (layer, 0, 0)),
        ],
        out_specs=pl.BlockSpec((tm, d), lambda i: (i, 0)),
        out_shape=jax.ShapeDtypeStruct((n, d), F32),
        compiler_params=_cparams(1),
        name="out_proj",
    )(on, of, x, nn, fn, w)


def _rope_tables(pos):
    inv = ROPE_THETA ** (-jnp.arange(0, ROPE_DIM, 2, dtype=F32) / ROPE_DIM)
    ang = pos.astype(F32)[:, None] * inv[None, :]
    cos, sin = jnp.cos(ang), jnp.sin(ang)
    n = pos.shape[0]
    pad = HEAD_DIM - ROPE_DIM
    cos_t = jnp.concatenate([cos, cos, jnp.ones((n, pad), F32)], axis=-1)
    s1 = jnp.concatenate([-sin, jnp.zeros((n, HEAD_DIM - ROPE_HALF), F32)], axis=-1)
    s2 = jnp.concatenate([jnp.zeros((n, ROPE_HALF), F32), sin, jnp.zeros((n, pad), F32)], axis=-1)
    return cos_t, s1, s2


def _overlap_matrix(t):
    n_cmp = (t - CMP_BLOCK) // CMP_STRIDE + 1
    n_slc = t // SLC_BLOCK
    cmp_start = np.arange(HEAD_DIM) * CMP_STRIDE
    slc_start = np.arange(HEAD_DIM) * SLC_BLOCK
    ov = ((cmp_start[:, None] < slc_start[None, :] + SLC_BLOCK)
          & (cmp_start[:, None] + CMP_BLOCK > slc_start[None, :]))
    ov = ov & (np.arange(HEAD_DIM)[:, None] < n_cmp) & (np.arange(HEAD_DIM)[None, :] < n_slc)
    return jnp.asarray(ov.T.astype(np.float32), BF16)


def _expand_matrix(t, tk):
    kpos = np.arange(t).reshape(t // tk, 1, tk)
    j = np.arange(HEAD_DIM).reshape(1, HEAD_DIM, 1)
    return jnp.asarray((kpos // SLC_BLOCK == j).astype(np.float32), BF16)


def _mixer(x2, b, t, layer, wts, tabs):
    n = x2.shape[0]
    p3, kc32, sm = _proj(x2, wts["mix_norm"], wts["w_main"], wts["w_small"], wts["gains"],
                         tabs["cos"], tabs["s1"], tabs["s2"], wts["fbias"], layer, t)
    p4 = p3.reshape(N_SLABS, b, t, HEAD_DIM)
    kr = kc32.reshape(HEADS_PER_TILE, b, t, HEAD_DIM)
    sm4 = sm.reshape(N_KV_NSA, b, t, HEAD_DIM)
    ct = _cumsum(sm4)
    fox_tq = min(512, t)
    ctr = ct.reshape(b * N_HEADS_FOX, t // fox_tq, fox_tq)
    kvc = _compress(kr, wts["pos"], wts["cmp_w1"], wts["cmp_w2"], wts["knorm0"],
                    tabs["cos_c"], tabs["s1_c"], tabs["s2_c"], layer)
    o_nsa = _nsa(p4, kvc, sm4, tabs["ov"], tabs["e"])
    o_fox = _fox_attn(p4, ctr, fox_tq)
    return _out(o_nsa.reshape(n, NSA_WIDTH), o_fox.reshape(n, FOX_WIDTH), x2, wts["nsa_out_norm"],
                wts["fox_out_norm"], wts["w_out"], layer)


def _prep_weights(ffn1_norm, ffn1_w_gate, ffn1_w_up, ffn1_w_down, mix_norm, w_in, nsa_q_norm, nsa_k_norm,
                  cmp_pos_emb, cmp_w1, cmp_w2, nsa_out_norm, fox_q_norm, fox_k_norm, fox_forget_bias,
                  fox_out_norm, w_out, ffn2_norm, ffn2_w_gate, ffn2_w_up, ffn2_w_down):
    depth, d, _ = w_in.shape
    c0 = COLS_NSA_Q + COLS_NSA_KV
    c1 = c0 + COLS_NSA_GATE
    c2 = c1 + COLS_FOX_QKV
    w_small = jnp.concatenate(
        [w_in[:, :, c0:c1], w_in[:, :, c2:], jnp.zeros((depth, d, HEAD_DIM - COLS_NSA_GATE - COLS_FOX_F), F32)],
        axis=-1).astype(BF16)
    w_main = jnp.concatenate([w_in[:, :, :c0], w_in[:, :, c1:c2]], axis=-1).astype(BF16)

    def tile(v):
        return jnp.concatenate([v] * HEADS_PER_TILE, axis=-1)

    one = jnp.ones((depth, HEAD_DIM), F32)
    gains = [tile(one)] * N_PROJ_TILES
    gains[0] = gains[1] = tile(nsa_q_norm * Q_SCALE)
    gains[3] = jnp.concatenate([nsa_k_norm[:, 1]] * 2 + [one] * 2, axis=-1)
    gains[4] = jnp.concatenate([nsa_k_norm[:, 2]] * 2 + [one] * 2, axis=-1)
    gains[5] = gains[6] = tile(fox_q_norm * Q_SCALE)
    gains[7] = gains[8] = tile(fox_k_norm)
    gains = jnp.stack(gains, axis=1)[:, :, None, :]

    fbias = jnp.zeros((depth, 1, HEAD_DIM), F32).at[:, 0, LANE_LOGF:LANE_LOGF + N_HEADS_FOX].set(
        fox_forget_bias)
    half = CMP_BLOCK * HEAD_DIM // 2
    pos = cmp_pos_emb.reshape(depth, 2, 2, 1, half)
    return dict(
        ffn1_norm=ffn1_norm[:, None, :], ffn2_norm=ffn2_norm[:, None, :],
        ffn1=(ffn1_w_gate, ffn1_w_up, ffn1_w_down), ffn2=(ffn2_w_gate, ffn2_w_up, ffn2_w_down),
        mix_norm=mix_norm[:, None, :], w_main=w_main, w_small=w_small, gains=gains, fbias=fbias, pos=pos,
        cmp_w1=cmp_w1.astype(BF16), cmp_w2=cmp_w2.astype(BF16), knorm0=nsa_k_norm[:, 0][:, None, :],
        nsa_out_norm=nsa_out_norm[:, None, :], fox_out_norm=fox_out_norm[:, None, :],
        w_out=w_out.astype(BF16),
    )


def _tables(t):
    cos, s1, s2 = _rope_tables(jnp.arange(t))
    cmp_end = jnp.arange(t // CMP_STRIDE) * CMP_STRIDE + (CMP_BLOCK - 1)
    cos_c, s1_c, s2_c = _rope_tables(cmp_end)
    return dict(cos=cos, s1=s1, s2=s2, cos_c=cos_c, s1_c=s1_c, s2_c=s2_c,
                ov=_overlap_matrix(t), e=_expand_matrix(t, min(512, t)))


def kernel(x, ffn1_norm, ffn1_w_gate, ffn1_w_up, ffn1_w_down, mix_norm, w_in, nsa_q_norm, nsa_k_norm, cmp_pos_emb, cmp_w1, cmp_w2, nsa_out_norm, fox_q_norm, fox_k_norm, fox_forget_bias, fox_out_norm, w_out, ffn2_norm, ffn2_w_gate, ffn2_w_up, ffn2_w_down):
    b, t, d = x.shape
    depth = w_in.shape[0]
    wts = _prep_weights(ffn1_norm, ffn1_w_gate, ffn1_w_up, ffn1_w_down, mix_norm, w_in, nsa_q_norm,
                        nsa_k_norm, cmp_pos_emb, cmp_w1, cmp_w2, nsa_out_norm, fox_q_norm, fox_k_norm,
                        fox_forget_bias, fox_out_norm, w_out, ffn2_norm, ffn2_w_gate, ffn2_w_up, ffn2_w_down)
    tabs = _tables(t)
    x2 = x.reshape(b * t, d)
    w16 = tuple(w[0].astype(BF16) for w in wts["ffn1"])
    for layer in range(depth):
        x2, w16 = _ffn(x2, wts["ffn1_norm"], *w16, layer, nxt=wts["ffn2"] + (layer,))
        x2 = _mixer(x2, b, t, layer, wts, tabs)
        nxt = wts["ffn1"] + (layer + 1,) if layer + 1 < depth else None
        x2, w16 = _ffn(x2, wts["ffn2_norm"], *w16, layer, nxt=nxt)
    return x2.reshape(b, t, d)
```

```python
import functools

import numpy as np
import jax
import jax.numpy as jnp
from jax import lax
from jax.experimental import pallas as pl
from jax.experimental.pallas import tpu as pltpu

F32 = jnp.float32
BF16 = jnp.bfloat16

HEAD_DIM = 128
N_HEADS_NSA = 8
N_KV_NSA = 2
GQA_GROUP = 4
N_HEADS_FOX = 8
NSA_WIDTH = N_HEADS_NSA * HEAD_DIM
FOX_WIDTH = N_HEADS_FOX * HEAD_DIM
ROPE_DIM = 32
ROPE_HALF = ROPE_DIM // 2
ROPE_THETA = 500000.0
CMP_BLOCK = 32
CMP_STRIDE = 16
CMP_HIDDEN = 256
SLC_BLOCK = 64
SLC_TOPK = 8
N_LOCAL_SLC = 2
WINDOW = 512
EPS = 1e-6
NEG_INF = -1e30
FORCED_SCORE = 1e9
SCALE = HEAD_DIM ** -0.5
LOG2E = float(np.log2(np.e))
Q_SCALE = SCALE * LOG2E

COLS_NSA_Q = NSA_WIDTH
COLS_NSA_KV = 3 * 2 * N_KV_NSA * HEAD_DIM
COLS_NSA_GATE = 3 * N_HEADS_NSA
COLS_FOX_QKV = 3 * FOX_WIDTH
COLS_FOX_F = N_HEADS_FOX
COLS_MAIN = COLS_NSA_Q + COLS_NSA_KV + COLS_FOX_QKV

HEADS_PER_TILE = 4
PROJ_TN = HEADS_PER_TILE * HEAD_DIM
N_PROJ_TILES = COLS_MAIN // PROJ_TN
N_SLABS = COLS_MAIN // HEAD_DIM
SLAB_FOX = 0
SLAB_Q = 24
SLAB_KV = 32
TILE_Q = SLAB_Q // HEADS_PER_TILE
TILE_CMP_KV = SLAB_KV // HEADS_PER_TILE
PROJ_ROW_CHUNKS = 4
FFN_ROW_CHUNKS = 4
LANE_LOGF = COLS_NSA_GATE

VMEM_LIMIT_BYTES = 56 * 1024 * 1024


def _cparams(n_axes):
    return pltpu.CompilerParams(dimension_semantics=("arbitrary",) * n_axes,
                                vmem_limit_bytes=VMEM_LIMIT_BYTES)


def _dot(a, b):
    return jnp.dot(a, b, preferred_element_type=F32)


def _dot_nt(a, b):
    return lax.dot_general(a, b, (((1,), (1,)), ((), ())), preferred_element_type=F32)


def _row_rms(x, width):
    return x * lax.rsqrt(jnp.sum(x * x, axis=-1, keepdims=True) * (1.0 / width) + EPS)


def _ffn_body(x_ref, g_ref, wg_ref, wu_ref, wd_ref, *rest, cast_steps):
    j = pl.program_id(1)
    if cast_steps:
        *srcs, o_ref, ng_ref, nu_ref, nd_ref, h_ref = rest

        @pl.when(pl.program_id(0) * pl.num_programs(1) + j < cast_steps)
        def _():
            for src, dst in zip(srcs, (ng_ref, nu_ref, nd_ref)):
                dst[...] = src[...].astype(dst.dtype)
    else:
        o_ref, h_ref = rest

    def step(first):
        chunks = FFN_ROW_CHUNKS if first else 1
        cm = o_ref.shape[0] // chunks
        for c in range(chunks):
            rs = slice(c * cm, (c + 1) * cm)
            if first:
                base = x_ref[rs, :]
                h = (_row_rms(base, base.shape[-1]) * g_ref[...]).astype(BF16)
                h_ref[rs, :] = h
            else:
                base = o_ref[rs, :]
                h = h_ref[rs, :]
            a = _dot(h, wg_ref[...])
            b = _dot(h, wu_ref[...])
            u = (0.5 * a * jax.nn.sigmoid(a)) * b
            o_ref[rs, :] = base + _dot(u.astype(BF16), wd_ref[...])

    @pl.when(j == 0)
    def _():
        step(True)

    @pl.when(j > 0)
    def _():
        step(False)


FFN_CAST_ROWS = 16
FFN_CAST_COLS = 2816


def _ffn(x, g, wg, wu, wd, layer, nxt=None, tm=1024, tf=512):
    n, d = x.shape
    f = wg.shape[-1]
    nj = f // tf
    in_specs = [
        pl.BlockSpec((tm, d), lambda i, j: (i, 0)),
        pl.BlockSpec((None, 1, d), lambda i, j: (layer, 0, 0)),
        pl.BlockSpec((d, tf), lambda i, j: (0, j)),
        pl.BlockSpec((d, tf), lambda i, j: (0, j)),
        pl.BlockSpec((tf, d), lambda i, j: (j, 0)),
    ]
    out_specs = [pl.BlockSpec((tm, d), lambda i, j: (i, 0))]
    out_shape = [jax.ShapeDtypeStruct((n, d), F32)]
    args = [x, g, wg, wu, wd]
    cast_steps = 0
    if nxt is not None:
        nl = nxt[3]
        rows = FFN_CAST_ROWS
        for src in nxt[:3]:
            r, c = src.shape[1:]
            ncol = pl.cdiv(c, FFN_CAST_COLS)
            cb = c // ncol
            steps = (r // rows) * ncol
            cast_steps = max(cast_steps, steps)
            assert (n // tm) * nj >= steps, "not enough grid steps to cast the next weights"

            def slab(i, j, steps=steps, ncol=ncol):
                s = jnp.minimum(i * nj + j, steps - 1)
                return s // ncol, s % ncol

            in_specs.append(pl.BlockSpec((None, rows, cb), lambda i, j, slab=slab: (nl,) + slab(i, j)))
            out_specs.append(pl.BlockSpec((rows, cb), slab))
            out_shape.append(jax.ShapeDtypeStruct((r, c), BF16))
            args.append(src)
    outs = pl.pallas_call(
        functools.partial(_ffn_body, cast_steps=cast_steps),
        grid=(n // tm, nj),
        in_specs=in_specs,
        out_specs=out_specs,
        out_shape=out_shape,
        scratch_shapes=[pltpu.VMEM((tm, d), BF16)],
        compiler_params=_cparams(2),
        name="ffn",
    )(*args)
    return outs[0], tuple(outs[1:])


def _rope(y, cos, s1, s2):
    return y * cos + pltpu.roll(y, HEAD_DIM - ROPE_HALF, 1) * s1 + pltpu.roll(y, ROPE_HALF, 1) * s2


def _proj_body(x_ref, g_ref, w_ref, ws_ref, gain_ref, cos_ref, s1_ref, s2_ref, bias_ref,
               p_ref, kc_ref, sm_ref, h_ref):
    j = pl.program_id(1)
    gain = gain_ref[...]
    tm = h_ref.shape[0]

    def normalize_rows(rs):
        x = x_ref[rs, :]
        h = (_row_rms(x, x.shape[-1]) * g_ref[...]).astype(BF16)
        h_ref[rs, :] = h
        y = _dot(h, ws_ref[...])
        lane = lax.broadcasted_iota(jnp.int32, y.shape, 1)
        z = y + bias_ref[...]
        logsig = jnp.minimum(z, 0.0) - jnp.log1p(jnp.exp(-jnp.abs(z)))
        small = jnp.where(lane < LANE_LOGF, jax.nn.sigmoid(y), logsig)
        sm_ref[0, rs, :] = small
        sm_ref[1, rs, :] = pltpu.roll(small, HEAD_DIM - 3 * GQA_GROUP, 1)
        return h

    def raw(y, hh, rs):
        return y

    def normed(y, hh, rs):
        return _row_rms(y, HEAD_DIM) * gain[:, hh * HEAD_DIM:(hh + 1) * HEAD_DIM]

    def roped(y, hh, rs):
        return _rope(normed(y, hh, rs), cos_ref[rs, :], s1_ref[rs, :], s2_ref[rs, :])

    def emit(fns, keep_f32=False, first=False):
        chunks = 1 if all(fn is raw for fn in fns) and not first else PROJ_ROW_CHUNKS
        cm = tm // chunks
        for c in range(chunks):
            rs = slice(c * cm, (c + 1) * cm)
            h = normalize_rows(rs) if first else h_ref[rs, :]
            acc = _dot(h, w_ref[...])
            for hh, fn in enumerate(fns):
                y = acc[:, hh * HEAD_DIM:(hh + 1) * HEAD_DIM]
                p_ref[hh, rs, :] = fn(y, hh, rs).astype(p_ref.dtype)
                if keep_f32:
                    kc_ref[hh, rs, :] = y

    @pl.when(j == 0)
    def _():
        emit([normed] * 4, first=True)

    @pl.when((j >= 1) & (j < 4))
    def _():
        emit([normed] * 4)

    @pl.when((j >= 4) & (j < TILE_Q))
    def _():
        emit([raw] * 4)

    @pl.when((j >= TILE_Q) & (j < TILE_CMP_KV))
    def _():
        emit([roped] * 4)

    @pl.when(j == TILE_CMP_KV)
    def _():
        emit([raw] * 4, keep_f32=True)

    @pl.when(j > TILE_CMP_KV)
    def _():
        emit([roped, roped, raw, raw])


def _proj(x, g, w_main, w_small, gains, cos, s1, s2, bias, layer, seq, tm=1024):
    n, d = x.shape
    tm = min(tm, seq)
    tpb = seq // tm
    hp = HEADS_PER_TILE
    return pl.pallas_call(
        _proj_body,
        grid=(n // tm, N_PROJ_TILES),
        in_specs=[
            pl.BlockSpec((tm, d), lambda i, j: (i, 0)),
            pl.BlockSpec((None, 1, d), lambda i, j: (layer, 0, 0)),
            pl.BlockSpec((None, d, PROJ_TN), lambda i, j: (layer, 0, j)),
            pl.BlockSpec((None, d, HEAD_DIM), lambda i, j: (layer, 0, 0)),
            pl.BlockSpec((None, None, 1, PROJ_TN), lambda i, j: (layer, j, 0, 0)),
            pl.BlockSpec((tm, HEAD_DIM), lambda i, j: (i % tpb, 0)),
            pl.BlockSpec((tm, HEAD_DIM), lambda i, j: (i % tpb, 0)),
            pl.BlockSpec((tm, HEAD_DIM), lambda i, j: (i % tpb, 0)),
            pl.BlockSpec((None, 1, HEAD_DIM), lambda i, j: (layer, 0, 0)),
        ],
        out_specs=[
            pl.BlockSpec((hp, tm, HEAD_DIM), lambda i, j: (j, i, 0)),
            pl.BlockSpec((hp, tm, HEAD_DIM), lambda i, j: (0, i, 0)),
            pl.BlockSpec((N_KV_NSA, tm, HEAD_DIM), lambda i, j: (0, i, 0)),
        ],
        out_shape=[
            jax.ShapeDtypeStruct((N_SLABS, n, HEAD_DIM), BF16),
            jax.ShapeDtypeStruct((hp, n, HEAD_DIM), F32),
            jax.ShapeDtypeStruct((N_KV_NSA, n, HEAD_DIM), F32),
        ],
        scratch_shapes=[pltpu.VMEM((tm, d), BF16)],
        compiler_params=_cparams(2),
        name="proj",
    )(x, g, w_main, w_small, gains, cos, s1, s2, bias)


def _cumsum_body(x_ref, ct_ref):
    x = x_ref[0, 0]
    t = x.shape[0]
    row = lax.broadcasted_iota(jnp.int32, x.shape, 0)
    s = 1
    while s < t:
        x = x + jnp.where(row >= s, pltpu.roll(x, s, 0), 0.0)
        s *= 2
    ct_ref[0] = x.T[LANE_LOGF:LANE_LOGF + N_HEADS_FOX, :] * LOG2E


def _cumsum(sm4):
    _, b, t, _ = sm4.shape
    return pl.pallas_call(
        _cumsum_body,
        grid=(b,),
        in_specs=[pl.BlockSpec((1, 1, t, HEAD_DIM), lambda i: (0, i, 0, 0))],
        out_specs=pl.BlockSpec((1, N_HEADS_FOX, t), lambda i: (i, 0, 0)),
        out_shape=jax.ShapeDtypeStruct((b, N_HEADS_FOX, t), F32),
        compiler_params=_cparams(1),
        name="fox_cumsum",
    )(sm4)


def _gelu_tanh(x):
    c = float(np.sqrt(2.0 / np.pi))
    return x * (0.5 * (1.0 + jnp.tanh(c * (x + 0.044715 * (x * x * x)))))


def _compress_body(k_ref, pos_ref, w1_ref, w2_ref, kn_ref, cos_ref, s1_ref, s2_ref, o_ref):
    is_key = pl.program_id(0) == 0
    nr = o_ref.shape[-2]
    half = CMP_STRIDE * HEAD_DIM
    a = jnp.zeros((nr, CMP_HIDDEN), F32)
    bm = jnp.zeros((nr, CMP_HIDDEN), F32)
    for l in range(CMP_STRIDE):
        tok = k_ref[0, 0, pl.ds(l, nr, stride=CMP_STRIDE), :]
        sl = slice(l * HEAD_DIM, (l + 1) * HEAD_DIM)
        a = a + _dot((tok + pos_ref[0][:, sl]).astype(BF16), w1_ref[sl, :])
        bm = bm + _dot((tok + pos_ref[1][:, sl]).astype(BF16), w1_ref[half + l * HEAD_DIM:half + (l + 1) * HEAD_DIM, :])
    h = a + pltpu.roll(bm, bm.shape[0] - 1, 0)
    y = _dot(_gelu_tanh(h).astype(BF16), w2_ref[...])

    @pl.when(is_key)
    def _():
        yk = _row_rms(y, HEAD_DIM) * kn_ref[...]
        o_ref[0, 0, 0] = _rope(yk, cos_ref[...], s1_ref[...], s2_ref[...]).astype(o_ref.dtype)

    @pl.when(jnp.logical_not(is_key))
    def _():
        o_ref[0, 0, 0] = y.astype(o_ref.dtype)


def _compress(kr, pos, w1, w2, knorm0, cos_c, s1_c, s2_c, layer):
    _, b, t, _ = kr.shape
    nr = t // CMP_STRIDE
    width = CMP_STRIDE * HEAD_DIM
    return pl.pallas_call(
        _compress_body,
        grid=(2, b, N_KV_NSA),
        in_specs=[
            pl.BlockSpec((1, 1, t, HEAD_DIM), lambda ty, i, k: (ty * 2 + k, i, 0, 0)),
            pl.BlockSpec((None, None, 2, 1, width), lambda ty, i, k: (layer, ty, 0, 0, 0)),
            pl.BlockSpec((None, None, 2 * width, CMP_HIDDEN), lambda ty, i, k: (layer, ty, 0, 0)),
            pl.BlockSpec((None, None, CMP_HIDDEN, HEAD_DIM), lambda ty, i, k: (layer, ty, 0, 0)),
            pl.BlockSpec((None, 1, HEAD_DIM), lambda ty, i, k: (layer, 0, 0)),
            pl.BlockSpec((nr, HEAD_DIM), lambda ty, i, k: (0, 0)),
            pl.BlockSpec((nr, HEAD_DIM), lambda ty, i, k: (0, 0)),
            pl.BlockSpec((nr, HEAD_DIM), lambda ty, i, k: (0, 0)),
        ],
        out_specs=pl.BlockSpec((1, 1, 1, nr, HEAD_DIM), lambda ty, i, k: (ty, i, k, 0, 0)),
        out_shape=jax.ShapeDtypeStruct((2, b, N_KV_NSA, nr, HEAD_DIM), BF16),
        compiler_params=_cparams(3),
        name="compress",
    )(kr, pos, w1, w2, knorm0, cos_c, s1_c, s2_c)


def _split3(x):
    hi = x.astype(BF16)
    r1 = x - hi.astype(F32)
    mid = r1.astype(BF16)
    lo = (r1 - mid.astype(F32)).astype(BF16)
    return hi, mid, lo


def _cmp_branch(q, kc, vc, ovt, i, tq, n_blk):
    rows = q.shape[0]
    g = rows // tq
    s = _dot_nt(q, kc)
    row_g = lax.broadcasted_iota(jnp.int32, (rows, HEAD_DIM), 0)
    lane_g = lax.broadcasted_iota(jnp.int32, (rows, HEAD_DIM), 1)
    t_g = i * tq + (row_g & (tq - 1))
    valid = (lane_g * CMP_STRIDE + (CMP_BLOCK - 1)) <= t_g
    s = jnp.where(valid, s, NEG_INF)
    m = jnp.max(s, axis=-1, keepdims=True)
    p = jnp.where(valid, jnp.exp2(s - m), 0.0)
    p = p / jnp.maximum(jnp.sum(p, axis=-1, keepdims=True), 1e-30)
    o = _dot(p.astype(BF16), vc)
    psum = p[0:tq]
    for gg in range(1, g):
        psum = psum + p[gg * tq:(gg + 1) * tq]

    hi, mid, lo = _split3(psum)
    imp = ((_dot_nt(ovt, hi) + _dot_nt(ovt, mid)) + _dot_nt(ovt, lo))[:n_blk]
    blk = lax.broadcasted_iota(jnp.int32, (n_blk, tq), 0)
    t_blk = (i * tq + lax.broadcasted_iota(jnp.int32, (n_blk, tq), 1)) // SLC_BLOCK
    causal = blk <= t_blk
    forced = (blk == 0) | (causal & (blk > t_blk - N_LOCAL_SLC))
    score = jnp.where(forced, FORCED_SCORE, jnp.where(causal, imp, NEG_INF))
    cnt = jnp.zeros((n_blk, tq), jnp.int32)
    for c in range(n_blk):
        other = score[c:c + 1, :]
        ahead = (other > score) | ((other == score) & (blk > c))
        cnt = cnt + ahead.astype(jnp.int32)
    picked = jnp.where(cnt < SLC_TOPK, 1.0, 0.0)
    picked = jnp.concatenate([picked, jnp.zeros((HEAD_DIM - n_blk, tq), F32)], axis=0)
    return o, picked.T.astype(BF16)


def _online_update(s, v, m_ref, l_ref, acc_ref, rows=slice(None)):
    cols = [s[:, c:c + HEAD_DIM] for c in range(0, s.shape[1], HEAD_DIM)]
    m_el = functools.reduce(jnp.maximum, cols)
    m_prev = m_ref[rows]
    m_new = jnp.maximum(m_prev, jnp.max(m_el, axis=-1, keepdims=True))
    alpha = jnp.exp2(m_prev - m_new)
    ps = [jnp.exp2(c - m_new) for c in cols]
    l_ref[rows] = alpha * l_ref[rows] + functools.reduce(jnp.add, ps)
    p = jnp.concatenate([x.astype(BF16) for x in ps], axis=1)
    acc_ref[rows] = alpha * acc_ref[rows] + _dot(p, v)
    m_ref[rows] = m_new


def _normalize(acc_ref, l_ref):
    l = jnp.sum(l_ref[...], axis=-1, keepdims=True)
    return acc_ref[...] / jnp.maximum(l, 1e-30)


def _init_online(m_ref, l_ref, acc_ref):
    m_ref[...] = jnp.full(m_ref.shape, NEG_INF, F32)
    l_ref[...] = jnp.zeros(l_ref.shape, F32)
    acc_ref[...] = jnp.zeros(acc_ref.shape, F32)


def _flash_scratch(rows):
    return [pltpu.VMEM((rows, HEAD_DIM), F32)] * 3


def _win_branch(q, k_ref, v_ref, kh, i, tq, band):
    rows = q.shape[0]
    g = rows // tq
    start = pl.multiple_of(jnp.maximum(i * tq - WINDOW, 0), tq)
    k = k_ref[kh, 0, pl.ds(start, band), :]
    v = v_ref[kh, 0, pl.ds(start, band), :]
    row = lax.broadcasted_iota(jnp.int32, (tq, band), 0)
    col = lax.broadcasted_iota(jnp.int32, (tq, band), 1)
    kpos = start + col
    tpos = i * tq + row
    live = (kpos <= tpos) & (kpos > tpos - WINDOW)
    bias = jnp.where(live, 0.0, NEG_INF)
    s = (_dot_nt(q, k).reshape(g, tq, band) + bias[None]).reshape(rows, band)
    m = jnp.max(s, axis=-1, keepdims=True)
    p = jnp.exp2(s - m)
    l = jnp.sum(p, axis=-1, keepdims=True)
    return _dot(p.astype(BF16), v) / jnp.maximum(l, 1e-30)


def _nsa_body(q_ref, kc_ref, vc_ref, ks_ref, vs_ref, kw_ref, vw_ref, gt_ref, ov_ref, e_ref, o_ref,
              m_ref, l_ref, acc_ref, part_ref, sel_ref, *, tq, tk, band, n_blk):
    i = pl.program_id(1)
    g = GQA_GROUP
    rows = g * tq

    def queries(kh):
        return q_ref[kh * g:(kh + 1) * g].reshape(rows, HEAD_DIM)

    def gate(kh, branch):
        gt = gt_ref[kh, 0]
        return jnp.concatenate(
            [jnp.broadcast_to(gt[:, 3 * gg + branch:3 * gg + branch + 1], (tq, HEAD_DIM)) for gg in range(g)],
            axis=0)

    for kh in range(N_KV_NSA):
        rs = slice(kh * rows, (kh + 1) * rows)
        q = queries(kh)
        o_cmp, sel = _cmp_branch(q, kc_ref[0, 0, kh], vc_ref[0, 0, kh], ov_ref[...], i, tq, n_blk)
        sel_ref[kh] = sel
        part_ref[rs] = gate(kh, 0) * o_cmp + gate(kh, 2) * _win_branch(q, kw_ref, vw_ref, kh, i, tq, band)

    _init_online(m_ref, l_ref, acc_ref)
    row = lax.broadcasted_iota(jnp.int32, (tq, tk), 0)
    col = lax.broadcasted_iota(jnp.int32, (tq, tk), 1)
    last = (i * tq) // tk

    def step(kc, diagonal):
        start = pl.multiple_of(kc * tk, tk)
        for kh in range(N_KV_NSA):
            k = ks_ref[kh, 0, pl.ds(start, tk), :]
            v = vs_ref[kh, 0, pl.ds(start, tk), :]
            live = _dot(sel_ref[kh], e_ref[kc]) > 0.5
            if diagonal:
                live = live & ((start + col) <= (i * tq + row))
            bias = jnp.where(live, 0.0, NEG_INF)
            s = _dot_nt(queries(kh), k).reshape(g, tq, tk) + bias[None]
            _online_update(s.reshape(rows, tk), v, m_ref, l_ref, acc_ref, slice(kh * rows, (kh + 1) * rows))

    def body(kc, carry):
        step(kc, False)
        return carry

    lax.fori_loop(0, last, body, 0)
    step(last, True)
    o_slc = _normalize(acc_ref, l_ref)
    for kh in range(N_KV_NSA):
        rs = slice(kh * rows, (kh + 1) * rows)
        y = part_ref[rs] + gate(kh, 1) * o_slc[rs]
        for gg in range(g):
            hq = kh * g + gg
            o_ref[0, :, hq * HEAD_DIM:(hq + 1) * HEAD_DIM] = y[gg * tq:(gg + 1) * tq]


def _nsa(p4, kvc, sm4, ovt, e, tq=256):
    _, b, t, _ = p4.shape
    nr = kvc.shape[3]
    tk = e.shape[-1]
    band = min(WINDOW + tq, t)
    nk = N_KV_NSA
    rows = N_HEADS_NSA * tq

    def kv(slab):
        return pl.BlockSpec((nk, 1, t, HEAD_DIM), lambda bb, i: (slab // nk, bb, 0, 0))

    return pl.pallas_call(
        functools.partial(_nsa_body, tq=tq, tk=tk, band=band, n_blk=t // SLC_BLOCK),
        grid=(b, t // tq),
        in_specs=[
            pl.BlockSpec((N_HEADS_NSA, None, tq, HEAD_DIM), lambda bb, i: (SLAB_Q // N_HEADS_NSA, bb, i, 0)),
            pl.BlockSpec((1, 1, nk, nr, HEAD_DIM), lambda bb, i: (0, bb, 0, 0, 0)),
            pl.BlockSpec((1, 1, nk, nr, HEAD_DIM), lambda bb, i: (1, bb, 0, 0, 0)),
            kv(SLAB_KV + 4), kv(SLAB_KV + 6), kv(SLAB_KV + 8), kv(SLAB_KV + 10),
            pl.BlockSpec((nk, 1, tq, HEAD_DIM), lambda bb, i: (0, bb, i, 0)),
            pl.BlockSpec((HEAD_DIM, HEAD_DIM), lambda bb, i: (0, 0)),
            pl.BlockSpec(e.shape, lambda bb, i: (0, 0, 0)),
        ],
        out_specs=pl.BlockSpec((1, tq, NSA_WIDTH), lambda bb, i: (bb, i, 0)),
        out_shape=jax.ShapeDtypeStruct((b, t, NSA_WIDTH), F32),
        scratch_shapes=_flash_scratch(rows) + [pltpu.VMEM((rows, HEAD_DIM), F32),
                                               pltpu.VMEM((nk, tq, HEAD_DIM), BF16)],
        compiler_params=_cparams(2),
        name="nsa",
    )(p4, kvc, kvc, p4, p4, p4, p4, sm4, ovt, e)


def _fox_body(q_ref, k_ref, v_ref, ct_ref, o_ref, m_ref, l_ref, acc_ref, *, tq):
    i = pl.program_id(2)
    nh = q_ref.shape[0]
    _init_online(m_ref, l_ref, acc_ref)
    row = lax.broadcasted_iota(jnp.int32, (tq, tq), 0)
    col = lax.broadcasted_iota(jnp.int32, (tq, tq), 1)

    def step(kc, diagonal):
        start = pl.multiple_of(kc * tq, tq)
        for hh in range(nh):
            k = k_ref[hh, 0, pl.ds(start, tq), :]
            v = v_ref[hh, 0, pl.ds(start, tq), :]
            s = _dot_nt(q_ref[hh], k) - ct_ref[hh, pl.ds(kc, 1), :]
            if diagonal:
                s = jnp.where(col <= row, s, NEG_INF)
            _online_update(s, v, m_ref, l_ref, acc_ref, slice(hh * tq, (hh + 1) * tq))

    def body(kc, carry):
        step(kc, False)
        return carry

    lax.fori_loop(0, i, body, 0)
    step(i, True)
    o = _normalize(acc_ref, l_ref)
    for hh in range(nh):
        o_ref[0, :, hh * HEAD_DIM:(hh + 1) * HEAD_DIM] = o[hh * tq:(hh + 1) * tq]


FOX_HEADS_PER_STEP = 8
assert SLAB_FOX % FOX_HEADS_PER_STEP == 0 and N_HEADS_FOX % FOX_HEADS_PER_STEP == 0


def _fox_attn(p4, ctr, tq=512):
    _, b, t, _ = p4.shape
    nh = N_HEADS_FOX
    hs = FOX_HEADS_PER_STEP
    q0, k0, v0 = (SLAB_FOX // hs, (SLAB_FOX + nh) // hs, (SLAB_FOX + 2 * nh) // hs)
    return pl.pallas_call(
        functools.partial(_fox_body, tq=tq),
        grid=(b, nh // hs, t // tq),
        in_specs=[
            pl.BlockSpec((hs, None, tq, HEAD_DIM), lambda bb, h, i: (q0 + h, bb, i, 0)),
            pl.BlockSpec((hs, 1, t, HEAD_DIM), lambda bb, h, i: (k0 + h, bb, 0, 0)),
            pl.BlockSpec((hs, 1, t, HEAD_DIM), lambda bb, h, i: (v0 + h, bb, 0, 0)),
            pl.BlockSpec((hs, t // tq, tq), lambda bb, h, i: (bb * (nh // hs) + h, 0, 0)),
        ],
        out_specs=pl.BlockSpec((1, tq, hs * HEAD_DIM), lambda bb, h, i: (bb, i, h)),
        out_shape=jax.ShapeDtypeStruct((b, t, FOX_WIDTH), F32),
        scratch_shapes=_flash_scratch(hs * tq),
        compiler_params=_cparams(3),
        name="fox_attn",
    )(p4, p4, p4, ctr)


def _out_body(on_ref, of_ref, x_ref, nn_ref, fn_ref, w_ref, o_ref):
    nsa = (_row_rms(on_ref[...], NSA_WIDTH) * nn_ref[...]).astype(BF16)
    fox = (_row_rms(of_ref[...], FOX_WIDTH) * fn_ref[...]).astype(BF16)
    o_ref[...] = x_ref[...] + (_dot(nsa, w_ref[:NSA_WIDTH, :]) + _dot(fox, w_ref[NSA_WIDTH:, :]))


def _out(on, of, x, nn, fn, w, layer, tm=512):
    n, d = x.shape
    return pl.pallas_call(
        _out_body,
        grid=(n // tm,),
        in_specs=[
            pl.BlockSpec((tm, NSA_WIDTH), lambda i: (i, 0)),
            pl.BlockSpec((tm, FOX_WIDTH), lambda i: (i, 0)),
            pl.BlockSpec((tm, d), lambda i: (i, 0)),
            pl.BlockSpec((None, 1, NSA_WIDTH), lambda i: (layer, 0, 0)),
            pl.BlockSpec((None, 1, FOX_WIDTH), lambda i: (layer, 0, 0)),
            pl.BlockSpec((None, NSA_WIDTH + FOX_WIDTH, d), lambda i: (layer, 0, 0)),
        ],
        out_specs=pl.BlockSpec((tm, d), lambda i: (i, 0)),
        out_shape=jax.ShapeDtypeStruct((n, d), F32),
        compiler_params=_cparams(1),
        name="out_proj",
    )(on, of, x, nn, fn, w)


def _rope_tables(pos):
    inv = ROPE_THETA ** (-jnp.arange(0, ROPE_DIM, 2, dtype=F32) / ROPE_DIM)
    ang = pos.astype(F32)[:, None] * inv[None, :]
    cos, sin = jnp.cos(ang), jnp.sin(ang)
    n = pos.shape[0]
    pad = HEAD_DIM - ROPE_DIM
    cos_t = jnp.concatenate([cos, cos, jnp.ones((n, pad), F32)], axis=-1)
    s1 = jnp.concatenate([-sin, jnp.zeros((n, HEAD_DIM - ROPE_HALF), F32)], axis=-1)
    s2 = jnp.concatenate([jnp.zeros((n, ROPE_HALF), F32), sin, jnp.zeros((n, pad), F32)], axis=-1)
    return cos_t, s1, s2


def _overlap_matrix(t):
    n_cmp = (t - CMP_BLOCK) // CMP_STRIDE + 1
    n_slc = t // SLC_BLOCK
    cmp_start = np.arange(HEAD_DIM) * CMP_STRIDE
    slc_start = np.arange(HEAD_DIM) * SLC_BLOCK
    ov = ((cmp_start[:, None] < slc_start[None, :] + SLC_BLOCK)
          & (cmp_start[:, None] + CMP_BLOCK > slc_start[None, :]))
    ov = ov & (np.arange(HEAD_DIM)[:, None] < n_cmp) & (np.arange(HEAD_DIM)[None, :] < n_slc)
    return jnp.asarray(ov.T.astype(np.float32), BF16)


def _expand_matrix(t, tk):
    kpos = np.arange(t).reshape(t // tk, 1, tk)
    j = np.arange(HEAD_DIM).reshape(1, HEAD_DIM, 1)
    return jnp.asarray((kpos // SLC_BLOCK == j).astype(np.float32), BF16)


def _mixer(x2, b, t, layer, wts, tabs):
    n = x2.shape[0]
    p3, kc32, sm = _proj(x2, wts["mix_norm"], wts["w_main"], wts["w_small"], wts["gains"],
                         tabs["cos"], tabs["s1"], tabs["s2"], wts["fbias"], layer, t)
    p4 = p3.reshape(N_SLABS, b, t, HEAD_DIM)
    kr = kc32.reshape(HEADS_PER_TILE, b, t, HEAD_DIM)
    sm4 = sm.reshape(N_KV_NSA, b, t, HEAD_DIM)
    ct = _cumsum(sm4)
    fox_tq = min(512, t)
    ctr = ct.reshape(b * N_HEADS_FOX, t // fox_tq, fox_tq)
    kvc = _compress(kr, wts["pos"], wts["cmp_w1"], wts["cmp_w2"], wts["knorm0"],
                    tabs["cos_c"], tabs["s1_c"], tabs["s2_c"], layer)
    o_nsa = _nsa(p4, kvc, sm4, tabs["ov"], tabs["e"])
    o_fox = _fox_attn(p4, ctr, fox_tq)
    return _out(o_nsa.reshape(n, NSA_WIDTH), o_fox.reshape(n, FOX_WIDTH), x2, wts["nsa_out_norm"],
                wts["fox_out_norm"], wts["w_out"], layer)


def _prep_weights(ffn1_norm, ffn1_w_gate, ffn1_w_up, ffn1_w_down, mix_norm, w_in, nsa_q_norm, nsa_k_norm,
                  cmp_pos_emb, cmp_w1, cmp_w2, nsa_out_norm, fox_q_norm, fox_k_norm, fox_forget_bias,
                  fox_out_norm, w_out, ffn2_norm, ffn2_w_gate, ffn2_w_up, ffn2_w_down):
    depth, d, _ = w_in.shape
    c0 = COLS_NSA_Q + COLS_NSA_KV
    c1 = c0 + COLS_NSA_GATE
    c2 = c1 + COLS_FOX_QKV
    w_small = jnp.concatenate(
        [w_in[:, :, c0:c1], w_in[:, :, c2:], jnp.zeros((depth, d, HEAD_DIM - COLS_NSA_GATE - COLS_FOX_F), F32)],
        axis=-1).astype(BF16)
    w_main = jnp.concatenate([w_in[:, :, c1:c2], w_in[:, :, :c0]], axis=-1).astype(BF16)

    def tile(v):
        return jnp.concatenate([v] * HEADS_PER_TILE, axis=-1)

    one = jnp.ones((depth, HEAD_DIM), F32)
    gains = [tile(one)] * N_PROJ_TILES
    gains[0] = gains[1] = tile(fox_q_norm * Q_SCALE)
    gains[2] = gains[3] = tile(fox_k_norm)
    gains[TILE_Q] = gains[TILE_Q + 1] = tile(nsa_q_norm * Q_SCALE)
    gains[TILE_CMP_KV + 1] = jnp.concatenate([nsa_k_norm[:, 1]] * 2 + [one] * 2, axis=-1)
    gains[TILE_CMP_KV + 2] = jnp.concatenate([nsa_k_norm[:, 2]] * 2 + [one] * 2, axis=-1)
    gains = jnp.stack(gains, axis=1)[:, :, None, :]

    fbias = jnp.zeros((depth, 1, HEAD_DIM), F32).at[:, 0, LANE_LOGF:LANE_LOGF + N_HEADS_FOX].set(
        fox_forget_bias)
    half = CMP_BLOCK * HEAD_DIM // 2
    pos = cmp_pos_emb.reshape(depth, 2, 2, 1, half)
    return dict(
        ffn1_norm=ffn1_norm[:, None, :], ffn2_norm=ffn2_norm[:, None, :],
        ffn1=(ffn1_w_gate, ffn1_w_up, ffn1_w_down), ffn2=(ffn2_w_gate, ffn2_w_up, ffn2_w_down),
        mix_norm=mix_norm[:, None, :], w_main=w_main, w_small=w_small, gains=gains, fbias=fbias, pos=pos,
        cmp_w1=cmp_w1.astype(BF16), cmp_w2=cmp_w2.astype(BF16), knorm0=nsa_k_norm[:, 0][:, None, :],
        nsa_out_norm=nsa_out_norm[:, None, :], fox_out_norm=fox_out_norm[:, None, :],
        w_out=w_out.astype(BF16),
    )


def _tables(t):
    cos, s1, s2 = _rope_tables(jnp.arange(t))
    cmp_end = jnp.arange(t // CMP_STRIDE) * CMP_STRIDE + (CMP_BLOCK - 1)
    cos_c, s1_c, s2_c = _rope_tables(cmp_end)
    return dict(cos=cos, s1=s1, s2=s2, cos_c=cos_c, s1_c=s1_c, s2_c=s2_c,
                ov=_overlap_matrix(t), e=_expand_matrix(t, min(512, t)))


def kernel(x, ffn1_norm, ffn1_w_gate, ffn1_w_up, ffn1_w_down, mix_norm, w_in, nsa_q_norm, nsa_k_norm, cmp_pos_emb, cmp_w1, cmp_w2, nsa_out_norm, fox_q_norm, fox_k_norm, fox_forget_bias, fox_out_norm, w_out, ffn2_norm, ffn2_w_gate, ffn2_w_up, ffn2_w_down):
    b, t, d = x.shape
    depth = w_in.shape[0]
    wts = _prep_weights(ffn1_norm, ffn1_w_gate, ffn1_w_up, ffn1_w_down, mix_norm, w_in, nsa_q_norm,
                        nsa_k_norm, cmp_pos_emb, cmp_w1, cmp_w2, nsa_out_norm, fox_q_norm, fox_k_norm,
                        fox_forget_bias, fox_out_norm, w_out, ffn2_norm, ffn2_w_gate, ffn2_w_up, ffn2_w_down)
    tabs = _tables(t)
    x2 = x.reshape(b * t, d)
    w16 = tuple(w[0].astype(BF16) for w in wts["ffn1"])
    for layer in range(depth):
        x2, w16 = _ffn(x2, wts["ffn1_norm"], *w16, layer, nxt=wts["ffn2"] + (layer,))
        x2 = _mixer(x2, b, t, layer, wts, tabs)
        nxt = wts["ffn1"] + (layer + 1,) if layer + 1 < depth else None
        x2, w16 = _ffn(x2, wts["ffn2_norm"], *w16, layer, nxt=nxt)
    return x2.reshape(b, t, d)
```

```python
import functools

import numpy as np
import jax
import jax.numpy as jnp
from jax import lax
from jax.experimental import pallas as pl
from jax.experimental.pallas import tpu as pltpu

F32 = jnp.float32
BF16 = jnp.bfloat16

HEAD_DIM = 128
N_HEADS_NSA = 8
N_KV_NSA = 2
GQA_GROUP = 4
N_HEADS_FOX = 8
NSA_WIDTH = N_HEADS_NSA * HEAD_DIM
FOX_WIDTH = N_HEADS_FOX * HEAD_DIM
ROPE_DIM = 32
ROPE_HALF = ROPE_DIM // 2
ROPE_THETA = 500000.0
CMP_BLOCK = 32
CMP_STRIDE = 16
CMP_HIDDEN = 256
SLC_BLOCK = 64
SLC_TOPK = 8
N_LOCAL_SLC = 2
WINDOW = 512
EPS = 1e-6
NEG_INF = -1e30
FORCED_SCORE = 1e9
SCALE = HEAD_DIM ** -0.5
LOG2E = float(np.log2(np.e))
Q_SCALE = SCALE * LOG2E

COLS_NSA_Q = NSA_WIDTH
COLS_NSA_KV = 3 * 2 * N_KV_NSA * HEAD_DIM
COLS_NSA_GATE = 3 * N_HEADS_NSA
COLS_FOX_QKV = 3 * FOX_WIDTH
COLS_FOX_F = N_HEADS_FOX
COLS_MAIN = COLS_NSA_Q + COLS_NSA_KV + COLS_FOX_QKV

HEADS_PER_TILE = 4
PROJ_TN = HEADS_PER_TILE * HEAD_DIM
N_PROJ_TILES = COLS_MAIN // PROJ_TN
N_SLABS = COLS_MAIN // HEAD_DIM
SLAB_FOX = 0
SLAB_Q = 24
SLAB_KV = 32
TILE_Q = SLAB_Q // HEADS_PER_TILE
TILE_CMP_KV = SLAB_KV // HEADS_PER_TILE
PROJ_ROW_CHUNKS = 8
FFN_ROW_CHUNKS = 2
LANE_LOGF = COLS_NSA_GATE

VMEM_LIMIT_BYTES = 56 * 1024 * 1024


def _cparams(n_axes):
    return pltpu.CompilerParams(dimension_semantics=("arbitrary",) * n_axes,
                                vmem_limit_bytes=VMEM_LIMIT_BYTES)


def _dot(a, b):
    return jnp.dot(a, b, preferred_element_type=F32)


def _dot_nt(a, b):
    return lax.dot_general(a, b, (((1,), (1,)), ((), ())), preferred_element_type=F32)


def _row_rms(x, width):
    return x * lax.rsqrt(jnp.sum(x * x, axis=-1, keepdims=True) * (1.0 / width) + EPS)


def _ffn_body(x_ref, g_ref, wg_ref, wu_ref, wd_ref, *rest, cast_steps):
    j = pl.program_id(1)
    if cast_steps:
        *srcs, o_ref, ng_ref, nu_ref, nd_ref, h_ref = rest

        @pl.when(pl.program_id(0) * pl.num_programs(1) + j < cast_steps)
        def _():
            for src, dst in zip(srcs, (ng_ref, nu_ref, nd_ref)):
                dst[...] = src[...].astype(dst.dtype)
    else:
        o_ref, h_ref = rest

    def step(first):
        chunks = FFN_ROW_CHUNKS if first else 1
        cm = o_ref.shape[0] // chunks
        for c in range(chunks):
            rs = slice(c * cm, (c + 1) * cm)
            if first:
                base = x_ref[rs, :]
                h = (_row_rms(base, base.shape[-1]) * g_ref[...]).astype(BF16)
                h_ref[rs, :] = h
            else:
                base = o_ref[rs, :]
                h = h_ref[rs, :]
            a = _dot(h, wg_ref[...])
            b = _dot(h, wu_ref[...])
            u = (0.5 * a * jax.nn.sigmoid(a)) * b
            o_ref[rs, :] = base + _dot(u.astype(BF16), wd_ref[...])

    @pl.when(j == 0)
    def _():
        step(True)

    @pl.when(j > 0)
    def _():
        step(False)


FFN_CAST_ROWS = 16
FFN_CAST_COLS = 2816


def _ffn(x, g, wg, wu, wd, layer, nxt=None, tm=1024, tf=512):
    n, d = x.shape
    f = wg.shape[-1]
    nj = f // tf
    in_specs = [
        pl.BlockSpec((tm, d), lambda i, j: (i, 0)),
        pl.BlockSpec((None, 1, d), lambda i, j: (layer, 0, 0)),
        pl.BlockSpec((d, tf), lambda i, j: (0, j)),
        pl.BlockSpec((d, tf), lambda i, j: (0, j)),
        pl.BlockSpec((tf, d), lambda i, j: (j, 0)),
    ]
    out_specs = [pl.BlockSpec((tm, d), lambda i, j: (i, 0))]
    out_shape = [jax.ShapeDtypeStruct((n, d), F32)]
    args = [x, g, wg, wu, wd]
    cast_steps = 0
    if nxt is not None:
        nl = nxt[3]
        rows = FFN_CAST_ROWS
        for src in nxt[:3]:
            r, c = src.shape[1:]
            ncol = pl.cdiv(c, FFN_CAST_COLS)
            cb = c // ncol
            steps = (r // rows) * ncol
            cast_steps = max(cast_steps, steps)
            assert (n // tm) * nj >= steps, "not enough grid steps to cast the next weights"

            def slab(i, j, steps=steps, ncol=ncol):
                s = jnp.minimum(i * nj + j, steps - 1)
                return s // ncol, s % ncol

            in_specs.append(pl.BlockSpec((None, rows, cb), lambda i, j, slab=slab: (nl,) + slab(i, j)))
            out_specs.append(pl.BlockSpec((rows, cb), slab))
            out_shape.append(jax.ShapeDtypeStruct((r, c), BF16))
            args.append(src)
    outs = pl.pallas_call(
        functools.partial(_ffn_body, cast_steps=cast_steps),
        grid=(n // tm, nj),
        in_specs=in_specs,
        out_specs=out_specs,
        out_shape=out_shape,
        scratch_shapes=[pltpu.VMEM((tm, d), BF16)],
        compiler_params=_cparams(2),
        name="ffn",
    )(*args)
    return outs[0], tuple(outs[1:])


def _rope(y, cos, s1, s2):
    return y * cos + pltpu.roll(y, HEAD_DIM - ROPE_HALF, 1) * s1 + pltpu.roll(y, ROPE_HALF, 1) * s2


def _proj_body(x_ref, g_ref, w_ref, ws_ref, gain_ref, cos_ref, s1_ref, s2_ref, bias_ref,
               p_ref, kc_ref, sm_ref, h_ref):
    j = pl.program_id(1)
    gain = gain_ref[...]
    tm = h_ref.shape[0]

    def normalize_rows(rs):
        x = x_ref[rs, :]
        h = (_row_rms(x, x.shape[-1]) * g_ref[...]).astype(BF16)
        h_ref[rs, :] = h
        y = _dot(h, ws_ref[...])
        lane = lax.broadcasted_iota(jnp.int32, y.shape, 1)
        z = y + bias_ref[...]
        logsig = jnp.minimum(z, 0.0) - jnp.log1p(jnp.exp(-jnp.abs(z)))
        small = jnp.where(lane < LANE_LOGF, jax.nn.sigmoid(y), logsig)
        sm_ref[0, rs, :] = small
        sm_ref[1, rs, :] = pltpu.roll(small, HEAD_DIM - 3 * GQA_GROUP, 1)
        return h

    def raw(y, hh, rs):
        return y

    def normed(y, hh, rs):
        return _row_rms(y, HEAD_DIM) * gain[:, hh * HEAD_DIM:(hh + 1) * HEAD_DIM]

    def roped(y, hh, rs):
        return _rope(normed(y, hh, rs), cos_ref[rs, :], s1_ref[rs, :], s2_ref[rs, :])

    def emit(fns, keep_f32=False, first=False):
        chunks = 1 if all(fn is raw for fn in fns) and not first else PROJ_ROW_CHUNKS
        cm = tm // chunks
        for c in range(chunks):
            rs = slice(c * cm, (c + 1) * cm)
            h = normalize_rows(rs) if first else h_ref[rs, :]
            acc = _dot(h, w_ref[...])
            for hh, fn in enumerate(fns):
                y = acc[:, hh * HEAD_DIM:(hh + 1) * HEAD_DIM]
                p_ref[hh, rs, :] = fn(y, hh, rs).astype(p_ref.dtype)
                if keep_f32:
                    kc_ref[hh, rs, :] = y

    @pl.when(j == 0)
    def _():
        emit([normed] * 4, first=True)

    @pl.when((j >= 1) & (j < 4))
    def _():
        emit([normed] * 4)

    @pl.when((j >= 4) & (j < TILE_Q))
    def _():
        emit([raw] * 4)

    @pl.when((j >= TILE_Q) & (j < TILE_CMP_KV))
    def _():
        emit([roped] * 4)

    @pl.when(j == TILE_CMP_KV)
    def _():
        emit([raw] * 4, keep_f32=True)

    @pl.when(j > TILE_CMP_KV)
    def _():
        emit([roped, roped, raw, raw])


def _proj(x, g, w_main, w_small, gains, cos, s1, s2, bias, layer, seq, tm=1024):
    n, d = x.shape
    tm = min(tm, seq)
    tpb = seq // tm
    hp = HEADS_PER_TILE
    return pl.pallas_call(
        _proj_body,
        grid=(n // tm, N_PROJ_TILES),
        in_specs=[
            pl.BlockSpec((tm, d), lambda i, j: (i, 0)),
            pl.BlockSpec((None, 1, d), lambda i, j: (layer, 0, 0)),
            pl.BlockSpec((None, d, PROJ_TN), lambda i, j: (layer, 0, j)),
            pl.BlockSpec((None, d, HEAD_DIM), lambda i, j: (layer, 0, 0)),
            pl.BlockSpec((None, None, 1, PROJ_TN), lambda i, j: (layer, j, 0, 0)),
            pl.BlockSpec((tm, HEAD_DIM), lambda i, j: (i % tpb, 0)),
            pl.BlockSpec((tm, HEAD_DIM), lambda i, j: (i % tpb, 0)),
            pl.BlockSpec((tm, HEAD_DIM), lambda i, j: (i % tpb, 0)),
            pl.BlockSpec((None, 1, HEAD_DIM), lambda i, j: (layer, 0, 0)),
        ],
        out_specs=[
            pl.BlockSpec((hp, tm, HEAD_DIM), lambda i, j: (j, i, 0)),
            pl.BlockSpec((hp, tm, HEAD_DIM), lambda i, j: (0, i, 0)),
            pl.BlockSpec((N_KV_NSA, tm, HEAD_DIM), lambda i, j: (0, i, 0)),
        ],
        out_shape=[
            jax.ShapeDtypeStruct((N_SLABS, n, HEAD_DIM), BF16),
            jax.ShapeDtypeStruct((hp, n, HEAD_DIM), F32),
            jax.ShapeDtypeStruct((N_KV_NSA, n, HEAD_DIM), F32),
        ],
        scratch_shapes=[pltpu.VMEM((tm, d), BF16)],
        compiler_params=_cparams(2),
        name="proj",
    )(x, g, w_main, w_small, gains, cos, s1, s2, bias)


def _cumsum_body(x_ref, ct_ref):
    x = x_ref[0, 0]
    t = x.shape[0]
    row = lax.broadcasted_iota(jnp.int32, x.shape, 0)
    s = 1
    while s < t:
        x = x + jnp.where(row >= s, pltpu.roll(x, s, 0), 0.0)
        s *= 2
    ct_ref[0] = x.T[LANE_LOGF:LANE_LOGF + N_HEADS_FOX, :] * LOG2E


def _cumsum(sm4):
    _, b, t, _ = sm4.shape
    return pl.pallas_call(
        _cumsum_body,
        grid=(b,),
        in_specs=[pl.BlockSpec((1, 1, t, HEAD_DIM), lambda i: (0, i, 0, 0))],
        out_specs=pl.BlockSpec((1, N_HEADS_FOX, t), lambda i: (i, 0, 0)),
        out_shape=jax.ShapeDtypeStruct((b, N_HEADS_FOX, t), F32),
        compiler_params=_cparams(1),
        name="fox_cumsum",
    )(sm4)


def _gelu_tanh(x):
    c = float(np.sqrt(2.0 / np.pi))
    return x * (0.5 * (1.0 + jnp.tanh(c * (x + 0.044715 * (x * x * x)))))


def _compress_body(k_ref, pos_ref, w1_ref, w2_ref, kn_ref, cos_ref, s1_ref, s2_ref, o_ref):
    is_key = pl.program_id(0) == 0
    nk = k_ref.shape[0]
    nr = o_ref.shape[-2]
    half = CMP_STRIDE * HEAD_DIM
    a = jnp.zeros((nk * nr, CMP_HIDDEN), F32)
    bm = jnp.zeros((nk * nr, CMP_HIDDEN), F32)
    for l in range(CMP_STRIDE):
        tok = jnp.concatenate([k_ref[kh, 0, pl.ds(l, nr, stride=CMP_STRIDE), :] for kh in range(nk)], axis=0)
        sl = slice(l * HEAD_DIM, (l + 1) * HEAD_DIM)
        a = a + _dot((tok + pos_ref[0][:, sl]).astype(BF16), w1_ref[sl, :])
        bm = bm + _dot((tok + pos_ref[1][:, sl]).astype(BF16), w1_ref[half + l * HEAD_DIM:half + (l + 1) * HEAD_DIM, :])
    nxt = jnp.concatenate([pltpu.roll(bm[kh * nr:(kh + 1) * nr], nr - 1, 0) for kh in range(nk)], axis=0)
    y = _dot(_gelu_tanh(a + nxt).astype(BF16), w2_ref[...])

    @pl.when(is_key)
    def _():
        yk = _row_rms(y, HEAD_DIM) * kn_ref[...]
        for kh in range(nk):
            rs = slice(kh * nr, (kh + 1) * nr)
            o_ref[0, 0, kh] = _rope(yk[rs], cos_ref[...], s1_ref[...], s2_ref[...]).astype(o_ref.dtype)

    @pl.when(jnp.logical_not(is_key))
    def _():
        for kh in range(nk):
            o_ref[0, 0, kh] = y[kh * nr:(kh + 1) * nr].astype(o_ref.dtype)


def _compress(kr, pos, w1, w2, knorm0, cos_c, s1_c, s2_c, layer):
    _, b, t, _ = kr.shape
    nr = t // CMP_STRIDE
    width = CMP_STRIDE * HEAD_DIM
    nk = N_KV_NSA
    return pl.pallas_call(
        _compress_body,
        grid=(2, b),
        in_specs=[
            pl.BlockSpec((nk, 1, t, HEAD_DIM), lambda ty, i: (ty, i, 0, 0)),
            pl.BlockSpec((None, None, 2, 1, width), lambda ty, i: (layer, ty, 0, 0, 0)),
            pl.BlockSpec((None, None, 2 * width, CMP_HIDDEN), lambda ty, i: (layer, ty, 0, 0)),
            pl.BlockSpec((None, None, CMP_HIDDEN, HEAD_DIM), lambda ty, i: (layer, ty, 0, 0)),
            pl.BlockSpec((None, 1, HEAD_DIM), lambda ty, i: (layer, 0, 0)),
            pl.BlockSpec((nr, HEAD_DIM), lambda ty, i: (0, 0)),
            pl.BlockSpec((nr, HEAD_DIM), lambda ty, i: (0, 0)),
            pl.BlockSpec((nr, HEAD_DIM), lambda ty, i: (0, 0)),
        ],
        out_specs=pl.BlockSpec((1, 1, nk, nr, HEAD_DIM), lambda ty, i: (ty, i, 0, 0, 0)),
        out_shape=jax.ShapeDtypeStruct((2, b, nk, nr, HEAD_DIM), BF16),
        compiler_params=_cparams(2),
        name="compress",
    )(kr, pos, w1, w2, knorm0, cos_c, s1_c, s2_c)


def _split3(x):
    hi = x.astype(BF16)
    r1 = x - hi.astype(F32)
    mid = r1.astype(BF16)
    lo = (r1 - mid.astype(F32)).astype(BF16)
    return hi, mid, lo


def _cmp_branch(q, kc, vc, ovt, i, tq, n_blk):
    rows = q.shape[0]
    g = rows // tq
    s = _dot_nt(q, kc)
    row_g = lax.broadcasted_iota(jnp.int32, (rows, HEAD_DIM), 0)
    lane_g = lax.broadcasted_iota(jnp.int32, (rows, HEAD_DIM), 1)
    t_g = i * tq + (row_g & (tq - 1))
    valid = (lane_g * CMP_STRIDE + (CMP_BLOCK - 1)) <= t_g
    s = jnp.where(valid, s, NEG_INF)
    m = jnp.max(s, axis=-1, keepdims=True)
    p = jnp.where(valid, jnp.exp2(s - m), 0.0)
    p = p / jnp.maximum(jnp.sum(p, axis=-1, keepdims=True), 1e-30)
    o = _dot(p.astype(BF16), vc)
    psum = p[0:tq]
    for gg in range(1, g):
        psum = psum + p[gg * tq:(gg + 1) * tq]

    hi, mid, lo = _split3(psum)
    imp = ((_dot_nt(ovt, hi) + _dot_nt(ovt, mid)) + _dot_nt(ovt, lo))[:n_blk]
    blk = lax.broadcasted_iota(jnp.int32, (n_blk, tq), 0)
    t_blk = (i * tq + lax.broadcasted_iota(jnp.int32, (n_blk, tq), 1)) // SLC_BLOCK
    causal = blk <= t_blk
    forced = (blk == 0) | (causal & (blk > t_blk - N_LOCAL_SLC))
    score = jnp.where(forced, FORCED_SCORE, jnp.where(causal, imp, NEG_INF))
    cnt = jnp.zeros((n_blk, tq), jnp.int32)
    for c in range(n_blk):
        other = score[c:c + 1, :]
        ahead = (other > score) | ((other == score) & (blk > c))
        cnt = cnt + ahead.astype(jnp.int32)
    picked = jnp.where(cnt < SLC_TOPK, 1.0, 0.0)
    picked = jnp.concatenate([picked, jnp.zeros((HEAD_DIM - n_blk, tq), F32)], axis=0)
    return o, picked.T.astype(BF16)


def _online_update(s, v, m_ref, l_ref, acc_ref, rows=slice(None)):
    cols = [s[:, c:c + HEAD_DIM] for c in range(0, s.shape[1], HEAD_DIM)]
    m_el = functools.reduce(jnp.maximum, cols)
    m_prev = m_ref[rows]
    m_new = jnp.maximum(m_prev, jnp.max(m_el, axis=-1, keepdims=True))
    alpha = jnp.exp2(m_prev - m_new)
    ps = [jnp.exp2(c - m_new) for c in cols]
    l_ref[rows] = alpha * l_ref[rows] + functools.reduce(jnp.add, ps)
    p = jnp.concatenate([x.astype(BF16) for x in ps], axis=1)
    acc_ref[rows] = alpha * acc_ref[rows] + _dot(p, v)
    m_ref[rows] = m_new


def _normalize(acc_ref, l_ref):
    l = jnp.sum(l_ref[...], axis=-1, keepdims=True)
    return acc_ref[...] / jnp.maximum(l, 1e-30)


def _init_online(m_ref, l_ref, acc_ref):
    m_ref[...] = jnp.full(m_ref.shape, NEG_INF, F32)
    l_ref[...] = jnp.zeros(l_ref.shape, F32)
    acc_ref[...] = jnp.zeros(acc_ref.shape, F32)


def _flash_scratch(rows):
    return [pltpu.VMEM((rows, HEAD_DIM), F32)] * 3


def _win_branch(q, k_ref, v_ref, kh, i, tq, band):
    rows = q.shape[0]
    g = rows // tq
    start = pl.multiple_of(jnp.maximum(i * tq - WINDOW, 0), tq)
    k = k_ref[kh, 0, pl.ds(start, band), :]
    v = v_ref[kh, 0, pl.ds(start, band), :]
    row = lax.broadcasted_iota(jnp.int32, (tq, band), 0)
    col = lax.broadcasted_iota(jnp.int32, (tq, band), 1)
    kpos = start + col
    tpos = i * tq + row
    live = (kpos <= tpos) & (kpos > tpos - WINDOW)
    bias = jnp.where(live, 0.0, NEG_INF)
    s = (_dot_nt(q, k).reshape(g, tq, band) + bias[None]).reshape(rows, band)
    m = jnp.max(s, axis=-1, keepdims=True)
    p = jnp.exp2(s - m)
    l = jnp.sum(p, axis=-1, keepdims=True)
    return _dot(p.astype(BF16), v) / jnp.maximum(l, 1e-30)


def _nsa_body(q_ref, kc_ref, vc_ref, ks_ref, vs_ref, kw_ref, vw_ref, gt_ref, ov_ref, e_ref, o_ref,
              m_ref, l_ref, acc_ref, part_ref, sel_ref, *, tq, tk, band, n_blk):
    i = pl.program_id(1)
    g = GQA_GROUP
    rows = g * tq

    def queries(kh):
        return q_ref[kh * g:(kh + 1) * g].reshape(rows, HEAD_DIM)

    def gate(kh, branch):
        gt = gt_ref[kh, 0]
        return jnp.concatenate(
            [jnp.broadcast_to(gt[:, 3 * gg + branch:3 * gg + branch + 1], (tq, HEAD_DIM)) for gg in range(g)],
            axis=0)

    for kh in range(N_KV_NSA):
        rs = slice(kh * rows, (kh + 1) * rows)
        q = queries(kh)
        o_cmp, sel = _cmp_branch(q, kc_ref[0, 0, kh], vc_ref[0, 0, kh], ov_ref[...], i, tq, n_blk)
        sel_ref[kh] = sel
        part_ref[rs] = gate(kh, 0) * o_cmp + gate(kh, 2) * _win_branch(q, kw_ref, vw_ref, kh, i, tq, band)

    _init_online(m_ref, l_ref, acc_ref)
    row = lax.broadcasted_iota(jnp.int32, (tq, tk), 0)
    col = lax.broadcasted_iota(jnp.int32, (tq, tk), 1)
    last = (i * tq) // tk

    def step(kc, diagonal):
        start = pl.multiple_of(kc * tk, tk)
        for kh in range(N_KV_NSA):
            k = ks_ref[kh, 0, pl.ds(start, tk), :]
            v = vs_ref[kh, 0, pl.ds(start, tk), :]
            live = _dot(sel_ref[kh], e_ref[kc]) > 0.5
            if diagonal:
                live = live & ((start + col) <= (i * tq + row))
            bias = jnp.where(live, 0.0, NEG_INF)
            s = _dot_nt(queries(kh), k).reshape(g, tq, tk) + bias[None]
            _online_update(s.reshape(rows, tk), v, m_ref, l_ref, acc_ref, slice(kh * rows, (kh + 1) * rows))

    def body(kc, carry):
        step(kc, False)
        return carry

    lax.fori_loop(0, last, body, 0)
    step(last, True)
    o_slc = _normalize(acc_ref, l_ref)
    for kh in range(N_KV_NSA):
        rs = slice(kh * rows, (kh + 1) * rows)
        y = part_ref[rs] + gate(kh, 1) * o_slc[rs]
        for gg in range(g):
            hq = kh * g + gg
            o_ref[0, :, hq * HEAD_DIM:(hq + 1) * HEAD_DIM] = y[gg * tq:(gg + 1) * tq]


def _nsa(p4, kvc, sm4, ovt, e, tq=256):
    _, b, t, _ = p4.shape
    nr = kvc.shape[3]
    tk = e.shape[-1]
    band = min(WINDOW + tq, t)
    nk = N_KV_NSA
    rows = N_HEADS_NSA * tq

    def kv(slab):
        return pl.BlockSpec((nk, 1, t, HEAD_DIM), lambda bb, i: (slab // nk, bb, 0, 0))

    return pl.pallas_call(
        functools.partial(_nsa_body, tq=tq, tk=tk, band=band, n_blk=t // SLC_BLOCK),
        grid=(b, t // tq),
        in_specs=[
            pl.BlockSpec((N_HEADS_NSA, None, tq, HEAD_DIM), lambda bb, i: (SLAB_Q // N_HEADS_NSA, bb, i, 0)),
            pl.BlockSpec((1, 1, nk, nr, HEAD_DIM), lambda bb, i: (0, bb, 0, 0, 0)),
            pl.BlockSpec((1, 1, nk, nr, HEAD_DIM), lambda bb, i: (1, bb, 0, 0, 0)),
            kv(SLAB_KV + 4), kv(SLAB_KV + 6), kv(SLAB_KV + 8), kv(SLAB_KV + 10),
            pl.BlockSpec((nk, 1, tq, HEAD_DIM), lambda bb, i: (0, bb, i, 0)),
            pl.BlockSpec((HEAD_DIM, HEAD_DIM), lambda bb, i: (0, 0)),
            pl.BlockSpec(e.shape, lambda bb, i: (0, 0, 0)),
        ],
        out_specs=pl.BlockSpec((1, tq, NSA_WIDTH), lambda bb, i: (bb, i, 0)),
        out_shape=jax.ShapeDtypeStruct((b, t, NSA_WIDTH), F32),
        scratch_shapes=_flash_scratch(rows) + [pltpu.VMEM((rows, HEAD_DIM), F32),
                                               pltpu.VMEM((nk, tq, HEAD_DIM), BF16)],
        compiler_params=_cparams(2),
        name="nsa",
    )(p4, kvc, kvc, p4, p4, p4, p4, sm4, ovt, e)


def _fox_body(q_ref, k_ref, v_ref, ct_ref, o_ref, m_ref, l_ref, acc_ref, *, tq):
    i = pl.program_id(2)
    nh = q_ref.shape[0]
    _init_online(m_ref, l_ref, acc_ref)
    row = lax.broadcasted_iota(jnp.int32, (tq, tq), 0)
    col = lax.broadcasted_iota(jnp.int32, (tq, tq), 1)

    def step(kc, diagonal):
        start = pl.multiple_of(kc * tq, tq)
        for hh in range(nh):
            k = k_ref[hh, 0, pl.ds(start, tq), :]
            v = v_ref[hh, 0, pl.ds(start, tq), :]
            s = _dot_nt(q_ref[hh], k) - ct_ref[hh, pl.ds(kc, 1), :]
            if diagonal:
                s = jnp.where(col <= row, s, NEG_INF)
            _online_update(s, v, m_ref, l_ref, acc_ref, slice(hh * tq, (hh + 1) * tq))

    def body(kc, carry):
        step(kc, False)
        return carry

    lax.fori_loop(0, i, body, 0)
    step(i, True)
    o = _normalize(acc_ref, l_ref)
    for hh in range(nh):
        o_ref[0, :, hh * HEAD_DIM:(hh + 1) * HEAD_DIM] = o[hh * tq:(hh + 1) * tq]


FOX_HEADS_PER_STEP = 8
assert SLAB_FOX % FOX_HEADS_PER_STEP == 0 and N_HEADS_FOX % FOX_HEADS_PER_STEP == 0


def _fox_attn(p4, ctr, tq=512):
    _, b, t, _ = p4.shape
    nh = N_HEADS_FOX
    hs = FOX_HEADS_PER_STEP
    q0, k0, v0 = (SLAB_FOX // hs, (SLAB_FOX + nh) // hs, (SLAB_FOX + 2 * nh) // hs)
    return pl.pallas_call(
        functools.partial(_fox_body, tq=tq),
        grid=(b, nh // hs, t // tq),
        in_specs=[
            pl.BlockSpec((hs, None, tq, HEAD_DIM), lambda bb, h, i: (q0 + h, bb, i, 0)),
            pl.BlockSpec((hs, 1, t, HEAD_DIM), lambda bb, h, i: (k0 + h, bb, 0, 0)),
            pl.BlockSpec((hs, 1, t, HEAD_DIM), lambda bb, h, i: (v0 + h, bb, 0, 0)),
            pl.BlockSpec((hs, t // tq, tq), lambda bb, h, i: (bb * (nh // hs) + h, 0, 0)),
        ],
        out_specs=pl.BlockSpec((1, tq, hs * HEAD_DIM), lambda bb, h, i: (bb, i, h)),
        out_shape=jax.ShapeDtypeStruct((b, t, FOX_WIDTH), F32),
        scratch_shapes=_flash_scratch(hs * tq),
        compiler_params=_cparams(3),
        name="fox_attn",
    )(p4, p4, p4, ctr)


def _out_body(on_ref, of_ref, x_ref, nn_ref, fn_ref, w_ref, o_ref):
    nsa = (_row_rms(on_ref[...], NSA_WIDTH) * nn_ref[...]).astype(BF16)
    fox = (_row_rms(of_ref[...], FOX_WIDTH) * fn_ref[...]).astype(BF16)
    o_ref[...] = x_ref[...] + (_dot(nsa, w_ref[:NSA_WIDTH, :]) + _dot(fox, w_ref[NSA_WIDTH:, :]))


def _out(on, of, x, nn, fn, w, layer, tm=512):
    n, d = x.shape
    return pl.pallas_call(
        _out_body,
        grid=(n // tm,),
        in_specs=[
            pl.BlockSpec((tm, NSA_WIDTH), lambda i: (i, 0)),
            pl.BlockSpec((tm, FOX_WIDTH), lambda i: (i, 0)),
            pl.BlockSpec((tm, d), lambda i: (i, 0)),
            pl.BlockSpec((None, 1, NSA_WIDTH), lambda i: (layer, 0, 0)),
            pl.BlockSpec((None, 1, FOX_WIDTH), lambda i: (layer, 0, 0)),
            pl.BlockSpec((None, NSA_WIDTH + FOX_WIDTH, d), lambda i: (layer, 0, 0)),
        ],
        out_specs=pl.BlockSpec((tm, d), lambda i: (i, 0)),
        out_shape=jax.ShapeDtypeStruct((n, d), F32),
        compiler_params=_cparams(1),
        name="out_proj",
    )(on, of, x, nn, fn, w)


def _rope_tables(pos):
    inv = ROPE_THETA ** (-jnp.arange(0, ROPE_DIM, 2, dtype=F32) / ROPE_DIM)
    ang = pos.astype(F32)[:, None] * inv[None, :]
    cos, sin = jnp.cos(ang), jnp.sin(ang)
    n = pos.shape[0]
    pad = HEAD_DIM - ROPE_DIM
    cos_t = jnp.concatenate([cos, cos, jnp.ones((n, pad), F32)], axis=-1)
    s1 = jnp.concatenate([-sin, jnp.zeros((n, HEAD_DIM - ROPE_HALF), F32)], axis=-1)
    s2 = jnp.concatenate([jnp.zeros((n, ROPE_HALF), F32), sin, jnp.zeros((n, pad), F32)], axis=-1)
    return cos_t, s1, s2


def _overlap_matrix(t):
    n_cmp = (t - CMP_BLOCK) // CMP_STRIDE + 1
    n_slc = t // SLC_BLOCK
    cmp_start = np.arange(HEAD_DIM) * CMP_STRIDE
    slc_start = np.arange(HEAD_DIM) * SLC_BLOCK
    ov = ((cmp_start[:, None] < slc_start[None, :] + SLC_BLOCK)
          & (cmp_start[:, None] + CMP_BLOCK > slc_start[None, :]))
    ov = ov & (np.arange(HEAD_DIM)[:, None] < n_cmp) & (np.arange(HEAD_DIM)[None, :] < n_slc)
    return jnp.asarray(ov.T.astype(np.float32), BF16)


def _expand_matrix(t, tk):
    kpos = np.arange(t).reshape(t // tk, 1, tk)
    j = np.arange(HEAD_DIM).reshape(1, HEAD_DIM, 1)
    return jnp.asarray((kpos // SLC_BLOCK == j).astype(np.float32), BF16)


def _mixer(x2, b, t, layer, wts, tabs):
    n = x2.shape[0]
    p3, kc32, sm = _proj(x2, wts["mix_norm"], wts["w_main"], wts["w_small"], wts["gains"],
                         tabs["cos"], tabs["s1"], tabs["s2"], wts["fbias"], layer, t)
    p4 = p3.reshape(N_SLABS, b, t, HEAD_DIM)
    kr = kc32.reshape(HEADS_PER_TILE, b, t, HEAD_DIM)
    sm4 = sm.reshape(N_KV_NSA, b, t, HEAD_DIM)
    ct = _cumsum(sm4)
    fox_tq = min(512, t)
    ctr = ct.reshape(b * N_HEADS_FOX, t // fox_tq, fox_tq)
    kvc = _compress(kr, wts["pos"], wts["cmp_w1"], wts["cmp_w2"], wts["knorm0"],
                    tabs["cos_c"], tabs["s1_c"], tabs["s2_c"], layer)
    o_nsa = _nsa(p4, kvc, sm4, tabs["ov"], tabs["e"])
    o_fox = _fox_attn(p4, ctr, fox_tq)
    return _out(o_nsa.reshape(n, NSA_WIDTH), o_fox.reshape(n, FOX_WIDTH), x2, wts["nsa_out_norm"],
                wts["fox_out_norm"], wts["w_out"], layer)


def _prep_weights(ffn1_norm, ffn1_w_gate, ffn1_w_up, ffn1_w_down, mix_norm, w_in, nsa_q_norm, nsa_k_norm,
                  cmp_pos_emb, cmp_w1, cmp_w2, nsa_out_norm, fox_q_norm, fox_k_norm, fox_forget_bias,
                  fox_out_norm, w_out, ffn2_norm, ffn2_w_gate, ffn2_w_up, ffn2_w_down):
    depth, d, _ = w_in.shape
    c0 = COLS_NSA_Q + COLS_NSA_KV
    c1 = c0 + COLS_NSA_GATE
    c2 = c1 + COLS_FOX_QKV
    w16 = w_in.astype(BF16)
    w_small = jnp.concatenate(
        [w16[:, :, c0:c1], w16[:, :, c2:], jnp.zeros((depth, d, HEAD_DIM - COLS_NSA_GATE - COLS_FOX_F), BF16)],
        axis=-1)
    w_main = jnp.concatenate([w16[:, :, c1:c2], w16[:, :, :c0]], axis=-1)

    def tile(v):
        return jnp.concatenate([v] * HEADS_PER_TILE, axis=-1)

    one = jnp.ones((depth, HEAD_DIM), F32)
    gains = [tile(one)] * N_PROJ_TILES
    gains[0] = gains[1] = tile(fox_q_norm * Q_SCALE)
    gains[2] = gains[3] = tile(fox_k_norm)
    gains[TILE_Q] = gains[TILE_Q + 1] = tile(nsa_q_norm * Q_SCALE)
    gains[TILE_CMP_KV + 1] = jnp.concatenate([nsa_k_norm[:, 1]] * 2 + [one] * 2, axis=-1)
    gains[TILE_CMP_KV + 2] = jnp.concatenate([nsa_k_norm[:, 2]] * 2 + [one] * 2, axis=-1)
    gains = jnp.stack(gains, axis=1)[:, :, None, :]

    fbias = jnp.zeros((depth, 1, HEAD_DIM), F32).at[:, 0, LANE_LOGF:LANE_LOGF + N_HEADS_FOX].set(
        fox_forget_bias)
    half = CMP_BLOCK * HEAD_DIM // 2
    pos = cmp_pos_emb.reshape(depth, 2, 2, 1, half)
    return dict(
        ffn1_norm=ffn1_norm[:, None, :], ffn2_norm=ffn2_norm[:, None, :],
        ffn1=(ffn1_w_gate, ffn1_w_up, ffn1_w_down), ffn2=(ffn2_w_gate, ffn2_w_up, ffn2_w_down),
        mix_norm=mix_norm[:, None, :], w_main=w_main, w_small=w_small, gains=gains, fbias=fbias, pos=pos,
        cmp_w1=cmp_w1.astype(BF16), cmp_w2=cmp_w2.astype(BF16), knorm0=nsa_k_norm[:, 0][:, None, :],
        nsa_out_norm=nsa_out_norm[:, None, :], fox_out_norm=fox_out_norm[:, None, :],
        w_out=w_out.astype(BF16),
    )


def _tables(t):
    cos, s1, s2 = _rope_tables(jnp.arange(t))
    cmp_end = jnp.arange(t // CMP_STRIDE) * CMP_STRIDE + (CMP_BLOCK - 1)
    cos_c, s1_c, s2_c = _rope_tables(cmp_end)
    return dict(cos=cos, s1=s1, s2=s2, cos_c=cos_c, s1_c=s1_c, s2_c=s2_c,
                ov=_overlap_matrix(t), e=_expand_matrix(t, min(512, t)))


def kernel(x, ffn1_norm, ffn1_w_gate, ffn1_w_up, ffn1_w_down, mix_norm, w_in, nsa_q_norm, nsa_k_norm, cmp_pos_emb, cmp_w1, cmp_w2, nsa_out_norm, fox_q_norm, fox_k_norm, fox_forget_bias, fox_out_norm, w_out, ffn2_norm, ffn2_w_gate, ffn2_w_up, ffn2_w_down):
    b, t, d = x.shape
    depth = w_in.shape[0]
    wts = _prep_weights(ffn1_norm, ffn1_w_gate, ffn1_w_up, ffn1_w_down, mix_norm, w_in, nsa_q_norm,
                        nsa_k_norm, cmp_pos_emb, cmp_w1, cmp_w2, nsa_out_norm, fox_q_norm, fox_k_norm,
                        fox_forget_bias, fox_out_norm, w_out, ffn2_norm, ffn2_w_gate, ffn2_w_up, ffn2_w_down)
    tabs = _tables(t)
    x2 = x.reshape(b * t, d)
    w16 = tuple(w[0].astype(BF16) for w in wts["ffn1"])
    for layer in range(depth):
        x2, w16 = _ffn(x2, wts["ffn1_norm"], *w16, layer, nxt=wts["ffn2"] + (layer,))
        x2 = _mixer(x2, b, t, layer, wts, tabs)
        nxt = wts["ffn1"] + (layer + 1,) if layer + 1 < depth else None
        x2, w16 = _ffn(x2, wts["ffn2_norm"], *w16, layer, nxt=nxt)
    return x2.reshape(b, t, d)
```

```python
import functools

import numpy as np
import jax
import jax.numpy as jnp
from jax import lax
from jax.experimental import pallas as pl
from jax.experimental.pallas import tpu as pltpu

F32 = jnp.float32
BF16 = jnp.bfloat16

HEAD_DIM = 128
N_HEADS_NSA = 8
N_KV_NSA = 2
GQA_GROUP = 4
N_HEADS_FOX = 8
NSA_WIDTH = N_HEADS_NSA * HEAD_DIM
FOX_WIDTH = N_HEADS_FOX * HEAD_DIM
ROPE_DIM = 32
ROPE_HALF = ROPE_DIM // 2
ROPE_THETA = 500000.0
CMP_BLOCK = 32
CMP_STRIDE = 16
CMP_HIDDEN = 256
SLC_BLOCK = 64
SLC_TOPK = 8
N_LOCAL_SLC = 2
WINDOW = 512
EPS = 1e-6
NEG_INF = -1e30
FORCED_SCORE = 1e9
SCALE = HEAD_DIM ** -0.5
LOG2E = float(np.log2(np.e))
Q_SCALE = SCALE * LOG2E

COLS_NSA_Q = NSA_WIDTH
COLS_NSA_KV = 3 * 2 * N_KV_NSA * HEAD_DIM
COLS_NSA_GATE = 3 * N_HEADS_NSA
COLS_FOX_QKV = 3 * FOX_WIDTH
COLS_FOX_F = N_HEADS_FOX
COLS_MAIN = COLS_NSA_Q + COLS_NSA_KV + COLS_FOX_QKV

HEADS_PER_TILE = 4
PROJ_TN = HEADS_PER_TILE * HEAD_DIM
N_PROJ_TILES = COLS_MAIN // PROJ_TN
N_SLABS = COLS_MAIN // HEAD_DIM
SLAB_FOX = 0
SLAB_Q = 24
SLAB_KV = 32
TILE_Q = SLAB_Q // HEADS_PER_TILE
TILE_CMP_KV = SLAB_KV // HEADS_PER_TILE
PROJ_ROW_CHUNKS = 8
FFN_ROW_CHUNKS = 2
LANE_LOGF = COLS_NSA_GATE

VMEM_LIMIT_BYTES = 56 * 1024 * 1024


def _cparams(n_axes):
    return pltpu.CompilerParams(dimension_semantics=("arbitrary",) * n_axes,
                                vmem_limit_bytes=VMEM_LIMIT_BYTES)


def _dot(a, b):
    return jnp.dot(a, b, preferred_element_type=F32)


def _dot_nt(a, b):
    return lax.dot_general(a, b, (((1,), (1,)), ((), ())), preferred_element_type=F32)


def _row_rms(x, width):
    return x * lax.rsqrt(jnp.sum(x * x, axis=-1, keepdims=True) * (1.0 / width) + EPS)


def _ffn_body(x_ref, g_ref, wg_ref, wu_ref, wd_ref, *rest, cast_steps):
    j = pl.program_id(1)
    if cast_steps:
        *srcs, o_ref, ng_ref, nu_ref, nd_ref, h_ref = rest

        @pl.when(pl.program_id(0) * pl.num_programs(1) + j < cast_steps)
        def _():
            for src, dst in zip(srcs, (ng_ref, nu_ref, nd_ref)):
                dst[...] = src[...].astype(dst.dtype)
    else:
        o_ref, h_ref = rest

    def step(first):
        chunks = FFN_ROW_CHUNKS if first else 1
        cm = o_ref.shape[0] // chunks
        for c in range(chunks):
            rs = slice(c * cm, (c + 1) * cm)
            if first:
                base = x_ref[rs, :]
                h = (_row_rms(base, base.shape[-1]) * g_ref[...]).astype(BF16)
                h_ref[rs, :] = h
            else:
                base = o_ref[rs, :]
                h = h_ref[rs, :]
            a = _dot(h, wg_ref[...])
            b = _dot(h, wu_ref[...])
            u = (0.5 * a * jax.nn.sigmoid(a)) * b
            o_ref[rs, :] = base + _dot(u.astype(BF16), wd_ref[...])

    @pl.when(j == 0)
    def _():
        step(True)

    @pl.when(j > 0)
    def _():
        step(False)


FFN_CAST_ROWS = 16
FFN_CAST_COLS = 2816


def _ffn(x, g, wg, wu, wd, layer, nxt=None, tm=1024, tf=512):
    n, d = x.shape
    f = wg.shape[-1]
    nj = f // tf
    in_specs = [
        pl.BlockSpec((tm, d), lambda i, j: (i, 0)),
        pl.BlockSpec((None, 1, d), lambda i, j: (layer, 0, 0)),
        pl.BlockSpec((d, tf), lambda i, j: (0, j)),
        pl.BlockSpec((d, tf), lambda i, j: (0, j)),
        pl.BlockSpec((tf, d), lambda i, j: (j, 0)),
    ]
    out_specs = [pl.BlockSpec((tm, d), lambda i, j: (i, 0))]
    out_shape = [jax.ShapeDtypeStruct((n, d), F32)]
    args = [x, g, wg, wu, wd]
    cast_steps = 0
    if nxt is not None:
        nl = nxt[3]
        rows = FFN_CAST_ROWS
        for src in nxt[:3]:
            r, c = src.shape[1:]
            ncol = pl.cdiv(c, FFN_CAST_COLS)
            cb = c // ncol
            steps = (r // rows) * ncol
            cast_steps = max(cast_steps, steps)
            assert (n // tm) * nj >= steps, "not enough grid steps to cast the next weights"

            def slab(i, j, steps=steps, ncol=ncol):
                s = jnp.minimum(i * nj + j, steps - 1)
                return s // ncol, s % ncol

            in_specs.append(pl.BlockSpec((None, rows, cb), lambda i, j, slab=slab: (nl,) + slab(i, j)))
            out_specs.append(pl.BlockSpec((rows, cb), slab))
            out_shape.append(jax.ShapeDtypeStruct((r, c), BF16))
            args.append(src)
    outs = pl.pallas_call(
        functools.partial(_ffn_body, cast_steps=cast_steps),
        grid=(n // tm, nj),
        in_specs=in_specs,
        out_specs=out_specs,
        out_shape=out_shape,
        scratch_shapes=[pltpu.VMEM((tm, d), BF16)],
        compiler_params=_cparams(2),
        name="ffn",
    )(*args)
    return outs[0], tuple(outs[1:])


def _cast_layer_body(src_ref, dst_ref):
    dst_ref[...] = src_ref[...].astype(dst_ref.dtype)


def _cast_layer(w, layer, rows=256):
    _, r, c = w.shape
    return pl.pallas_call(
        _cast_layer_body,
        grid=(r // rows,),
        in_specs=[pl.BlockSpec((None, rows, c), lambda i: (layer, i, 0))],
        out_specs=pl.BlockSpec((rows, c), lambda i: (i, 0)),
        out_shape=jax.ShapeDtypeStruct((r, c), BF16),
        compiler_params=_cparams(1),
        name="cast_layer",
    )(w)


def _rope(y, cos, s1, s2):
    return y * cos + pltpu.roll(y, HEAD_DIM - ROPE_HALF, 1) * s1 + pltpu.roll(y, ROPE_HALF, 1) * s2


def _proj_body(x_ref, g_ref, w_ref, ws_ref, gain_ref, cos_ref, s1_ref, s2_ref, bias_ref,
               p_ref, kc_ref, sm_ref, h_ref):
    j = pl.program_id(1)
    gain = gain_ref[...]
    tm = h_ref.shape[0]

    def normalize_rows(rs):
        x = x_ref[rs, :]
        h = (_row_rms(x, x.shape[-1]) * g_ref[...]).astype(BF16)
        h_ref[rs, :] = h
        y = _dot(h, ws_ref[...])
        lane = lax.broadcasted_iota(jnp.int32, y.shape, 1)
        z = y + bias_ref[...]
        logsig = jnp.minimum(z, 0.0) - jnp.log1p(jnp.exp(-jnp.abs(z)))
        small = jnp.where(lane < LANE_LOGF, jax.nn.sigmoid(y), logsig)
        sm_ref[0, rs, :] = small
        sm_ref[1, rs, :] = pltpu.roll(small, HEAD_DIM - 3 * GQA_GROUP, 1)
        return h

    def raw(y, hh, rs):
        return y

    def normed(y, hh, rs):
        return _row_rms(y, HEAD_DIM) * gain[:, hh * HEAD_DIM:(hh + 1) * HEAD_DIM]

    def roped(y, hh, rs):
        return _rope(normed(y, hh, rs), cos_ref[rs, :], s1_ref[rs, :], s2_ref[rs, :])

    def emit(fns, keep_f32=False, first=False):
        chunks = 1 if all(fn is raw for fn in fns) and not first else PROJ_ROW_CHUNKS
        cm = tm // chunks
        for c in range(chunks):
            rs = slice(c * cm, (c + 1) * cm)
            h = normalize_rows(rs) if first else h_ref[rs, :]
            acc = _dot(h, w_ref[...])
            for hh, fn in enumerate(fns):
                y = acc[:, hh * HEAD_DIM:(hh + 1) * HEAD_DIM]
                p_ref[hh, rs, :] = fn(y, hh, rs).astype(p_ref.dtype)
                if keep_f32:
                    kc_ref[hh, rs, :] = y

    @pl.when(j == 0)
    def _():
        emit([normed] * 4, first=True)

    @pl.when((j >= 1) & (j < 4))
    def _():
        emit([normed] * 4)

    @pl.when((j >= 4) & (j < TILE_Q))
    def _():
        emit([raw] * 4)

    @pl.when((j >= TILE_Q) & (j < TILE_CMP_KV))
    def _():
        emit([roped] * 4)

    @pl.when(j == TILE_CMP_KV)
    def _():
        emit([raw] * 4, keep_f32=True)

    @pl.when(j > TILE_CMP_KV)
    def _():
        emit([roped, roped, raw, raw])


def _proj(x, g, w_main, w_small, gains, cos, s1, s2, bias, layer, seq, tm=1024):
    n, d = x.shape
    tm = min(tm, seq)
    tpb = seq // tm
    hp = HEADS_PER_TILE
    return pl.pallas_call(
        _proj_body,
        grid=(n // tm, N_PROJ_TILES),
        in_specs=[
            pl.BlockSpec((tm, d), lambda i, j: (i, 0)),
            pl.BlockSpec((None, 1, d), lambda i, j: (layer, 0, 0)),
            pl.BlockSpec((None, d, PROJ_TN), lambda i, j: (layer, 0, j)),
            pl.BlockSpec((None, d, HEAD_DIM), lambda i, j: (layer, 0, 0)),
            pl.BlockSpec((None, None, 1, PROJ_TN), lambda i, j: (layer, j, 0, 0)),
            pl.BlockSpec((tm, HEAD_DIM), lambda i, j: (i % tpb, 0)),
            pl.BlockSpec((tm, HEAD_DIM), lambda i, j: (i % tpb, 0)),
            pl.BlockSpec((tm, HEAD_DIM), lambda i, j: (i % tpb, 0)),
            pl.BlockSpec((None, 1, HEAD_DIM), lambda i, j: (layer, 0, 0)),
        ],
        out_specs=[
            pl.BlockSpec((hp, tm, HEAD_DIM), lambda i, j: (j, i, 0)),
            pl.BlockSpec((hp, tm, HEAD_DIM), lambda i, j: (0, i, 0)),
            pl.BlockSpec((N_KV_NSA, tm, HEAD_DIM), lambda i, j: (0, i, 0)),
        ],
        out_shape=[
            jax.ShapeDtypeStruct((N_SLABS, n, HEAD_DIM), BF16),
            jax.ShapeDtypeStruct((hp, n, HEAD_DIM), F32),
            jax.ShapeDtypeStruct((N_KV_NSA, n, HEAD_DIM), F32),
        ],
        scratch_shapes=[pltpu.VMEM((tm, d), BF16)],
        compiler_params=_cparams(2),
        name="proj",
    )(x, g, w_main, w_small, gains, cos, s1, s2, bias)


def _cumsum_body(x_ref, ct_ref):
    x = x_ref[0, 0]
    t = x.shape[0]
    row = lax.broadcasted_iota(jnp.int32, x.shape, 0)
    s = 1
    while s < t:
        x = x + jnp.where(row >= s, pltpu.roll(x, s, 0), 0.0)
        s *= 2
    ct_ref[0] = x.T[LANE_LOGF:LANE_LOGF + N_HEADS_FOX, :] * LOG2E


def _cumsum(sm4):
    _, b, t, _ = sm4.shape
    return pl.pallas_call(
        _cumsum_body,
        grid=(b,),
        in_specs=[pl.BlockSpec((1, 1, t, HEAD_DIM), lambda i: (0, i, 0, 0))],
        out_specs=pl.BlockSpec((1, N_HEADS_FOX, t), lambda i: (i, 0, 0)),
        out_shape=jax.ShapeDtypeStruct((b, N_HEADS_FOX, t), F32),
        compiler_params=_cparams(1),
        name="fox_cumsum",
    )(sm4)


def _gelu_tanh(x):
    c = float(np.sqrt(2.0 / np.pi))
    return x * (0.5 * (1.0 + jnp.tanh(c * (x + 0.044715 * (x * x * x)))))


def _compress_body(k_ref, pos_ref, w1_ref, w2_ref, kn_ref, cos_ref, s1_ref, s2_ref, o_ref):
    is_key = pl.program_id(0) == 0
    nk = k_ref.shape[0]
    nr = o_ref.shape[-2]
    half = CMP_STRIDE * HEAD_DIM
    a = jnp.zeros((nk * nr, CMP_HIDDEN), F32)
    bm = jnp.zeros((nk * nr, CMP_HIDDEN), F32)
    for l in range(CMP_STRIDE):
        tok = jnp.concatenate([k_ref[kh, 0, pl.ds(l, nr, stride=CMP_STRIDE), :] for kh in range(nk)], axis=0)
        sl = slice(l * HEAD_DIM, (l + 1) * HEAD_DIM)
        a = a + _dot((tok + pos_ref[0][:, sl]).astype(BF16), w1_ref[sl, :])
        bm = bm + _dot((tok + pos_ref[1][:, sl]).astype(BF16), w1_ref[half + l * HEAD_DIM:half + (l + 1) * HEAD_DIM, :])
    nxt = jnp.concatenate([pltpu.roll(bm[kh * nr:(kh + 1) * nr], nr - 1, 0) for kh in range(nk)], axis=0)
    y = _dot(_gelu_tanh(a + nxt).astype(BF16), w2_ref[...])

    @pl.when(is_key)
    def _():
        yk = _row_rms(y, HEAD_DIM) * kn_ref[...]
        for kh in range(nk):
            rs = slice(kh * nr, (kh + 1) * nr)
            o_ref[0, 0, kh] = _rope(yk[rs], cos_ref[...], s1_ref[...], s2_ref[...]).astype(o_ref.dtype)

    @pl.when(jnp.logical_not(is_key))
    def _():
        for kh in range(nk):
            o_ref[0, 0, kh] = y[kh * nr:(kh + 1) * nr].astype(o_ref.dtype)


def _compress(kr, pos, w1, w2, knorm0, cos_c, s1_c, s2_c, layer):
    _, b, t, _ = kr.shape
    nr = t // CMP_STRIDE
    width = CMP_STRIDE * HEAD_DIM
    nk = N_KV_NSA
    return pl.pallas_call(
        _compress_body,
        grid=(2, b),
        in_specs=[
            pl.BlockSpec((nk, 1, t, HEAD_DIM), lambda ty, i: (ty, i, 0, 0)),
            pl.BlockSpec((None, None, 2, 1, width), lambda ty, i: (layer, ty, 0, 0, 0)),
            pl.BlockSpec((None, None, 2 * width, CMP_HIDDEN), lambda ty, i: (layer, ty, 0, 0)),
            pl.BlockSpec((None, None, CMP_HIDDEN, HEAD_DIM), lambda ty, i: (layer, ty, 0, 0)),
            pl.BlockSpec((None, 1, HEAD_DIM), lambda ty, i: (layer, 0, 0)),
            pl.BlockSpec((nr, HEAD_DIM), lambda ty, i: (0, 0)),
            pl.BlockSpec((nr, HEAD_DIM), lambda ty, i: (0, 0)),
            pl.BlockSpec((nr, HEAD_DIM), lambda ty, i: (0, 0)),
        ],
        out_specs=pl.BlockSpec((1, 1, nk, nr, HEAD_DIM), lambda ty, i: (ty, i, 0, 0, 0)),
        out_shape=jax.ShapeDtypeStruct((2, b, nk, nr, HEAD_DIM), BF16),
        compiler_params=_cparams(2),
        name="compress",
    )(kr, pos, w1, w2, knorm0, cos_c, s1_c, s2_c)


def _split3(x):
    hi = x.astype(BF16)
    r1 = x - hi.astype(F32)
    mid = r1.astype(BF16)
    lo = (r1 - mid.astype(F32)).astype(BF16)
    return hi, mid, lo


def _cmp_branch(q, kc, vc, ovt, i, tq, n_blk):
    rows = q.shape[0]
    g = rows // tq
    s = _dot_nt(q, kc)
    row_g = lax.broadcasted_iota(jnp.int32, (rows, HEAD_DIM), 0)
    lane_g = lax.broadcasted_iota(jnp.int32, (rows, HEAD_DIM), 1)
    t_g = i * tq + (row_g & (tq - 1))
    valid = (lane_g * CMP_STRIDE + (CMP_BLOCK - 1)) <= t_g
    s = jnp.where(valid, s, NEG_INF)
    m = jnp.max(s, axis=-1, keepdims=True)
    p = jnp.where(valid, jnp.exp2(s - m), 0.0)
    p = p / jnp.maximum(jnp.sum(p, axis=-1, keepdims=True), 1e-30)
    o = _dot(p.astype(BF16), vc)
    psum = p[0:tq]
    for gg in range(1, g):
        psum = psum + p[gg * tq:(gg + 1) * tq]

    hi, mid, lo = _split3(psum)
    imp = ((_dot_nt(ovt, hi) + _dot_nt(ovt, mid)) + _dot_nt(ovt, lo))[:n_blk]
    blk = lax.broadcasted_iota(jnp.int32, (n_blk, tq), 0)
    t_blk = (i * tq + lax.broadcasted_iota(jnp.int32, (n_blk, tq), 1)) // SLC_BLOCK
    causal = blk <= t_blk
    forced = (blk == 0) | (causal & (blk > t_blk - N_LOCAL_SLC))
    score = jnp.where(forced, FORCED_SCORE, jnp.where(causal, imp, NEG_INF))
    cnt = jnp.zeros((n_blk, tq), jnp.int32)
    for c in range(n_blk):
        other = score[c:c + 1, :]
        ahead = (other > score) | ((other == score) & (blk > c))
        cnt = cnt + ahead.astype(jnp.int32)
    picked = jnp.where(cnt < SLC_TOPK, 1.0, 0.0)
    picked = jnp.concatenate([picked, jnp.zeros((HEAD_DIM - n_blk, tq), F32)], axis=0)
    return o, picked.T.astype(BF16)


def _online_update(s, v, m_ref, l_ref, acc_ref, rows=slice(None)):
    cols = [s[:, c:c + HEAD_DIM] for c in range(0, s.shape[1], HEAD_DIM)]
    m_el = functools.reduce(jnp.maximum, cols)
    m_prev = m_ref[rows]
    m_new = jnp.maximum(m_prev, jnp.max(m_el, axis=-1, keepdims=True))
    alpha = jnp.exp2(m_prev - m_new)
    ps = [jnp.exp2(c - m_new) for c in cols]
    l_ref[rows] = alpha * l_ref[rows] + functools.reduce(jnp.add, ps)
    p = jnp.concatenate([x.astype(BF16) for x in ps], axis=1)
    acc_ref[rows] = alpha * acc_ref[rows] + _dot(p, v)
    m_ref[rows] = m_new


def _normalize(acc_ref, l_ref):
    l = jnp.sum(l_ref[...], axis=-1, keepdims=True)
    return acc_ref[...] / jnp.maximum(l, 1e-30)


def _init_online(m_ref, l_ref, acc_ref):
    m_ref[...] = jnp.full(m_ref.shape, NEG_INF, F32)
    l_ref[...] = jnp.zeros(l_ref.shape, F32)
    acc_ref[...] = jnp.zeros(acc_ref.shape, F32)


def _flash_scratch(rows):
    return [pltpu.VMEM((rows, HEAD_DIM), F32)] * 3


def _win_branch(q, k_ref, v_ref, kh, i, tq, band):
    rows = q.shape[0]
    g = rows // tq
    start = pl.multiple_of(jnp.maximum(i * tq - WINDOW, 0), tq)
    k = k_ref[kh, 0, pl.ds(start, band), :]
    v = v_ref[kh, 0, pl.ds(start, band), :]
    row = lax.broadcasted_iota(jnp.int32, (tq, band), 0)
    col = lax.broadcasted_iota(jnp.int32, (tq, band), 1)
    kpos = start + col
    tpos = i * tq + row
    live = (kpos <= tpos) & (kpos > tpos - WINDOW)
    bias = jnp.where(live, 0.0, NEG_INF)
    s = (_dot_nt(q, k).reshape(g, tq, band) + bias[None]).reshape(rows, band)
    m = jnp.max(s, axis=-1, keepdims=True)
    p = jnp.exp2(s - m)
    l = jnp.sum(p, axis=-1, keepdims=True)
    return _dot(p.astype(BF16), v) / jnp.maximum(l, 1e-30)


def _nsa_body(q_ref, kc_ref, vc_ref, ks_ref, vs_ref, kw_ref, vw_ref, gt_ref, ov_ref, e_ref, o_ref,
              m_ref, l_ref, acc_ref, part_ref, sel_ref, *, tq, tk, band, n_blk):
    i = pl.program_id(1)
    g = GQA_GROUP
    rows = g * tq

    def queries(kh):
        return q_ref[kh * g:(kh + 1) * g].reshape(rows, HEAD_DIM)

    def gate(kh, branch):
        gt = gt_ref[kh, 0]
        return jnp.concatenate(
            [jnp.broadcast_to(gt[:, 3 * gg + branch:3 * gg + branch + 1], (tq, HEAD_DIM)) for gg in range(g)],
            axis=0)

    for kh in range(N_KV_NSA):
        rs = slice(kh * rows, (kh + 1) * rows)
        q = queries(kh)
        o_cmp, sel = _cmp_branch(q, kc_ref[0, 0, kh], vc_ref[0, 0, kh], ov_ref[...], i, tq, n_blk)
        sel_ref[kh] = sel
        part_ref[rs] = gate(kh, 0) * o_cmp + gate(kh, 2) * _win_branch(q, kw_ref, vw_ref, kh, i, tq, band)

    _init_online(m_ref, l_ref, acc_ref)
    row = lax.broadcasted_iota(jnp.int32, (tq, tk), 0)
    col = lax.broadcasted_iota(jnp.int32, (tq, tk), 1)
    last = (i * tq) // tk

    def step(kc, diagonal):
        start = pl.multiple_of(kc * tk, tk)
        for kh in range(N_KV_NSA):
            k = ks_ref[kh, 0, pl.ds(start, tk), :]
            v = vs_ref[kh, 0, pl.ds(start, tk), :]
            live = _dot(sel_ref[kh], e_ref[kc]) > 0.5
            if diagonal:
                live = live & ((start + col) <= (i * tq + row))
            bias = jnp.where(live, 0.0, NEG_INF)
            s = _dot_nt(queries(kh), k).reshape(g, tq, tk) + bias[None]
            _online_update(s.reshape(rows, tk), v, m_ref, l_ref, acc_ref, slice(kh * rows, (kh + 1) * rows))

    def body(kc, carry):
        step(kc, False)
        return carry

    lax.fori_loop(0, last, body, 0)
    step(last, True)
    o_slc = _normalize(acc_ref, l_ref)
    for kh in range(N_KV_NSA):
        rs = slice(kh * rows, (kh + 1) * rows)
        y = part_ref[rs] + gate(kh, 1) * o_slc[rs]
        for gg in range(g):
            hq = kh * g + gg
            o_ref[0, :, hq * HEAD_DIM:(hq + 1) * HEAD_DIM] = y[gg * tq:(gg + 1) * tq]


def _nsa(p4, kvc, sm4, ovt, e, tq=256):
    _, b, t, _ = p4.shape
    nr = kvc.shape[3]
    tk = e.shape[-1]
    band = min(WINDOW + tq, t)
    nk = N_KV_NSA
    rows = N_HEADS_NSA * tq

    def kv(slab):
        return pl.BlockSpec((nk, 1, t, HEAD_DIM), lambda bb, i: (slab // nk, bb, 0, 0))

    return pl.pallas_call(
        functools.partial(_nsa_body, tq=tq, tk=tk, band=band, n_blk=t // SLC_BLOCK),
        grid=(b, t // tq),
        in_specs=[
            pl.BlockSpec((N_HEADS_NSA, None, tq, HEAD_DIM), lambda bb, i: (SLAB_Q // N_HEADS_NSA, bb, i, 0)),
            pl.BlockSpec((1, 1, nk, nr, HEAD_DIM), lambda bb, i: (0, bb, 0, 0, 0)),
            pl.BlockSpec((1, 1, nk, nr, HEAD_DIM), lambda bb, i: (1, bb, 0, 0, 0)),
            kv(SLAB_KV + 4), kv(SLAB_KV + 6), kv(SLAB_KV + 8), kv(SLAB_KV + 10),
            pl.BlockSpec((nk, 1, tq, HEAD_DIM), lambda bb, i: (0, bb, i, 0)),
            pl.BlockSpec((HEAD_DIM, HEAD_DIM), lambda bb, i: (0, 0)),
            pl.BlockSpec(e.shape, lambda bb, i: (0, 0, 0)),
        ],
        out_specs=pl.BlockSpec((1, tq, NSA_WIDTH), lambda bb, i: (bb, i, 0)),
        out_shape=jax.ShapeDtypeStruct((b, t, NSA_WIDTH), F32),
        scratch_shapes=_flash_scratch(rows) + [pltpu.VMEM((rows, HEAD_DIM), F32),
                                               pltpu.VMEM((nk, tq, HEAD_DIM), BF16)],
        compiler_params=_cparams(2),
        name="nsa",
    )(p4, kvc, kvc, p4, p4, p4, p4, sm4, ovt, e)


def _fox_body(q_ref, k_ref, v_ref, ct_ref, o_ref, m_ref, l_ref, acc_ref, *, tq):
    i = pl.program_id(2)
    nh = q_ref.shape[0]
    _init_online(m_ref, l_ref, acc_ref)
    row = lax.broadcasted_iota(jnp.int32, (tq, tq), 0)
    col = lax.broadcasted_iota(jnp.int32, (tq, tq), 1)

    def step(kc, diagonal):
        start = pl.multiple_of(kc * tq, tq)
        for hh in range(nh):
            k = k_ref[hh, 0, pl.ds(start, tq), :]
            v = v_ref[hh, 0, pl.ds(start, tq), :]
            s = _dot_nt(q_ref[hh], k) - ct_ref[hh, pl.ds(kc, 1), :]
            if diagonal:
                s = jnp.where(col <= row, s, NEG_INF)
            _online_update(s, v, m_ref, l_ref, acc_ref, slice(hh * tq, (hh + 1) * tq))

    def body(kc, carry):
        step(kc, False)
        return carry

    lax.fori_loop(0, i, body, 0)
    step(i, True)
    o = _normalize(acc_ref, l_ref)
    for hh in range(nh):
        o_ref[0, :, hh * HEAD_DIM:(hh + 1) * HEAD_DIM] = o[hh * tq:(hh + 1) * tq]


FOX_HEADS_PER_STEP = 8
assert SLAB_FOX % FOX_HEADS_PER_STEP == 0 and N_HEADS_FOX % FOX_HEADS_PER_STEP == 0


def _fox_attn(p4, ctr, tq=512):
    _, b, t, _ = p4.shape
    nh = N_HEADS_FOX
    hs = FOX_HEADS_PER_STEP
    q0, k0, v0 = (SLAB_FOX // hs, (SLAB_FOX + nh) // hs, (SLAB_FOX + 2 * nh) // hs)
    return pl.pallas_call(
        functools.partial(_fox_body, tq=tq),
        grid=(b, nh // hs, t // tq),
        in_specs=[
            pl.BlockSpec((hs, None, tq, HEAD_DIM), lambda bb, h, i: (q0 + h, bb, i, 0)),
            pl.BlockSpec((hs, 1, t, HEAD_DIM), lambda bb, h, i: (k0 + h, bb, 0, 0)),
            pl.BlockSpec((hs, 1, t, HEAD_DIM), lambda bb, h, i: (v0 + h, bb, 0, 0)),
            pl.BlockSpec((hs, t // tq, tq), lambda bb, h, i: (bb * (nh // hs) + h, 0, 0)),
        ],
        out_specs=pl.BlockSpec((1, tq, hs * HEAD_DIM), lambda bb, h, i: (bb, i, h)),
        out_shape=jax.ShapeDtypeStruct((b, t, FOX_WIDTH), F32),
        scratch_shapes=_flash_scratch(hs * tq),
        compiler_params=_cparams(3),
        name="fox_attn",
    )(p4, p4, p4, ctr)


def _out_body(on_ref, of_ref, x_ref, nn_ref, fn_ref, w_ref, o_ref):
    nsa = (_row_rms(on_ref[...], NSA_WIDTH) * nn_ref[...]).astype(BF16)
    fox = (_row_rms(of_ref[...], FOX_WIDTH) * fn_ref[...]).astype(BF16)
    o_ref[...] = x_ref[...] + (_dot(nsa, w_ref[:NSA_WIDTH, :]) + _dot(fox, w_ref[NSA_WIDTH:, :]))


def _out(on, of, x, nn, fn, w, layer, tm=512):
    n, d = x.shape
    return pl.pallas_call(
        _out_body,
        grid=(n // tm,),
        in_specs=[
            pl.BlockSpec((tm, NSA_WIDTH), lambda i: (i, 0)),
            pl.BlockSpec((tm, FOX_WIDTH), lambda i: (i, 0)),
            pl.BlockSpec((tm, d), lambda i: (i, 0)),
            pl.BlockSpec((None, 1, NSA_WIDTH), lambda i: (layer, 0, 0)),
            pl.BlockSpec((None, 1, FOX_WIDTH), lambda i: (layer, 0, 0)),
            pl.BlockSpec((None, NSA_WIDTH + FOX_WIDTH, d), lambda i: (layer, 0, 0)),
        ],
        out_specs=pl.BlockSpec((tm, d), lambda i: (i, 0)),
        out_shape=jax.ShapeDtypeStruct((n, d), F32),
        compiler_params=_cparams(1),
        name="out_proj",
    )(on, of, x, nn, fn, w)


def _rope_tables(pos):
    inv = ROPE_THETA ** (-jnp.arange(0, ROPE_DIM, 2, dtype=F32) / ROPE_DIM)
    ang = pos.astype(F32)[:, None] * inv[None, :]
    cos, sin = jnp.cos(ang), jnp.sin(ang)
    n = pos.shape[0]
    pad = HEAD_DIM - ROPE_DIM
    cos_t = jnp.concatenate([cos, cos, jnp.ones((n, pad), F32)], axis=-1)
    s1 = jnp.concatenate([-sin, jnp.zeros((n, HEAD_DIM - ROPE_HALF), F32)], axis=-1)
    s2 = jnp.concatenate([jnp.zeros((n, ROPE_HALF), F32), sin, jnp.zeros((n, pad), F32)], axis=-1)
    return cos_t, s1, s2


def _overlap_matrix(t):
    n_cmp = (t - CMP_BLOCK) // CMP_STRIDE + 1
    n_slc = t // SLC_BLOCK
    cmp_start = np.arange(HEAD_DIM) * CMP_STRIDE
    slc_start = np.arange(HEAD_DIM) * SLC_BLOCK
    ov = ((cmp_start[:, None] < slc_start[None, :] + SLC_BLOCK)
          & (cmp_start[:, None] + CMP_BLOCK > slc_start[None, :]))
    ov = ov & (np.arange(HEAD_DIM)[:, None] < n_cmp) & (np.arange(HEAD_DIM)[None, :] < n_slc)
    return jnp.asarray(ov.T.astype(np.float32), BF16)


def _expand_matrix(t, tk):
    kpos = np.arange(t).reshape(t // tk, 1, tk)
    j = np.arange(HEAD_DIM).reshape(1, HEAD_DIM, 1)
    return jnp.asarray((kpos // SLC_BLOCK == j).astype(np.float32), BF16)


def _mixer(x2, b, t, layer, wts, tabs):
    n = x2.shape[0]
    p3, kc32, sm = _proj(x2, wts["mix_norm"], wts["w_main"], wts["w_small"], wts["gains"],
                         tabs["cos"], tabs["s1"], tabs["s2"], wts["fbias"], layer, t)
    p4 = p3.reshape(N_SLABS, b, t, HEAD_DIM)
    kr = kc32.reshape(HEADS_PER_TILE, b, t, HEAD_DIM)
    sm4 = sm.reshape(N_KV_NSA, b, t, HEAD_DIM)
    ct = _cumsum(sm4)
    fox_tq = min(512, t)
    ctr = ct.reshape(b * N_HEADS_FOX, t // fox_tq, fox_tq)
    kvc = _compress(kr, wts["pos"], wts["cmp_w1"], wts["cmp_w2"], wts["knorm0"],
                    tabs["cos_c"], tabs["s1_c"], tabs["s2_c"], layer)
    o_nsa = _nsa(p4, kvc, sm4, tabs["ov"], tabs["e"])
    o_fox = _fox_attn(p4, ctr, fox_tq)
    return _out(o_nsa.reshape(n, NSA_WIDTH), o_fox.reshape(n, FOX_WIDTH), x2, wts["nsa_out_norm"],
                wts["fox_out_norm"], wts["w_out"], layer)


def _prep_weights(ffn1_norm, ffn1_w_gate, ffn1_w_up, ffn1_w_down, mix_norm, w_in, nsa_q_norm, nsa_k_norm,
                  cmp_pos_emb, cmp_w1, cmp_w2, nsa_out_norm, fox_q_norm, fox_k_norm, fox_forget_bias,
                  fox_out_norm, w_out, ffn2_norm, ffn2_w_gate, ffn2_w_up, ffn2_w_down):
    depth, d, _ = w_in.shape
    c0 = COLS_NSA_Q + COLS_NSA_KV
    c1 = c0 + COLS_NSA_GATE
    c2 = c1 + COLS_FOX_QKV
    w16 = w_in.astype(BF16)
    w_small = jnp.concatenate(
        [w16[:, :, c0:c1], w16[:, :, c2:], jnp.zeros((depth, d, HEAD_DIM - COLS_NSA_GATE - COLS_FOX_F), BF16)],
        axis=-1)
    w_main = jnp.concatenate([w16[:, :, c1:c2], w16[:, :, :c0]], axis=-1)

    def tile(v):
        return jnp.concatenate([v] * HEADS_PER_TILE, axis=-1)

    one = jnp.ones((depth, HEAD_DIM), F32)
    gains = [tile(one)] * N_PROJ_TILES
    gains[0] = gains[1] = tile(fox_q_norm * Q_SCALE)
    gains[2] = gains[3] = tile(fox_k_norm)
    gains[TILE_Q] = gains[TILE_Q + 1] = tile(nsa_q_norm * Q_SCALE)
    gains[TILE_CMP_KV + 1] = jnp.concatenate([nsa_k_norm[:, 1]] * 2 + [one] * 2, axis=-1)
    gains[TILE_CMP_KV + 2] = jnp.concatenate([nsa_k_norm[:, 2]] * 2 + [one] * 2, axis=-1)
    gains = jnp.stack(gains, axis=1)[:, :, None, :]

    fbias = jnp.zeros((depth, 1, HEAD_DIM), F32).at[:, 0, LANE_LOGF:LANE_LOGF + N_HEADS_FOX].set(
        fox_forget_bias)
    half = CMP_BLOCK * HEAD_DIM // 2
    pos = cmp_pos_emb.reshape(depth, 2, 2, 1, half)
    return dict(
        ffn1_norm=ffn1_norm[:, None, :], ffn2_norm=ffn2_norm[:, None, :],
        ffn1=(ffn1_w_gate, ffn1_w_up, ffn1_w_down), ffn2=(ffn2_w_gate, ffn2_w_up, ffn2_w_down),
        mix_norm=mix_norm[:, None, :], w_main=w_main, w_small=w_small, gains=gains, fbias=fbias, pos=pos,
        cmp_w1=cmp_w1.astype(BF16), cmp_w2=cmp_w2.astype(BF16), knorm0=nsa_k_norm[:, 0][:, None, :],
        nsa_out_norm=nsa_out_norm[:, None, :], fox_out_norm=fox_out_norm[:, None, :],
        w_out=w_out.astype(BF16),
    )


def _tables(t):
    cos, s1, s2 = _rope_tables(jnp.arange(t))
    cmp_end = jnp.arange(t // CMP_STRIDE) * CMP_STRIDE + (CMP_BLOCK - 1)
    cos_c, s1_c, s2_c = _rope_tables(cmp_end)
    return dict(cos=cos, s1=s1, s2=s2, cos_c=cos_c, s1_c=s1_c, s2_c=s2_c,
                ov=_overlap_matrix(t), e=_expand_matrix(t, min(512, t)))


def kernel(x, ffn1_norm, ffn1_w_gate, ffn1_w_up, ffn1_w_down, mix_norm, w_in, nsa_q_norm, nsa_k_norm, cmp_pos_emb, cmp_w1, cmp_w2, nsa_out_norm, fox_q_norm, fox_k_norm, fox_forget_bias, fox_out_norm, w_out, ffn2_norm, ffn2_w_gate, ffn2_w_up, ffn2_w_down):
    b, t, d = x.shape
    depth = w_in.shape[0]
    wts = _prep_weights(ffn1_norm, ffn1_w_gate, ffn1_w_up, ffn1_w_down, mix_norm, w_in, nsa_q_norm,
                        nsa_k_norm, cmp_pos_emb, cmp_w1, cmp_w2, nsa_out_norm, fox_q_norm, fox_k_norm,
                        fox_forget_bias, fox_out_norm, w_out, ffn2_norm, ffn2_w_gate, ffn2_w_up, ffn2_w_down)
    tabs = _tables(t)
    x2 = x.reshape(b * t, d)
    w16 = tuple(_cast_layer(w, 0) for w in wts["ffn1"])
    for layer in range(depth):
        x2, w16 = _ffn(x2, wts["ffn1_norm"], *w16, layer, nxt=wts["ffn2"] + (layer,))
        x2 = _mixer(x2, b, t, layer, wts, tabs)
        nxt = wts["ffn1"] + (layer + 1,) if layer + 1 < depth else None
        x2, w16 = _ffn(x2, wts["ffn2_norm"], *w16, layer, nxt=nxt)
    return x2.reshape(b, t, d)
```

```python
import functools

import numpy as np
import jax
import jax.numpy as jnp
from jax import lax
from jax.experimental import pallas as pl
from jax.experimental.pallas import tpu as pltpu

F32 = jnp.float32
BF16 = jnp.bfloat16

HEAD_DIM = 128
N_HEADS_NSA = 8
N_KV_NSA = 2
GQA_GROUP = 4
N_HEADS_FOX = 8
NSA_WIDTH = N_HEADS_NSA * HEAD_DIM
FOX_WIDTH = N_HEADS_FOX * HEAD_DIM
ROPE_DIM = 32
ROPE_HALF = ROPE_DIM // 2
ROPE_THETA = 500000.0
CMP_BLOCK = 32
CMP_STRIDE = 16
CMP_HIDDEN = 256
SLC_BLOCK = 64
SLC_TOPK = 8
N_LOCAL_SLC = 2
WINDOW = 512
EPS = 1e-6
NEG_INF = -1e30
FORCED_SCORE = 1e9
SCALE = HEAD_DIM ** -0.5
LOG2E = float(np.log2(np.e))
Q_SCALE = SCALE * LOG2E

COLS_NSA_Q = NSA_WIDTH
COLS_NSA_KV = 3 * 2 * N_KV_NSA * HEAD_DIM
COLS_NSA_GATE = 3 * N_HEADS_NSA
COLS_FOX_QKV = 3 * FOX_WIDTH
COLS_FOX_F = N_HEADS_FOX
COLS_MAIN = COLS_NSA_Q + COLS_NSA_KV + COLS_FOX_QKV

HEADS_PER_TILE = 4
PROJ_TN = HEADS_PER_TILE * HEAD_DIM
N_PROJ_TILES = COLS_MAIN // PROJ_TN
N_SLABS = COLS_MAIN // HEAD_DIM
SLAB_FOX = 0
SLAB_Q = 24
SLAB_KV = 32
TILE_Q = SLAB_Q // HEADS_PER_TILE
TILE_CMP_KV = SLAB_KV // HEADS_PER_TILE
PROJ_ROW_CHUNKS = 8
FFN_ROW_CHUNKS = 2
LANE_LOGF = COLS_NSA_GATE

VMEM_LIMIT_BYTES = 56 * 1024 * 1024


def _cparams(n_axes):
    return pltpu.CompilerParams(dimension_semantics=("arbitrary",) * n_axes,
                                vmem_limit_bytes=VMEM_LIMIT_BYTES)


def _dot(a, b):
    return jnp.dot(a, b, preferred_element_type=F32)


def _dot_nt(a, b):
    return lax.dot_general(a, b, (((1,), (1,)), ((), ())), preferred_element_type=F32)


def _row_rms(x, width):
    return x * lax.rsqrt(jnp.sum(x * x, axis=-1, keepdims=True) * (1.0 / width) + EPS)


def _ffn_body(x_ref, g_ref, wg_ref, wu_ref, wd_ref, *rest, cast_steps):
    j = pl.program_id(1)
    if cast_steps:
        *srcs, o_ref, ng_ref, nu_ref, nd_ref, h_ref = rest

        @pl.when(pl.program_id(0) * pl.num_programs(1) + j < cast_steps)
        def _():
            for src, dst in zip(srcs, (ng_ref, nu_ref, nd_ref)):
                dst[...] = src[...].astype(dst.dtype)
    else:
        o_ref, h_ref = rest

    def step(first):
        chunks = FFN_ROW_CHUNKS if first else 1
        cm = o_ref.shape[0] // chunks
        for c in range(chunks):
            rs = slice(c * cm, (c + 1) * cm)
            if first:
                base = x_ref[rs, :]
                h = (_row_rms(base, base.shape[-1]) * g_ref[...]).astype(BF16)
                h_ref[rs, :] = h
            else:
                base = o_ref[rs, :]
                h = h_ref[rs, :]
            a = _dot(h, wg_ref[...])
            b = _dot(h, wu_ref[...])
            u = (0.5 * a * jax.nn.sigmoid(a)) * b
            o_ref[rs, :] = base + _dot(u.astype(BF16), wd_ref[...])

    @pl.when(j == 0)
    def _():
        step(True)

    @pl.when(j > 0)
    def _():
        step(False)


FFN_TF = 512
FFN_CAST_ROWS_IN = 64
FFN_CAST_ROWS_OUT = 16


def _ffn(x, g, wg, wu, wd, layer, nxt=None, tm=1024):
    n, d = x.shape
    nj, _, tf = wg.shape
    in_specs = [
        pl.BlockSpec((tm, d), lambda i, j: (i, 0)),
        pl.BlockSpec((None, 1, d), lambda i, j: (layer, 0, 0)),
        pl.BlockSpec((None, d, tf), lambda i, j: (j, 0, 0)),
        pl.BlockSpec((None, d, tf), lambda i, j: (j, 0, 0)),
        pl.BlockSpec((tf, d), lambda i, j: (j, 0)),
    ]
    out_specs = [pl.BlockSpec((tm, d), lambda i, j: (i, 0))]
    out_shape = [jax.ShapeDtypeStruct((n, d), F32)]
    args = [x, g, wg, wu, wd]
    cast_steps = 0
    if nxt is not None:
        nl = nxt[3]
        n_steps = (n // tm) * nj

        def step_of(i, j, steps):
            return jnp.minimum(i * nj + j, steps - 1)

        for src in nxt[:2]:
            rows = FFN_CAST_ROWS_IN
            steps = (d // rows) * nj
            cast_steps = max(cast_steps, steps)
            in_specs.append(pl.BlockSpec(
                (None, rows, tf), lambda i, j, steps=steps: (nl, step_of(i, j, steps) // nj, step_of(i, j, steps) % nj)))
            out_specs.append(pl.BlockSpec(
                (None, rows, tf), lambda i, j, steps=steps: (step_of(i, j, steps) % nj, step_of(i, j, steps) // nj, 0)))
            out_shape.append(jax.ShapeDtypeStruct((nj, d, tf), BF16))
            args.append(src)
        rows = FFN_CAST_ROWS_OUT
        steps = nxt[2].shape[1] // rows
        cast_steps = max(cast_steps, steps)
        in_specs.append(pl.BlockSpec((None, rows, d), lambda i, j, steps=steps: (nl, step_of(i, j, steps), 0)))
        out_specs.append(pl.BlockSpec((rows, d), lambda i, j, steps=steps: (step_of(i, j, steps), 0)))
        out_shape.append(jax.ShapeDtypeStruct(nxt[2].shape[1:], BF16))
        args.append(nxt[2])
        assert n_steps >= cast_steps, "not enough grid steps to cast the next weights"
    outs = pl.pallas_call(
        functools.partial(_ffn_body, cast_steps=cast_steps),
        grid=(n // tm, nj),
        in_specs=in_specs,
        out_specs=out_specs,
        out_shape=out_shape,
        scratch_shapes=[pltpu.VMEM((tm, d), BF16)],
        compiler_params=_cparams(2),
        name="ffn",
    )(*args)
    return outs[0], tuple(outs[1:])


def _cast_layer_body(src_ref, dst_ref):
    dst_ref[...] = src_ref[...].astype(dst_ref.dtype)


def _cast_layer(w, layer, tile_cols=None, rows=256):
    _, r, c = w.shape
    if tile_cols is None:
        grid = (r // rows, 1)
        block, out_block = (None, rows, c), (rows, c)
        out_index = lambda i, j: (i, 0)
        out_shape = (r, c)
    else:
        grid = (r // rows, c // tile_cols)
        block, out_block = (None, rows, tile_cols), (None, rows, tile_cols)
        out_index = lambda i, j: (j, i, 0)
        out_shape = (c // tile_cols, r, tile_cols)
    return pl.pallas_call(
        _cast_layer_body,
        grid=grid,
        in_specs=[pl.BlockSpec(block, lambda i, j: (layer, i, j))],
        out_specs=pl.BlockSpec(out_block, out_index),
        out_shape=jax.ShapeDtypeStruct(out_shape, BF16),
        compiler_params=_cparams(2),
        name="cast_layer",
    )(w)


def _rope(y, cos, s1, s2):
    return y * cos + pltpu.roll(y, HEAD_DIM - ROPE_HALF, 1) * s1 + pltpu.roll(y, ROPE_HALF, 1) * s2


def _proj_body(x_ref, g_ref, w_ref, ws_ref, gain_ref, cos_ref, s1_ref, s2_ref, bias_ref,
               p_ref, kc_ref, sm_ref, h_ref):
    j = pl.program_id(1)
    gain = gain_ref[...]
    tm = h_ref.shape[0]

    def normalize_rows(rs):
        x = x_ref[rs, :]
        h = (_row_rms(x, x.shape[-1]) * g_ref[...]).astype(BF16)
        h_ref[rs, :] = h
        y = _dot(h, ws_ref[...])
        lane = lax.broadcasted_iota(jnp.int32, y.shape, 1)
        z = y + bias_ref[...]
        logsig = jnp.minimum(z, 0.0) - jnp.log1p(jnp.exp(-jnp.abs(z)))
        small = jnp.where(lane < LANE_LOGF, jax.nn.sigmoid(y), logsig)
        sm_ref[0, rs, :] = small
        sm_ref[1, rs, :] = pltpu.roll(small, HEAD_DIM - 3 * GQA_GROUP, 1)
        return h

    def raw(y, hh, rs):
        return y

    def normed(y, hh, rs):
        return _row_rms(y, HEAD_DIM) * gain[:, hh * HEAD_DIM:(hh + 1) * HEAD_DIM]

    def roped(y, hh, rs):
        return _rope(normed(y, hh, rs), cos_ref[rs, :], s1_ref[rs, :], s2_ref[rs, :])

    def emit(fns, keep_f32=False, first=False):
        chunks = 1 if all(fn is raw for fn in fns) and not first else PROJ_ROW_CHUNKS
        cm = tm // chunks
        for c in range(chunks):
            rs = slice(c * cm, (c + 1) * cm)
            h = normalize_rows(rs) if first else h_ref[rs, :]
            acc = _dot(h, w_ref[...])
            for hh, fn in enumerate(fns):
                y = acc[:, hh * HEAD_DIM:(hh + 1) * HEAD_DIM]
                p_ref[hh, rs, :] = fn(y, hh, rs).astype(p_ref.dtype)
                if keep_f32:
                    kc_ref[hh, rs, :] = y

    @pl.when(j == 0)
    def _():
        emit([normed] * 4, first=True)

    @pl.when((j >= 1) & (j < 4))
    def _():
        emit([normed] * 4)

    @pl.when((j >= 4) & (j < TILE_Q))
    def _():
        emit([raw] * 4)

    @pl.when((j >= TILE_Q) & (j < TILE_CMP_KV))
    def _():
        emit([roped] * 4)

    @pl.when(j == TILE_CMP_KV)
    def _():
        emit([raw] * 4, keep_f32=True)

    @pl.when(j > TILE_CMP_KV)
    def _():
        emit([roped, roped, raw, raw])


def _proj(x, g, w_main, w_small, gains, cos, s1, s2, bias, layer, seq, tm=1024):
    n, d = x.shape
    tm = min(tm, seq)
    tpb = seq // tm
    hp = HEADS_PER_TILE
    return pl.pallas_call(
        _proj_body,
        grid=(n // tm, N_PROJ_TILES),
        in_specs=[
            pl.BlockSpec((tm, d), lambda i, j: (i, 0)),
            pl.BlockSpec((None, 1, d), lambda i, j: (layer, 0, 0)),
            pl.BlockSpec((None, d, PROJ_TN), lambda i, j: (layer, 0, j)),
            pl.BlockSpec((None, d, HEAD_DIM), lambda i, j: (layer, 0, 0)),
            pl.BlockSpec((None, None, 1, PROJ_TN), lambda i, j: (layer, j, 0, 0)),
            pl.BlockSpec((tm, HEAD_DIM), lambda i, j: (i % tpb, 0)),
            pl.BlockSpec((tm, HEAD_DIM), lambda i, j: (i % tpb, 0)),
            pl.BlockSpec((tm, HEAD_DIM), lambda i, j: (i % tpb, 0)),
            pl.BlockSpec((None, 1, HEAD_DIM), lambda i, j: (layer, 0, 0)),
        ],
        out_specs=[
            pl.BlockSpec((hp, tm, HEAD_DIM), lambda i, j: (j, i, 0)),
            pl.BlockSpec((hp, tm, HEAD_DIM), lambda i, j: (0, i, 0)),
            pl.BlockSpec((N_KV_NSA, tm, HEAD_DIM), lambda i, j: (0, i, 0)),
        ],
        out_shape=[
            jax.ShapeDtypeStruct((N_SLABS, n, HEAD_DIM), BF16),
            jax.ShapeDtypeStruct((hp, n, HEAD_DIM), F32),
            jax.ShapeDtypeStruct((N_KV_NSA, n, HEAD_DIM), F32),
        ],
        scratch_shapes=[pltpu.VMEM((tm, d), BF16)],
        compiler_params=_cparams(2),
        name="proj",
    )(x, g, w_main, w_small, gains, cos, s1, s2, bias)


def _cumsum_body(x_ref, ct_ref):
    x = x_ref[0, 0]
    t = x.shape[0]
    row = lax.broadcasted_iota(jnp.int32, x.shape, 0)
    s = 1
    while s < t:
        x = x + jnp.where(row >= s, pltpu.roll(x, s, 0), 0.0)
        s *= 2
    ct_ref[0] = x.T[LANE_LOGF:LANE_LOGF + N_HEADS_FOX, :] * LOG2E


def _cumsum(sm4):
    _, b, t, _ = sm4.shape
    return pl.pallas_call(
        _cumsum_body,
        grid=(b,),
        in_specs=[pl.BlockSpec((1, 1, t, HEAD_DIM), lambda i: (0, i, 0, 0))],
        out_specs=pl.BlockSpec((1, N_HEADS_FOX, t), lambda i: (i, 0, 0)),
        out_shape=jax.ShapeDtypeStruct((b, N_HEADS_FOX, t), F32),
        compiler_params=_cparams(1),
        name="fox_cumsum",
    )(sm4)


def _gelu_tanh(x):
    c = float(np.sqrt(2.0 / np.pi))
    return x * (0.5 * (1.0 + jnp.tanh(c * (x + 0.044715 * (x * x * x)))))


def _compress_body(k_ref, pos_ref, w1_ref, w2_ref, kn_ref, cos_ref, s1_ref, s2_ref, o_ref):
    is_key = pl.program_id(0) == 0
    nk = k_ref.shape[0]
    nr = o_ref.shape[-2]
    half = CMP_STRIDE * HEAD_DIM
    a = jnp.zeros((nk * nr, CMP_HIDDEN), F32)
    bm = jnp.zeros((nk * nr, CMP_HIDDEN), F32)
    for l in range(CMP_STRIDE):
        tok = jnp.concatenate([k_ref[kh, 0, pl.ds(l, nr, stride=CMP_STRIDE), :] for kh in range(nk)], axis=0)
        sl = slice(l * HEAD_DIM, (l + 1) * HEAD_DIM)
        a = a + _dot((tok + pos_ref[0][:, sl]).astype(BF16), w1_ref[sl, :])
        bm = bm + _dot((tok + pos_ref[1][:, sl]).astype(BF16), w1_ref[half + l * HEAD_DIM:half + (l + 1) * HEAD_DIM, :])
    nxt = jnp.concatenate([pltpu.roll(bm[kh * nr:(kh + 1) * nr], nr - 1, 0) for kh in range(nk)], axis=0)
    y = _dot(_gelu_tanh(a + nxt).astype(BF16), w2_ref[...])

    @pl.when(is_key)
    def _():
        yk = _row_rms(y, HEAD_DIM) * kn_ref[...]
        for kh in range(nk):
            rs = slice(kh * nr, (kh + 1) * nr)
            o_ref[0, 0, kh] = _rope(yk[rs], cos_ref[...], s1_ref[...], s2_ref[...]).astype(o_ref.dtype)

    @pl.when(jnp.logical_not(is_key))
    def _():
        for kh in range(nk):
            o_ref[0, 0, kh] = y[kh * nr:(kh + 1) * nr].astype(o_ref.dtype)


def _compress(kr, pos, w1, w2, knorm0, cos_c, s1_c, s2_c, layer):
    _, b, t, _ = kr.shape
    nr = t // CMP_STRIDE
    width = CMP_STRIDE * HEAD_DIM
    nk = N_KV_NSA
    return pl.pallas_call(
        _compress_body,
        grid=(2, b),
        in_specs=[
            pl.BlockSpec((nk, 1, t, HEAD_DIM), lambda ty, i: (ty, i, 0, 0)),
            pl.BlockSpec((None, None, 2, 1, width), lambda ty, i: (layer, ty, 0, 0, 0)),
            pl.BlockSpec((None, None, 2 * width, CMP_HIDDEN), lambda ty, i: (layer, ty, 0, 0)),
            pl.BlockSpec((None, None, CMP_HIDDEN, HEAD_DIM), lambda ty, i: (layer, ty, 0, 0)),
            pl.BlockSpec((None, 1, HEAD_DIM), lambda ty, i: (layer, 0, 0)),
            pl.BlockSpec((nr, HEAD_DIM), lambda ty, i: (0, 0)),
            pl.BlockSpec((nr, HEAD_DIM), lambda ty, i: (0, 0)),
            pl.BlockSpec((nr, HEAD_DIM), lambda ty, i: (0, 0)),
        ],
        out_specs=pl.BlockSpec((1, 1, nk, nr, HEAD_DIM), lambda ty, i: (ty, i, 0, 0, 0)),
        out_shape=jax.ShapeDtypeStruct((2, b, nk, nr, HEAD_DIM), BF16),
        compiler_params=_cparams(2),
        name="compress",
    )(kr, pos, w1, w2, knorm0, cos_c, s1_c, s2_c)


def _split3(x):
    hi = x.astype(BF16)
    r1 = x - hi.astype(F32)
    mid = r1.astype(BF16)
    lo = (r1 - mid.astype(F32)).astype(BF16)
    return hi, mid, lo


def _cmp_branch(q, kc, vc, ovt, i, tq, n_blk):
    rows = q.shape[0]
    g = rows // tq
    s = _dot_nt(q, kc)
    row_g = lax.broadcasted_iota(jnp.int32, (rows, HEAD_DIM), 0)
    lane_g = lax.broadcasted_iota(jnp.int32, (rows, HEAD_DIM), 1)
    t_g = i * tq + (row_g & (tq - 1))
    valid = (lane_g * CMP_STRIDE + (CMP_BLOCK - 1)) <= t_g
    s = jnp.where(valid, s, NEG_INF)
    m = jnp.max(s, axis=-1, keepdims=True)
    p = jnp.where(valid, jnp.exp2(s - m), 0.0)
    p = p / jnp.maximum(jnp.sum(p, axis=-1, keepdims=True), 1e-30)
    o = _dot(p.astype(BF16), vc)
    psum = p[0:tq]
    for gg in range(1, g):
        psum = psum + p[gg * tq:(gg + 1) * tq]

    hi, mid, lo = _split3(psum)
    imp = ((_dot_nt(ovt, hi) + _dot_nt(ovt, mid)) + _dot_nt(ovt, lo))[:n_blk]
    blk = lax.broadcasted_iota(jnp.int32, (n_blk, tq), 0)
    t_blk = (i * tq + lax.broadcasted_iota(jnp.int32, (n_blk, tq), 1)) // SLC_BLOCK
    causal = blk <= t_blk
    forced = (blk == 0) | (causal & (blk > t_blk - N_LOCAL_SLC))
    score = jnp.where(forced, FORCED_SCORE, jnp.where(causal, imp, NEG_INF))
    cnt = jnp.zeros((n_blk, tq), jnp.int32)
    for c in range(n_blk):
        other = score[c:c + 1, :]
        ahead = (other > score) | ((other == score) & (blk > c))
        cnt = cnt + ahead.astype(jnp.int32)
    picked = jnp.where(cnt < SLC_TOPK, 1.0, 0.0)
    picked = jnp.concatenate([picked, jnp.zeros((HEAD_DIM - n_blk, tq), F32)], axis=0)
    return o, picked.T.astype(BF16)


def _online_update(s, v, m_ref, l_ref, acc_ref, rows=slice(None)):
    cols = [s[:, c:c + HEAD_DIM] for c in range(0, s.shape[1], HEAD_DIM)]
    m_el = functools.reduce(jnp.maximum, cols)
    m_prev = m_ref[rows]
    m_new = jnp.maximum(m_prev, jnp.max(m_el, axis=-1, keepdims=True))
    alpha = jnp.exp2(m_prev - m_new)
    ps = [jnp.exp2(c - m_new) for c in cols]
    l_ref[rows] = alpha * l_ref[rows] + functools.reduce(jnp.add, ps)
    p = jnp.concatenate([x.astype(BF16) for x in ps], axis=1)
    acc_ref[rows] = alpha * acc_ref[rows] + _dot(p, v)
    m_ref[rows] = m_new


def _normalize(acc_ref, l_ref):
    l = jnp.sum(l_ref[...], axis=-1, keepdims=True)
    return acc_ref[...] / jnp.maximum(l, 1e-30)


def _init_online(m_ref, l_ref, acc_ref):
    m_ref[...] = jnp.full(m_ref.shape, NEG_INF, F32)
    l_ref[...] = jnp.zeros(l_ref.shape, F32)
    acc_ref[...] = jnp.zeros(acc_ref.shape, F32)


def _flash_scratch(rows):
    return [pltpu.VMEM((rows, HEAD_DIM), F32)] * 3


def _win_branch(q, k_ref, v_ref, kh, i, tq, band):
    rows = q.shape[0]
    g = rows // tq
    start = pl.multiple_of(jnp.maximum(i * tq - WINDOW, 0), tq)
    k = k_ref[kh, 0, pl.ds(start, band), :]
    v = v_ref[kh, 0, pl.ds(start, band), :]
    row = lax.broadcasted_iota(jnp.int32, (tq, band), 0)
    col = lax.broadcasted_iota(jnp.int32, (tq, band), 1)
    kpos = start + col
    tpos = i * tq + row
    live = (kpos <= tpos) & (kpos > tpos - WINDOW)
    bias = jnp.where(live, 0.0, NEG_INF)
    s = (_dot_nt(q, k).reshape(g, tq, band) + bias[None]).reshape(rows, band)
    m = jnp.max(s, axis=-1, keepdims=True)
    p = jnp.exp2(s - m)
    l = jnp.sum(p, axis=-1, keepdims=True)
    return _dot(p.astype(BF16), v) / jnp.maximum(l, 1e-30)


def _nsa_body(q_ref, kc_ref, vc_ref, ks_ref, vs_ref, kw_ref, vw_ref, gt_ref, ov_ref, e_ref, o_ref,
              m_ref, l_ref, acc_ref, part_ref, sel_ref, *, tq, tk, band, n_blk):
    i = pl.program_id(1)
    g = GQA_GROUP
    rows = g * tq

    def queries(kh):
        return q_ref[kh * g:(kh + 1) * g].reshape(rows, HEAD_DIM)

    def gate(kh, branch):
        gt = gt_ref[kh, 0]
        return jnp.concatenate(
            [jnp.broadcast_to(gt[:, 3 * gg + branch:3 * gg + branch + 1], (tq, HEAD_DIM)) for gg in range(g)],
            axis=0)

    for kh in range(N_KV_NSA):
        rs = slice(kh * rows, (kh + 1) * rows)
        q = queries(kh)
        o_cmp, sel = _cmp_branch(q, kc_ref[0, 0, kh], vc_ref[0, 0, kh], ov_ref[...], i, tq, n_blk)
        sel_ref[kh] = sel
        part_ref[rs] = gate(kh, 0) * o_cmp + gate(kh, 2) * _win_branch(q, kw_ref, vw_ref, kh, i, tq, band)

    _init_online(m_ref, l_ref, acc_ref)
    row = lax.broadcasted_iota(jnp.int32, (tq, tk), 0)
    col = lax.broadcasted_iota(jnp.int32, (tq, tk), 1)
    last = (i * tq) // tk

    def step(kc, diagonal):
        start = pl.multiple_of(kc * tk, tk)
        for kh in range(N_KV_NSA):
            k = ks_ref[kh, 0, pl.ds(start, tk), :]
            v = vs_ref[kh, 0, pl.ds(start, tk), :]
            live = _dot(sel_ref[kh], e_ref[kc]) > 0.5
            if diagonal:
                live = live & ((start + col) <= (i * tq + row))
            bias = jnp.where(live, 0.0, NEG_INF)
            s = _dot_nt(queries(kh), k).reshape(g, tq, tk) + bias[None]
            _online_update(s.reshape(rows, tk), v, m_ref, l_ref, acc_ref, slice(kh * rows, (kh + 1) * rows))

    def body(kc, carry):
        step(kc, False)
        return carry

    lax.fori_loop(0, last, body, 0)
    step(last, True)
    o_slc = _normalize(acc_ref, l_ref)
    for kh in range(N_KV_NSA):
        rs = slice(kh * rows, (kh + 1) * rows)
        y = part_ref[rs] + gate(kh, 1) * o_slc[rs]
        for gg in range(g):
            hq = kh * g + gg
            o_ref[0, :, hq * HEAD_DIM:(hq + 1) * HEAD_DIM] = y[gg * tq:(gg + 1) * tq]


def _nsa(p4, kvc, sm4, ovt, e, tq=256):
    _, b, t, _ = p4.shape
    nr = kvc.shape[3]
    tk = e.shape[-1]
    band = min(WINDOW + tq, t)
    nk = N_KV_NSA
    rows = N_HEADS_NSA * tq

    def kv(slab):
        return pl.BlockSpec((nk, 1, t, HEAD_DIM), lambda bb, i: (slab // nk, bb, 0, 0))

    return pl.pallas_call(
        functools.partial(_nsa_body, tq=tq, tk=tk, band=band, n_blk=t // SLC_BLOCK),
        grid=(b, t // tq),
        in_specs=[
            pl.BlockSpec((N_HEADS_NSA, None, tq, HEAD_DIM), lambda bb, i: (SLAB_Q // N_HEADS_NSA, bb, i, 0)),
            pl.BlockSpec((1, 1, nk, nr, HEAD_DIM), lambda bb, i: (0, bb, 0, 0, 0)),
            pl.BlockSpec((1, 1, nk, nr, HEAD_DIM), lambda bb, i: (1, bb, 0, 0, 0)),
            kv(SLAB_KV + 4), kv(SLAB_KV + 6), kv(SLAB_KV + 8), kv(SLAB_KV + 10),
            pl.BlockSpec((nk, 1, tq, HEAD_DIM), lambda bb, i: (0, bb, i, 0)),
            pl.BlockSpec((HEAD_DIM, HEAD_DIM), lambda bb, i: (0, 0)),
            pl.BlockSpec(e.shape, lambda bb, i: (0, 0, 0)),
        ],
        out_specs=pl.BlockSpec((1, tq, NSA_WIDTH), lambda bb, i: (bb, i, 0)),
        out_shape=jax.ShapeDtypeStruct((b, t, NSA_WIDTH), F32),
        scratch_shapes=_flash_scratch(rows) + [pltpu.VMEM((rows, HEAD_DIM), F32),
                                               pltpu.VMEM((nk, tq, HEAD_DIM), BF16)],
        compiler_params=_cparams(2),
        name="nsa",
    )(p4, kvc, kvc, p4, p4, p4, p4, sm4, ovt, e)


def _fox_body(q_ref, k_ref, v_ref, ct_ref, o_ref, m_ref, l_ref, acc_ref, *, tq):
    i = pl.program_id(2)
    nh = q_ref.shape[0]
    _init_online(m_ref, l_ref, acc_ref)
    row = lax.broadcasted_iota(jnp.int32, (tq, tq), 0)
    col = lax.broadcasted_iota(jnp.int32, (tq, tq), 1)

    def step(kc, diagonal):
        start = pl.multiple_of(kc * tq, tq)
        for hh in range(nh):
            k = k_ref[hh, 0, pl.ds(start, tq), :]
            v = v_ref[hh, 0, pl.ds(start, tq), :]
            s = _dot_nt(q_ref[hh], k) - ct_ref[hh, pl.ds(kc, 1), :]
            if diagonal:
                s = jnp.where(col <= row, s, NEG_INF)
            _online_update(s, v, m_ref, l_ref, acc_ref, slice(hh * tq, (hh + 1) * tq))

    def body(kc, carry):
        step(kc, False)
        return carry

    lax.fori_loop(0, i, body, 0)
    step(i, True)
    o = _normalize(acc_ref, l_ref)
    for hh in range(nh):
        o_ref[0, :, hh * HEAD_DIM:(hh + 1) * HEAD_DIM] = o[hh * tq:(hh + 1) * tq]


FOX_HEADS_PER_STEP = 8
assert SLAB_FOX % FOX_HEADS_PER_STEP == 0 and N_HEADS_FOX % FOX_HEADS_PER_STEP == 0


def _fox_attn(p4, ctr, tq=512):
    _, b, t, _ = p4.shape
    nh = N_HEADS_FOX
    hs = FOX_HEADS_PER_STEP
    q0, k0, v0 = (SLAB_FOX // hs, (SLAB_FOX + nh) // hs, (SLAB_FOX + 2 * nh) // hs)
    return pl.pallas_call(
        functools.partial(_fox_body, tq=tq),
        grid=(b, nh // hs, t // tq),
        in_specs=[
            pl.BlockSpec((hs, None, tq, HEAD_DIM), lambda bb, h, i: (q0 + h, bb, i, 0)),
            pl.BlockSpec((hs, 1, t, HEAD_DIM), lambda bb, h, i: (k0 + h, bb, 0, 0)),
            pl.BlockSpec((hs, 1, t, HEAD_DIM), lambda bb, h, i: (v0 + h, bb, 0, 0)),
            pl.BlockSpec((hs, t // tq, tq), lambda bb, h, i: (bb * (nh // hs) + h, 0, 0)),
        ],
        out_specs=pl.BlockSpec((1, tq, hs * HEAD_DIM), lambda bb, h, i: (bb, i, h)),
        out_shape=jax.ShapeDtypeStruct((b, t, FOX_WIDTH), F32),
        scratch_shapes=_flash_scratch(hs * tq),
        compiler_params=_cparams(3),
        name="fox_attn",
    )(p4, p4, p4, ctr)


def _out_body(on_ref, of_ref, x_ref, nn_ref, fn_ref, w_ref, o_ref):
    nsa = (_row_rms(on_ref[...], NSA_WIDTH) * nn_ref[...]).astype(BF16)
    fox = (_row_rms(of_ref[...], FOX_WIDTH) * fn_ref[...]).astype(BF16)
    o_ref[...] = x_ref[...] + (_dot(nsa, w_ref[:NSA_WIDTH, :]) + _dot(fox, w_ref[NSA_WIDTH:, :]))


def _out(on, of, x, nn, fn, w, layer, tm=512):
    n, d = x.shape
    return pl.pallas_call(
        _out_body,
        grid=(n // tm,),
        in_specs=[
            pl.BlockSpec((tm, NSA_WIDTH), lambda i: (i, 0)),
            pl.BlockSpec((tm, FOX_WIDTH), lambda i: (i, 0)),
            pl.BlockSpec((tm, d), lambda i: (i, 0)),
            pl.BlockSpec((None, 1, NSA_WIDTH), lambda i: (layer, 0, 0)),
            pl.BlockSpec((None, 1, FOX_WIDTH), lambda i: (layer, 0, 0)),
            pl.BlockSpec((None, NSA_WIDTH + FOX_WIDTH, d), lambda i: (layer, 0, 0)),
        ],
        out_specs=pl.BlockSpec((tm, d), lambda i: (i, 0)),
        out_shape=jax.ShapeDtypeStruct((n, d), F32),
        compiler_params=_cparams(1),
        name="out_proj",
    )(on, of, x, nn, fn, w)


def _rope_tables(pos):
    inv = ROPE_THETA ** (-jnp.arange(0, ROPE_DIM, 2, dtype=F32) / ROPE_DIM)
    ang = pos.astype(F32)[:, None] * inv[None, :]
    cos, sin = jnp.cos(ang), jnp.sin(ang)
    n = pos.shape[0]
    pad = HEAD_DIM - ROPE_DIM
    cos_t = jnp.concatenate([cos, cos, jnp.ones((n, pad), F32)], axis=-1)
    s1 = jnp.concatenate([-sin, jnp.zeros((n, HEAD_DIM - ROPE_HALF), F32)], axis=-1)
    s2 = jnp.concatenate([jnp.zeros((n, ROPE_HALF), F32), sin, jnp.zeros((n, pad), F32)], axis=-1)
    return cos_t, s1, s2


def _overlap_matrix(t):
    n_cmp = (t - CMP_BLOCK) // CMP_STRIDE + 1
    n_slc = t // SLC_BLOCK
    cmp_start = np.arange(HEAD_DIM) * CMP_STRIDE
    slc_start = np.arange(HEAD_DIM) * SLC_BLOCK
    ov = ((cmp_start[:, None] < slc_start[None, :] + SLC_BLOCK)
          & (cmp_start[:, None] + CMP_BLOCK > slc_start[None, :]))
    ov = ov & (np.arange(HEAD_DIM)[:, None] < n_cmp) & (np.arange(HEAD_DIM)[None, :] < n_slc)
    return jnp.asarray(ov.T.astype(np.float32), BF16)


def _expand_matrix(t, tk):
    kpos = np.arange(t).reshape(t // tk, 1, tk)
    j = np.arange(HEAD_DIM).reshape(1, HEAD_DIM, 1)
    return jnp.asarray((kpos // SLC_BLOCK == j).astype(np.float32), BF16)


def _mixer(x2, b, t, layer, wts, tabs):
    n = x2.shape[0]
    p3, kc32, sm = _proj(x2, wts["mix_norm"], wts["w_main"], wts["w_small"], wts["gains"],
                         tabs["cos"], tabs["s1"], tabs["s2"], wts["fbias"], layer, t)
    p4 = p3.reshape(N_SLABS, b, t, HEAD_DIM)
    kr = kc32.reshape(HEADS_PER_TILE, b, t, HEAD_DIM)
    sm4 = sm.reshape(N_KV_NSA, b, t, HEAD_DIM)
    ct = _cumsum(sm4)
    fox_tq = min(512, t)
    ctr = ct.reshape(b * N_HEADS_FOX, t // fox_tq, fox_tq)
    kvc = _compress(kr, wts["pos"], wts["cmp_w1"], wts["cmp_w2"], wts["knorm0"],
                    tabs["cos_c"], tabs["s1_c"], tabs["s2_c"], layer)
    o_nsa = _nsa(p4, kvc, sm4, tabs["ov"], tabs["e"])
    o_fox = _fox_attn(p4, ctr, fox_tq)
    return _out(o_nsa.reshape(n, NSA_WIDTH), o_fox.reshape(n, FOX_WIDTH), x2, wts["nsa_out_norm"],
                wts["fox_out_norm"], wts["w_out"], layer)


def _prep_weights(ffn1_norm, ffn1_w_gate, ffn1_w_up, ffn1_w_down, mix_norm, w_in, nsa_q_norm, nsa_k_norm,
                  cmp_pos_emb, cmp_w1, cmp_w2, nsa_out_norm, fox_q_norm, fox_k_norm, fox_forget_bias,
                  fox_out_norm, w_out, ffn2_norm, ffn2_w_gate, ffn2_w_up, ffn2_w_down):
    depth, d, _ = w_in.shape
    c0 = COLS_NSA_Q + COLS_NSA_KV
    c1 = c0 + COLS_NSA_GATE
    c2 = c1 + COLS_FOX_QKV
    w16 = w_in.astype(BF16)
    w_small = jnp.concatenate(
        [w16[:, :, c0:c1], w16[:, :, c2:], jnp.zeros((depth, d, HEAD_DIM - COLS_NSA_GATE - COLS_FOX_F), BF16)],
        axis=-1)
    w_main = jnp.concatenate([w16[:, :, c1:c2], w16[:, :, :c0]], axis=-1)

    def tile(v):
        return jnp.concatenate([v] * HEADS_PER_TILE, axis=-1)

    one = jnp.ones((depth, HEAD_DIM), F32)
    gains = [tile(one)] * N_PROJ_TILES
    gains[0] = gains[1] = tile(fox_q_norm * Q_SCALE)
    gains[2] = gains[3] = tile(fox_k_norm)
    gains[TILE_Q] = gains[TILE_Q + 1] = tile(nsa_q_norm * Q_SCALE)
    gains[TILE_CMP_KV + 1] = jnp.concatenate([nsa_k_norm[:, 1]] * 2 + [one] * 2, axis=-1)
    gains[TILE_CMP_KV + 2] = jnp.concatenate([nsa_k_norm[:, 2]] * 2 + [one] * 2, axis=-1)
    gains = jnp.stack(gains, axis=1)[:, :, None, :]

    fbias = jnp.zeros((depth, 1, HEAD_DIM), F32).at[:, 0, LANE_LOGF:LANE_LOGF + N_HEADS_FOX].set(
        fox_forget_bias)
    half = CMP_BLOCK * HEAD_DIM // 2
    pos = cmp_pos_emb.reshape(depth, 2, 2, 1, half)
    return dict(
        ffn1_norm=ffn1_norm[:, None, :], ffn2_norm=ffn2_norm[:, None, :],
        ffn1=(ffn1_w_gate, ffn1_w_up, ffn1_w_down), ffn2=(ffn2_w_gate, ffn2_w_up, ffn2_w_down),
        mix_norm=mix_norm[:, None, :], w_main=w_main, w_small=w_small, gains=gains, fbias=fbias, pos=pos,
        cmp_w1=cmp_w1.astype(BF16), cmp_w2=cmp_w2.astype(BF16), knorm0=nsa_k_norm[:, 0][:, None, :],
        nsa_out_norm=nsa_out_norm[:, None, :], fox_out_norm=fox_out_norm[:, None, :],
        w_out=w_out.astype(BF16),
    )


def _tables(t):
    cos, s1, s2 = _rope_tables(jnp.arange(t))
    cmp_end = jnp.arange(t // CMP_STRIDE) * CMP_STRIDE + (CMP_BLOCK - 1)
    cos_c, s1_c, s2_c = _rope_tables(cmp_end)
    return dict(cos=cos, s1=s1, s2=s2, cos_c=cos_c, s1_c=s1_c, s2_c=s2_c,
                ov=_overlap_matrix(t), e=_expand_matrix(t, min(512, t)))


def kernel(x, ffn1_norm, ffn1_w_gate, ffn1_w_up, ffn1_w_down, mix_norm, w_in, nsa_q_norm, nsa_k_norm, cmp_pos_emb, cmp_w1, cmp_w2, nsa_out_norm, fox_q_norm, fox_k_norm, fox_forget_bias, fox_out_norm, w_out, ffn2_norm, ffn2_w_gate, ffn2_w_up, ffn2_w_down):
    b, t, d = x.shape
    depth = w_in.shape[0]
    wts = _prep_weights(ffn1_norm, ffn1_w_gate, ffn1_w_up, ffn1_w_down, mix_norm, w_in, nsa_q_norm,
                        nsa_k_norm, cmp_pos_emb, cmp_w1, cmp_w2, nsa_out_norm, fox_q_norm, fox_k_norm,
                        fox_forget_bias, fox_out_norm, w_out, ffn2_norm, ffn2_w_gate, ffn2_w_up, ffn2_w_down)
    tabs = _tables(t)
    x2 = x.reshape(b * t, d)
    wg0, wu0, wd0 = wts["ffn1"]
    w16 = (_cast_layer(wg0, 0, FFN_TF), _cast_layer(wu0, 0, FFN_TF), _cast_layer(wd0, 0))
    for layer in range(depth):
        x2, w16 = _ffn(x2, wts["ffn1_norm"], *w16, layer, nxt=wts["ffn2"] + (layer,))
        x2 = _mixer(x2, b, t, layer, wts, tabs)
        nxt = wts["ffn1"] + (layer + 1,) if layer + 1 < depth else None
        x2, w16 = _ffn(x2, wts["ffn2_norm"], *w16, layer, nxt=nxt)
    return x2.reshape(b, t, d)
```

```python
import functools

import numpy as np
import jax
import jax.numpy as jnp
from jax import lax
from jax.experimental import pallas as pl
from jax.experimental.pallas import tpu as pltpu

F32 = jnp.float32
BF16 = jnp.bfloat16

HEAD_DIM = 128
N_HEADS_NSA = 8
N_KV_NSA = 2
GQA_GROUP = 4
N_HEADS_FOX = 8
NSA_WIDTH = N_HEADS_NSA * HEAD_DIM
FOX_WIDTH = N_HEADS_FOX * HEAD_DIM
ROPE_DIM = 32
ROPE_HALF = ROPE_DIM // 2
ROPE_THETA = 500000.0
CMP_BLOCK = 32
CMP_STRIDE = 16
CMP_HIDDEN = 256
SLC_BLOCK = 64
SLC_TOPK = 8
N_LOCAL_SLC = 2
WINDOW = 512
EPS = 1e-6
NEG_INF = -1e30
FORCED_SCORE = 1e9
SCALE = HEAD_DIM ** -0.5
LOG2E = float(np.log2(np.e))
Q_SCALE = SCALE * LOG2E

COLS_NSA_Q = NSA_WIDTH
COLS_NSA_KV = 3 * 2 * N_KV_NSA * HEAD_DIM
COLS_NSA_GATE = 3 * N_HEADS_NSA
COLS_FOX_QKV = 3 * FOX_WIDTH
COLS_FOX_F = N_HEADS_FOX
COLS_MAIN = COLS_NSA_Q + COLS_NSA_KV + COLS_FOX_QKV

HEADS_PER_TILE = 4
PROJ_TN = HEADS_PER_TILE * HEAD_DIM
N_PROJ_TILES = COLS_MAIN // PROJ_TN
N_SLABS = COLS_MAIN // HEAD_DIM
SLAB_FOX = 0
SLAB_Q = 24
SLAB_KV = 32
TILE_Q = SLAB_Q // HEADS_PER_TILE
TILE_CMP_KV = SLAB_KV // HEADS_PER_TILE
PROJ_ROW_CHUNKS = 8
FFN_ROW_CHUNKS = 2
LANE_LOGF = COLS_NSA_GATE

VMEM_LIMIT_BYTES = 56 * 1024 * 1024


def _cparams(n_axes):
    return pltpu.CompilerParams(dimension_semantics=("arbitrary",) * n_axes,
                                vmem_limit_bytes=VMEM_LIMIT_BYTES)


def _dot(a, b):
    return jnp.dot(a, b, preferred_element_type=F32)


def _dot_nt(a, b):
    return lax.dot_general(a, b, (((1,), (1,)), ((), ())), preferred_element_type=F32)


def _row_rms(x, width):
    return x * lax.rsqrt(jnp.sum(x * x, axis=-1, keepdims=True) * (1.0 / width) + EPS)


def _ffn_body(x_ref, g_ref, wg_ref, wu_ref, wd_ref, *rest, cast_steps):
    j = pl.program_id(1)
    if cast_steps:
        *srcs, o_ref, ng_ref, nu_ref, nd_ref, h_ref = rest

        @pl.when(pl.program_id(0) * pl.num_programs(1) + j < cast_steps)
        def _():
            for src, dst in zip(srcs, (ng_ref, nu_ref, nd_ref)):
                dst[...] = src[...].astype(dst.dtype)
    else:
        o_ref, h_ref = rest

    def step(first):
        chunks = FFN_ROW_CHUNKS if first else 1
        cm = o_ref.shape[0] // chunks
        for c in range(chunks):
            rs = slice(c * cm, (c + 1) * cm)
            if first:
                base = x_ref[rs, :]
                h = (_row_rms(base, base.shape[-1]) * g_ref[...]).astype(BF16)
                h_ref[rs, :] = h
            else:
                base = o_ref[rs, :]
                h = h_ref[rs, :]
            a = _dot(h, wg_ref[...])
            b = _dot(h, wu_ref[...])
            u = (0.5 * a * jax.nn.sigmoid(a)) * b
            o_ref[rs, :] = base + _dot(u.astype(BF16), wd_ref[...])

    @pl.when(j == 0)
    def _():
        step(True)

    @pl.when(j > 0)
    def _():
        step(False)


FFN_CAST_ROWS = 16
FFN_CAST_COLS = 2816


def _ffn(x, g, wg, wu, wd, layer, nxt=None, tm=1024, tf=512):
    n, d = x.shape
    f = wg.shape[-1]
    nj = f // tf
    in_specs = [
        pl.BlockSpec((tm, d), lambda i, j: (i, 0)),
        pl.BlockSpec((None, 1, d), lambda i, j: (layer, 0, 0)),
        pl.BlockSpec((d, tf), lambda i, j: (0, j)),
        pl.BlockSpec((d, tf), lambda i, j: (0, j)),
        pl.BlockSpec((tf, d), lambda i, j: (j, 0)),
    ]
    out_specs = [pl.BlockSpec((tm, d), lambda i, j: (i, 0))]
    out_shape = [jax.ShapeDtypeStruct((n, d), F32)]
    args = [x, g, wg, wu, wd]
    cast_steps = 0
    if nxt is not None:
        nl = nxt[3]
        rows = FFN_CAST_ROWS
        for src in nxt[:3]:
            r, c = src.shape[1:]
            ncol = pl.cdiv(c, FFN_CAST_COLS)
            cb = c // ncol
            steps = (r // rows) * ncol
            cast_steps = max(cast_steps, steps)
            assert (n // tm) * nj >= steps, "not enough grid steps to cast the next weights"

            def slab(i, j, steps=steps, ncol=ncol):
                s = jnp.minimum(i * nj + j, steps - 1)
                return s // ncol, s % ncol

            in_specs.append(pl.BlockSpec((None, rows, cb), lambda i, j, slab=slab: (nl,) + slab(i, j)))
            out_specs.append(pl.BlockSpec((rows, cb), slab))
            out_shape.append(jax.ShapeDtypeStruct((r, c), BF16))
            args.append(src)
    outs = pl.pallas_call(
        functools.partial(_ffn_body, cast_steps=cast_steps),
        grid=(n // tm, nj),
        in_specs=in_specs,
        out_specs=out_specs,
        out_shape=out_shape,
        scratch_shapes=[pltpu.VMEM((tm, d), BF16)],
        compiler_params=_cparams(2),
        name="ffn",
    )(*args)
    return outs[0], tuple(outs[1:])


def _cast_layer_body(src_ref, dst_ref):
    dst_ref[...] = src_ref[...].astype(dst_ref.dtype)


def _cast_layer(w, layer, rows=256):
    _, r, c = w.shape
    return pl.pallas_call(
        _cast_layer_body,
        grid=(r // rows,),
        in_specs=[pl.BlockSpec((None, rows, c), lambda i: (layer, i, 0))],
        out_specs=pl.BlockSpec((rows, c), lambda i: (i, 0)),
        out_shape=jax.ShapeDtypeStruct((r, c), BF16),
        compiler_params=_cparams(1),
        name="cast_layer",
    )(w)


def _rope(y, cos, s1, s2):
    return y * cos + pltpu.roll(y, HEAD_DIM - ROPE_HALF, 1) * s1 + pltpu.roll(y, ROPE_HALF, 1) * s2


def _proj_body(x_ref, g_ref, w_ref, ws_ref, gain_ref, cos_ref, s1_ref, s2_ref, bias_ref,
               p_ref, kc_ref, sm_ref, h_ref):
    j = pl.program_id(1)
    gain = gain_ref[...]
    tm = h_ref.shape[0]

    def normalize_rows(rs):
        x = x_ref[rs, :]
        h = (_row_rms(x, x.shape[-1]) * g_ref[...]).astype(BF16)
        h_ref[rs, :] = h
        y = _dot(h, ws_ref[...])
        lane = lax.broadcasted_iota(jnp.int32, y.shape, 1)
        z = y + bias_ref[...]
        logsig = jnp.minimum(z, 0.0) - jnp.log1p(jnp.exp(-jnp.abs(z)))
        small = jnp.where(lane < LANE_LOGF, jax.nn.sigmoid(y), logsig)
        sm_ref[0, rs, :] = small
        sm_ref[1, rs, :] = pltpu.roll(small, HEAD_DIM - 3 * GQA_GROUP, 1)
        return h

    def raw(y, hh, rs):
        return y

    def normed(y, hh, rs):
        return _row_rms(y, HEAD_DIM) * gain[:, hh * HEAD_DIM:(hh + 1) * HEAD_DIM]

    def roped(y, hh, rs):
        return _rope(normed(y, hh, rs), cos_ref[rs, :], s1_ref[rs, :], s2_ref[rs, :])

    def emit(fns, keep_f32=False, first=False):
        chunks = 1 if all(fn is raw for fn in fns) and not first else PROJ_ROW_CHUNKS
        cm = tm // chunks
        for c in range(chunks):
            rs = slice(c * cm, (c + 1) * cm)
            h = normalize_rows(rs) if first else h_ref[rs, :]
            acc = _dot(h, w_ref[...])
            for hh, fn in enumerate(fns):
                y = acc[:, hh * HEAD_DIM:(hh + 1) * HEAD_DIM]
                p_ref[hh, rs, :] = fn(y, hh, rs).astype(p_ref.dtype)
                if keep_f32:
                    kc_ref[hh, rs, :] = y

    @pl.when(j == 0)
    def _():
        emit([normed] * 4, first=True)

    @pl.when((j >= 1) & (j < 4))
    def _():
        emit([normed] * 4)

    @pl.when((j >= 4) & (j < TILE_Q))
    def _():
        emit([raw] * 4)

    @pl.when((j >= TILE_Q) & (j < TILE_CMP_KV))
    def _():
        emit([roped] * 4)

    @pl.when(j == TILE_CMP_KV)
    def _():
        emit([raw] * 4, keep_f32=True)

    @pl.when(j > TILE_CMP_KV)
    def _():
        emit([roped, roped, raw, raw])


def _proj(x, g, w_main, w_small, gains, cos, s1, s2, bias, layer, seq, tm=1024):
    n, d = x.shape
    tm = min(tm, seq)
    tpb = seq // tm
    hp = HEADS_PER_TILE
    return pl.pallas_call(
        _proj_body,
        grid=(n // tm, N_PROJ_TILES),
        in_specs=[
            pl.BlockSpec((tm, d), lambda i, j: (i, 0)),
            pl.BlockSpec((None, 1, d), lambda i, j: (layer, 0, 0)),
            pl.BlockSpec((None, d, PROJ_TN), lambda i, j: (layer, 0, j)),
            pl.BlockSpec((None, d, HEAD_DIM), lambda i, j: (layer, 0, 0)),
            pl.BlockSpec((None, None, 1, PROJ_TN), lambda i, j: (layer, j, 0, 0)),
            pl.BlockSpec((tm, HEAD_DIM), lambda i, j: (i % tpb, 0)),
            pl.BlockSpec((tm, HEAD_DIM), lambda i, j: (i % tpb, 0)),
            pl.BlockSpec((tm, HEAD_DIM), lambda i, j: (i % tpb, 0)),
            pl.BlockSpec((None, 1, HEAD_DIM), lambda i, j: (layer, 0, 0)),
        ],
        out_specs=[
            pl.BlockSpec((hp, tm, HEAD_DIM), lambda i, j: (j, i, 0)),
            pl.BlockSpec((hp, tm, HEAD_DIM), lambda i, j: (0, i, 0)),
            pl.BlockSpec((N_KV_NSA, tm, HEAD_DIM), lambda i, j: (0, i, 0)),
        ],
        out_shape=[
            jax.ShapeDtypeStruct((N_SLABS, n, HEAD_DIM), BF16),
            jax.ShapeDtypeStruct((hp, n, HEAD_DIM), F32),
            jax.ShapeDtypeStruct((N_KV_NSA, n, HEAD_DIM), F32),
        ],
        scratch_shapes=[pltpu.VMEM((tm, d), BF16)],
        compiler_params=_cparams(2),
        name="proj",
    )(x, g, w_main, w_small, gains, cos, s1, s2, bias)


def _cumsum_body(x_ref, ct_ref):
    x = x_ref[0, 0]
    t = x.shape[0]
    row = lax.broadcasted_iota(jnp.int32, x.shape, 0)
    s = 1
    while s < t:
        x = x + jnp.where(row >= s, pltpu.roll(x, s, 0), 0.0)
        s *= 2
    ct_ref[0] = x.T[LANE_LOGF:LANE_LOGF + N_HEADS_FOX, :] * LOG2E


def _cumsum(sm4):
    _, b, t, _ = sm4.shape
    return pl.pallas_call(
        _cumsum_body,
        grid=(b,),
        in_specs=[pl.BlockSpec((1, 1, t, HEAD_DIM), lambda i: (0, i, 0, 0))],
        out_specs=pl.BlockSpec((1, N_HEADS_FOX, t), lambda i: (i, 0, 0)),
        out_shape=jax.ShapeDtypeStruct((b, N_HEADS_FOX, t), F32),
        compiler_params=_cparams(1),
        name="fox_cumsum",
    )(sm4)


def _gelu_tanh(x):
    c = float(np.sqrt(2.0 / np.pi))
    return x * (0.5 * (1.0 + jnp.tanh(c * (x + 0.044715 * (x * x * x)))))


def _compress_body(k_ref, pos_ref, w1_ref, w2_ref, kn_ref, cos_ref, s1_ref, s2_ref, o_ref):
    is_key = pl.program_id(0) == 0
    nk = k_ref.shape[0]
    nr = o_ref.shape[-2]
    half = CMP_STRIDE * HEAD_DIM
    a = jnp.zeros((nk * nr, CMP_HIDDEN), F32)
    bm = jnp.zeros((nk * nr, CMP_HIDDEN), F32)
    for l in range(CMP_STRIDE):
        tok = jnp.concatenate([k_ref[kh, 0, pl.ds(l, nr, stride=CMP_STRIDE), :] for kh in range(nk)], axis=0)
        sl = slice(l * HEAD_DIM, (l + 1) * HEAD_DIM)
        a = a + _dot((tok + pos_ref[0][:, sl]).astype(BF16), w1_ref[sl, :])
        bm = bm + _dot((tok + pos_ref[1][:, sl]).astype(BF16), w1_ref[half + l * HEAD_DIM:half + (l + 1) * HEAD_DIM, :])
    nxt = jnp.concatenate([pltpu.roll(bm[kh * nr:(kh + 1) * nr], nr - 1, 0) for kh in range(nk)], axis=0)
    y = _dot(_gelu_tanh(a + nxt).astype(BF16), w2_ref[...])

    @pl.when(is_key)
    def _():
        yk = _row_rms(y, HEAD_DIM) * kn_ref[...]
        for kh in range(nk):
            rs = slice(kh * nr, (kh + 1) * nr)
            o_ref[0, 0, kh] = _rope(yk[rs], cos_ref[...], s1_ref[...], s2_ref[...]).astype(o_ref.dtype)

    @pl.when(jnp.logical_not(is_key))
    def _():
        for kh in range(nk):
            o_ref[0, 0, kh] = y[kh * nr:(kh + 1) * nr].astype(o_ref.dtype)


def _compress(kr, pos, w1, w2, knorm0, cos_c, s1_c, s2_c, layer):
    _, b, t, _ = kr.shape
    nr = t // CMP_STRIDE
    width = CMP_STRIDE * HEAD_DIM
    nk = N_KV_NSA
    return pl.pallas_call(
        _compress_body,
        grid=(2, b),
        in_specs=[
            pl.BlockSpec((nk, 1, t, HEAD_DIM), lambda ty, i: (ty, i, 0, 0)),
            pl.BlockSpec((None, None, 2, 1, width), lambda ty, i: (layer, ty, 0, 0, 0)),
            pl.BlockSpec((None, None, 2 * width, CMP_HIDDEN), lambda ty, i: (layer, ty, 0, 0)),
            pl.BlockSpec((None, None, CMP_HIDDEN, HEAD_DIM), lambda ty, i: (layer, ty, 0, 0)),
            pl.BlockSpec((None, 1, HEAD_DIM), lambda ty, i: (layer, 0, 0)),
            pl.BlockSpec((nr, HEAD_DIM), lambda ty, i: (0, 0)),
            pl.BlockSpec((nr, HEAD_DIM), lambda ty, i: (0, 0)),
            pl.BlockSpec((nr, HEAD_DIM), lambda ty, i: (0, 0)),
        ],
        out_specs=pl.BlockSpec((1, 1, nk, nr, HEAD_DIM), lambda ty, i: (ty, i, 0, 0, 0)),
        out_shape=jax.ShapeDtypeStruct((2, b, nk, nr, HEAD_DIM), BF16),
        compiler_params=_cparams(2),
        name="compress",
    )(kr, pos, w1, w2, knorm0, cos_c, s1_c, s2_c)


def _split3(x):
    hi = x.astype(BF16)
    r1 = x - hi.astype(F32)
    mid = r1.astype(BF16)
    lo = (r1 - mid.astype(F32)).astype(BF16)
    return hi, mid, lo


def _cmp_branch(q, kc, vc, ovt, i, tq, n_blk):
    rows = q.shape[0]
    g = rows // tq
    s = _dot_nt(q, kc)
    row_g = lax.broadcasted_iota(jnp.int32, (rows, HEAD_DIM), 0)
    lane_g = lax.broadcasted_iota(jnp.int32, (rows, HEAD_DIM), 1)
    t_g = i * tq + (row_g & (tq - 1))
    valid = (lane_g * CMP_STRIDE + (CMP_BLOCK - 1)) <= t_g
    s = jnp.where(valid, s, NEG_INF)
    m = jnp.max(s, axis=-1, keepdims=True)
    p = jnp.where(valid, jnp.exp2(s - m), 0.0)
    p = p / jnp.maximum(jnp.sum(p, axis=-1, keepdims=True), 1e-30)
    o = _dot(p.astype(BF16), vc)
    psum = p[0:tq]
    for gg in range(1, g):
        psum = psum + p[gg * tq:(gg + 1) * tq]

    hi, mid, lo = _split3(psum)
    imp = ((_dot_nt(ovt, hi) + _dot_nt(ovt, mid)) + _dot_nt(ovt, lo))[:n_blk]
    blk = lax.broadcasted_iota(jnp.int32, (n_blk, tq), 0)
    t_blk = (i * tq + lax.broadcasted_iota(jnp.int32, (n_blk, tq), 1)) // SLC_BLOCK
    causal = blk <= t_blk
    forced = (blk == 0) | (causal & (blk > t_blk - N_LOCAL_SLC))
    score = jnp.where(forced, FORCED_SCORE, jnp.where(causal, imp, NEG_INF))
    cnt = jnp.zeros((n_blk, tq), jnp.int32)
    for c in range(n_blk):
        other = score[c:c + 1, :]
        ahead = (other > score) | ((other == score) & (blk > c))
        cnt = cnt + ahead.astype(jnp.int32)
    picked = jnp.where(cnt < SLC_TOPK, 1.0, 0.0)
    picked = jnp.concatenate([picked, jnp.zeros((HEAD_DIM - n_blk, tq), F32)], axis=0)
    return o, picked.T.astype(BF16)


def _online_update(s, v, m_ref, l_ref, acc_ref, rows=slice(None)):
    cols = [s[:, c:c + HEAD_DIM] for c in range(0, s.shape[1], HEAD_DIM)]
    m_el = functools.reduce(jnp.maximum, cols)
    m_prev = m_ref[rows]
    m_new = jnp.maximum(m_prev, jnp.max(m_el, axis=-1, keepdims=True))
    alpha = jnp.exp2(m_prev - m_new)
    ps = [jnp.exp2(c - m_new) for c in cols]
    l_ref[rows] = alpha * l_ref[rows] + functools.reduce(jnp.add, ps)
    p = jnp.concatenate([x.astype(BF16) for x in ps], axis=1)
    acc_ref[rows] = alpha * acc_ref[rows] + _dot(p, v)
    m_ref[rows] = m_new


def _normalize(acc_ref, l_ref):
    l = jnp.sum(l_ref[...], axis=-1, keepdims=True)
    return acc_ref[...] / jnp.maximum(l, 1e-30)


def _init_online(m_ref, l_ref, acc_ref):
    m_ref[...] = jnp.full(m_ref.shape, NEG_INF, F32)
    l_ref[...] = jnp.zeros(l_ref.shape, F32)
    acc_ref[...] = jnp.zeros(acc_ref.shape, F32)


def _flash_scratch(rows):
    return [pltpu.VMEM((rows, HEAD_DIM), F32)] * 3


def _win_branch(q, k_ref, v_ref, kh, i, tq, band):
    rows = q.shape[0]
    g = rows // tq
    start = pl.multiple_of(jnp.maximum(i * tq - WINDOW, 0), tq)
    k = k_ref[kh, 0, pl.ds(start, band), :]
    v = v_ref[kh, 0, pl.ds(start, band), :]
    row = lax.broadcasted_iota(jnp.int32, (tq, band), 0)
    col = lax.broadcasted_iota(jnp.int32, (tq, band), 1)
    kpos = start + col
    tpos = i * tq + row
    live = (kpos <= tpos) & (kpos > tpos - WINDOW)
    bias = jnp.where(live, 0.0, NEG_INF)
    s = (_dot_nt(q, k).reshape(g, tq, band) + bias[None]).reshape(rows, band)
    m = jnp.max(s, axis=-1, keepdims=True)
    p = jnp.exp2(s - m)
    l = jnp.sum(p, axis=-1, keepdims=True)
    return _dot(p.astype(BF16), v) / jnp.maximum(l, 1e-30)


def _nsa_body(q_ref, kc_ref, vc_ref, ks_ref, vs_ref, kw_ref, vw_ref, gt_ref, ov_ref, e_ref, o_ref,
              m_ref, l_ref, acc_ref, part_ref, sel_ref, *, tq, tk, band, n_blk):
    i = pl.program_id(1)
    g = GQA_GROUP
    rows = g * tq

    def queries(kh):
        return q_ref[kh * g:(kh + 1) * g].reshape(rows, HEAD_DIM)

    def gate(kh, branch):
        gt = gt_ref[kh, 0]
        return jnp.concatenate(
            [jnp.broadcast_to(gt[:, 3 * gg + branch:3 * gg + branch + 1], (tq, HEAD_DIM)) for gg in range(g)],
            axis=0)

    for kh in range(N_KV_NSA):
        rs = slice(kh * rows, (kh + 1) * rows)
        q = queries(kh)
        o_cmp, sel = _cmp_branch(q, kc_ref[0, 0, kh], vc_ref[0, 0, kh], ov_ref[...], i, tq, n_blk)
        sel_ref[kh] = sel
        part_ref[rs] = gate(kh, 0) * o_cmp + gate(kh, 2) * _win_branch(q, kw_ref, vw_ref, kh, i, tq, band)

    _init_online(m_ref, l_ref, acc_ref)
    row = lax.broadcasted_iota(jnp.int32, (tq, tk), 0)
    col = lax.broadcasted_iota(jnp.int32, (tq, tk), 1)
    last = (i * tq) // tk

    def step(kc, diagonal):
        start = pl.multiple_of(kc * tk, tk)
        for kh in range(N_KV_NSA):
            k = ks_ref[kh, 0, pl.ds(start, tk), :]
            v = vs_ref[kh, 0, pl.ds(start, tk), :]
            live = _dot(sel_ref[kh], e_ref[kc]) > 0.5
            if diagonal:
                live = live & ((start + col) <= (i * tq + row))
            bias = jnp.where(live, 0.0, NEG_INF)
            s = _dot_nt(queries(kh), k).reshape(g, tq, tk) + bias[None]
            _online_update(s.reshape(rows, tk), v, m_ref, l_ref, acc_ref, slice(kh * rows, (kh + 1) * rows))

    def body(kc, carry):
        step(kc, False)
        return carry

    lax.fori_loop(0, last, body, 0)
    step(last, True)
    o_slc = _normalize(acc_ref, l_ref)
    for kh in range(N_KV_NSA):
        rs = slice(kh * rows, (kh + 1) * rows)
        y = part_ref[rs] + gate(kh, 1) * o_slc[rs]
        for gg in range(g):
            hq = kh * g + gg
            o_ref[0, :, hq * HEAD_DIM:(hq + 1) * HEAD_DIM] = y[gg * tq:(gg + 1) * tq]


def _nsa(p4, kvc, sm4, ovt, e, tq=256):
    _, b, t, _ = p4.shape
    nr = kvc.shape[3]
    tk = e.shape[-1]
    band = min(WINDOW + tq, t)
    nk = N_KV_NSA
    rows = N_HEADS_NSA * tq

    def kv(slab):
        return pl.BlockSpec((nk, 1, t, HEAD_DIM), lambda bb, i: (slab // nk, bb, 0, 0))

    return pl.pallas_call(
        functools.partial(_nsa_body, tq=tq, tk=tk, band=band, n_blk=t // SLC_BLOCK),
        grid=(b, t // tq),
        in_specs=[
            pl.BlockSpec((N_HEADS_NSA, None, tq, HEAD_DIM), lambda bb, i: (SLAB_Q // N_HEADS_NSA, bb, i, 0)),
            pl.BlockSpec((1, 1, nk, nr, HEAD_DIM), lambda bb, i: (0, bb, 0, 0, 0)),
            pl.BlockSpec((1, 1, nk, nr, HEAD_DIM), lambda bb, i: (1, bb, 0, 0, 0)),
            kv(SLAB_KV + 4), kv(SLAB_KV + 6), kv(SLAB_KV + 8), kv(SLAB_KV + 10),
            pl.BlockSpec((nk, 1, tq, HEAD_DIM), lambda bb, i: (0, bb, i, 0)),
            pl.BlockSpec((HEAD_DIM, HEAD_DIM), lambda bb, i: (0, 0)),
            pl.BlockSpec(e.shape, lambda bb, i: (0, 0, 0)),
        ],
        out_specs=pl.BlockSpec((1, tq, NSA_WIDTH), lambda bb, i: (bb, i, 0)),
        out_shape=jax.ShapeDtypeStruct((b, t, NSA_WIDTH), F32),
        scratch_shapes=_flash_scratch(rows) + [pltpu.VMEM((rows, HEAD_DIM), F32),
                                               pltpu.VMEM((nk, tq, HEAD_DIM), BF16)],
        compiler_params=_cparams(2),
        name="nsa",
    )(p4, kvc, kvc, p4, p4, p4, p4, sm4, ovt, e)


def _fox_body(q_ref, k_ref, v_ref, ct_ref, o_ref, m_ref, acc_ref, *, tq):
    i = pl.program_id(2)
    nh = q_ref.shape[0]
    m_ref[...] = jnp.full(m_ref.shape, NEG_INF, F32)
    acc_ref[...] = jnp.zeros(acc_ref.shape, F32)
    row = lax.broadcasted_iota(jnp.int32, (tq, tq), 0)
    col = lax.broadcasted_iota(jnp.int32, (tq, tq), 1)
    ones_col = jnp.where(lax.broadcasted_iota(jnp.int32, (tq, HEAD_DIM), 1) == 0, 1.0, 0.0).astype(BF16)

    def step(kc, diagonal):
        start = pl.multiple_of(kc * tq, tq)
        for hh in range(nh):
            rs = slice(hh * tq, (hh + 1) * tq)
            k = k_ref[hh, 0, pl.ds(start, tq), :]
            v1 = jnp.concatenate([v_ref[hh, 0, pl.ds(start, tq), :], ones_col], axis=1)
            s = _dot_nt(q_ref[hh], k) - ct_ref[hh, pl.ds(kc, 1), :]
            if diagonal:
                s = jnp.where(col <= row, s, NEG_INF)
            cols = [s[:, c:c + HEAD_DIM] for c in range(0, tq, HEAD_DIM)]
            m_prev = m_ref[rs]
            m_new = jnp.maximum(m_prev, jnp.max(functools.reduce(jnp.maximum, cols), axis=-1, keepdims=True))
            alpha = jnp.exp2(m_prev - m_new)
            p = jnp.concatenate([jnp.exp2(c - m_new).astype(BF16) for c in cols], axis=1)
            upd = _dot(p, v1)
            for half in (slice(0, HEAD_DIM), slice(HEAD_DIM, 2 * HEAD_DIM)):
                acc_ref[rs, half] = alpha * acc_ref[rs, half] + upd[:, half]
            m_ref[rs] = m_new

    def body(kc, carry):
        step(kc, False)
        return carry

    lax.fori_loop(0, i, body, 0)
    step(i, True)
    acc = acc_ref[...]
    o = acc[:, :HEAD_DIM] / jnp.maximum(acc[:, HEAD_DIM:HEAD_DIM + 1], 1e-30)
    for hh in range(nh):
        o_ref[0, :, hh * HEAD_DIM:(hh + 1) * HEAD_DIM] = o[hh * tq:(hh + 1) * tq]


FOX_HEADS_PER_STEP = 8
assert SLAB_FOX % FOX_HEADS_PER_STEP == 0 and N_HEADS_FOX % FOX_HEADS_PER_STEP == 0


def _fox_attn(p4, ctr, tq=512):
    _, b, t, _ = p4.shape
    nh = N_HEADS_FOX
    hs = FOX_HEADS_PER_STEP
    q0, k0, v0 = (SLAB_FOX // hs, (SLAB_FOX + nh) // hs, (SLAB_FOX + 2 * nh) // hs)
    return pl.pallas_call(
        functools.partial(_fox_body, tq=tq),
        grid=(b, nh // hs, t // tq),
        in_specs=[
            pl.BlockSpec((hs, None, tq, HEAD_DIM), lambda bb, h, i: (q0 + h, bb, i, 0)),
            pl.BlockSpec((hs, 1, t, HEAD_DIM), lambda bb, h, i: (k0 + h, bb, 0, 0)),
            pl.BlockSpec((hs, 1, t, HEAD_DIM), lambda bb, h, i: (v0 + h, bb, 0, 0)),
            pl.BlockSpec((hs, t // tq, tq), lambda bb, h, i: (bb * (nh // hs) + h, 0, 0)),
        ],
        out_specs=pl.BlockSpec((1, tq, hs * HEAD_DIM), lambda bb, h, i: (bb, i, h)),
        out_shape=jax.ShapeDtypeStruct((b, t, FOX_WIDTH), F32),
        scratch_shapes=[pltpu.VMEM((hs * tq, HEAD_DIM), F32), pltpu.VMEM((hs * tq, 2 * HEAD_DIM), F32)],
        compiler_params=_cparams(3),
        name="fox_attn",
    )(p4, p4, p4, ctr)


def _out_body(on_ref, of_ref, x_ref, nn_ref, fn_ref, w_ref, o_ref):
    nsa = (_row_rms(on_ref[...], NSA_WIDTH) * nn_ref[...]).astype(BF16)
    fox = (_row_rms(of_ref[...], FOX_WIDTH) * fn_ref[...]).astype(BF16)
    o_ref[...] = x_ref[...] + (_dot(nsa, w_ref[:NSA_WIDTH, :]) + _dot(fox, w_ref[NSA_WIDTH:, :]))


def _out(on, of, x, nn, fn, w, layer, tm=512):
    n, d = x.shape
    return pl.pallas_call(
        _out_body,
        grid=(n // tm,),
        in_specs=[
            pl.BlockSpec((tm, NSA_WIDTH), lambda i: (i, 0)),
            pl.BlockSpec((tm, FOX_WIDTH), lambda i: (i, 0)),
            pl.BlockSpec((tm, d), lambda i: (i, 0)),
            pl.BlockSpec((None, 1, NSA_WIDTH), lambda i: (layer, 0, 0)),
            pl.BlockSpec((None, 1, FOX_WIDTH), lambda i: (layer, 0, 0)),
            pl.BlockSpec((None, NSA_WIDTH + FOX_WIDTH, d), lambda i: (layer, 0, 0)),
        ],
        out_specs=pl.BlockSpec((tm, d), lambda i: (i, 0)),
        out_shape=jax.ShapeDtypeStruct((n, d), F32),
        compiler_params=_cparams(1),
        name="out_proj",
    )(on, of, x, nn, fn, w)


def _rope_tables(pos):
    inv = ROPE_THETA ** (-jnp.arange(0, ROPE_DIM, 2, dtype=F32) / ROPE_DIM)
    ang = pos.astype(F32)[:, None] * inv[None, :]
    cos, sin = jnp.cos(ang), jnp.sin(ang)
    n = pos.shape[0]
    pad = HEAD_DIM - ROPE_DIM
    cos_t = jnp.concatenate([cos, cos, jnp.ones((n, pad), F32)], axis=-1)
    s1 = jnp.concatenate([-sin, jnp.zeros((n, HEAD_DIM - ROPE_HALF), F32)], axis=-1)
    s2 = jnp.concatenate([jnp.zeros((n, ROPE_HALF), F32), sin, jnp.zeros((n, pad), F32)], axis=-1)
    return cos_t, s1, s2


def _overlap_matrix(t):
    n_cmp = (t - CMP_BLOCK) // CMP_STRIDE + 1
    n_slc = t // SLC_BLOCK
    cmp_start = np.arange(HEAD_DIM) * CMP_STRIDE
    slc_start = np.arange(HEAD_DIM) * SLC_BLOCK
    ov = ((cmp_start[:, None] < slc_start[None, :] + SLC_BLOCK)
          & (cmp_start[:, None] + CMP_BLOCK > slc_start[None, :]))
    ov = ov & (np.arange(HEAD_DIM)[:, None] < n_cmp) & (np.arange(HEAD_DIM)[None, :] < n_slc)
    return jnp.asarray(ov.T.astype(np.float32), BF16)


def _expand_matrix(t, tk):
    kpos = np.arange(t).reshape(t // tk, 1, tk)
    j = np.arange(HEAD_DIM).reshape(1, HEAD_DIM, 1)
    return jnp.asarray((kpos // SLC_BLOCK == j).astype(np.float32), BF16)


def _mixer(x2, b, t, layer, wts, tabs):
    n = x2.shape[0]
    p3, kc32, sm = _proj(x2, wts["mix_norm"], wts["w_main"], wts["w_small"], wts["gains"],
                         tabs["cos"], tabs["s1"], tabs["s2"], wts["fbias"], layer, t)
    p4 = p3.reshape(N_SLABS, b, t, HEAD_DIM)
    kr = kc32.reshape(HEADS_PER_TILE, b, t, HEAD_DIM)
    sm4 = sm.reshape(N_KV_NSA, b, t, HEAD_DIM)
    ct = _cumsum(sm4)
    fox_tq = min(512, t)
    ctr = ct.reshape(b * N_HEADS_FOX, t // fox_tq, fox_tq)
    kvc = _compress(kr, wts["pos"], wts["cmp_w1"], wts["cmp_w2"], wts["knorm0"],
                    tabs["cos_c"], tabs["s1_c"], tabs["s2_c"], layer)
    o_nsa = _nsa(p4, kvc, sm4, tabs["ov"], tabs["e"])
    o_fox = _fox_attn(p4, ctr, fox_tq)
    return _out(o_nsa.reshape(n, NSA_WIDTH), o_fox.reshape(n, FOX_WIDTH), x2, wts["nsa_out_norm"],
                wts["fox_out_norm"], wts["w_out"], layer)


def _prep_weights(ffn1_norm, ffn1_w_gate, ffn1_w_up, ffn1_w_down, mix_norm, w_in, nsa_q_norm, nsa_k_norm,
                  cmp_pos_emb, cmp_w1, cmp_w2, nsa_out_norm, fox_q_norm, fox_k_norm, fox_forget_bias,
                  fox_out_norm, w_out, ffn2_norm, ffn2_w_gate, ffn2_w_up, ffn2_w_down):
    depth, d, _ = w_in.shape
    c0 = COLS_NSA_Q + COLS_NSA_KV
    c1 = c0 + COLS_NSA_GATE
    c2 = c1 + COLS_FOX_QKV
    w16 = w_in.astype(BF16)
    w_small = jnp.concatenate(
        [w16[:, :, c0:c1], w16[:, :, c2:], jnp.zeros((depth, d, HEAD_DIM - COLS_NSA_GATE - COLS_FOX_F), BF16)],
        axis=-1)
    w_main = jnp.concatenate([w16[:, :, c1:c2], w16[:, :, :c0]], axis=-1)

    def tile(v):
        return jnp.concatenate([v] * HEADS_PER_TILE, axis=-1)

    one = jnp.ones((depth, HEAD_DIM), F32)
    gains = [tile(one)] * N_PROJ_TILES
    gains[0] = gains[1] = tile(fox_q_norm * Q_SCALE)
    gains[2] = gains[3] = tile(fox_k_norm)
    gains[TILE_Q] = gains[TILE_Q + 1] = tile(nsa_q_norm * Q_SCALE)
    gains[TILE_CMP_KV + 1] = jnp.concatenate([nsa_k_norm[:, 1]] * 2 + [one] * 2, axis=-1)
    gains[TILE_CMP_KV + 2] = jnp.concatenate([nsa_k_norm[:, 2]] * 2 + [one] * 2, axis=-1)
    gains = jnp.stack(gains, axis=1)[:, :, None, :]

    fbias = jnp.zeros((depth, 1, HEAD_DIM), F32).at[:, 0, LANE_LOGF:LANE_LOGF + N_HEADS_FOX].set(
        fox_forget_bias)
    half = CMP_BLOCK * HEAD_DIM // 2
    pos = cmp_pos_emb.reshape(depth, 2, 2, 1, half)
    return dict(
        ffn1_norm=ffn1_norm[:, None, :], ffn2_norm=ffn2_norm[:, None, :],
        ffn1=(ffn1_w_gate, ffn1_w_up, ffn1_w_down), ffn2=(ffn2_w_gate, ffn2_w_up, ffn2_w_down),
        mix_norm=mix_norm[:, None, :], w_main=w_main, w_small=w_small, gains=gains, fbias=fbias, pos=pos,
        cmp_w1=cmp_w1.astype(BF16), cmp_w2=cmp_w2.astype(BF16), knorm0=nsa_k_norm[:, 0][:, None, :],
        nsa_out_norm=nsa_out_norm[:, None, :], fox_out_norm=fox_out_norm[:, None, :],
        w_out=w_out.astype(BF16),
    )


def _tables(t):
    cos, s1, s2 = _rope_tables(jnp.arange(t))
    cmp_end = jnp.arange(t // CMP_STRIDE) * CMP_STRIDE + (CMP_BLOCK - 1)
    cos_c, s1_c, s2_c = _rope_tables(cmp_end)
    return dict(cos=cos, s1=s1, s2=s2, cos_c=cos_c, s1_c=s1_c, s2_c=s2_c,
                ov=_overlap_matrix(t), e=_expand_matrix(t, min(512, t)))


def kernel(x, ffn1_norm, ffn1_w_gate, ffn1_w_up, ffn1_w_down, mix_norm, w_in, nsa_q_norm, nsa_k_norm, cmp_pos_emb, cmp_w1, cmp_w2, nsa_out_norm, fox_q_norm, fox_k_norm, fox_forget_bias, fox_out_norm, w_out, ffn2_norm, ffn2_w_gate, ffn2_w_up, ffn2_w_down):
    b, t, d = x.shape
    depth = w_in.shape[0]
    wts = _prep_weights(ffn1_norm, ffn1_w_gate, ffn1_w_up, ffn1_w_down, mix_norm, w_in, nsa_q_norm,
                        nsa_k_norm, cmp_pos_emb, cmp_w1, cmp_w2, nsa_out_norm, fox_q_norm, fox_k_norm,
                        fox_forget_bias, fox_out_norm, w_out, ffn2_norm, ffn2_w_gate, ffn2_w_up, ffn2_w_down)
    tabs = _tables(t)
    x2 = x.reshape(b * t, d)
    w16 = tuple(_cast_layer(w, 0) for w in wts["ffn1"])
    for layer in range(depth):
        x2, w16 = _ffn(x2, wts["ffn1_norm"], *w16, layer, nxt=wts["ffn2"] + (layer,))
        x2 = _mixer(x2, b, t, layer, wts, tabs)
        nxt = wts["ffn1"] + (layer + 1,) if layer + 1 < depth else None
        x2, w16 = _ffn(x2, wts["ffn2_norm"], *w16, layer, nxt=nxt)
    return x2.reshape(b, t, d)
```
